```python
import jax, jax.numpy as jnp
from jax import lax
import numpy as np

D_MODEL = 1024
BATCH = 16
SEQ = 2048
DEPTH = 1
DEC_BATCH = 128
DEC_SEQ = 1
PAST_LEN = 8192
PAGE_SIZE = 128

MLA_HEADS = 8
MLA_NOPE = 64
MLA_ROPE = 32
MLA_V = 64
MLA_Q_RANK = 256
MLA_KV_RANK = 256
ROPE_THETA = 10000.0
NSA_HEADS = 8
NSA_KV_HEADS = 2
NSA_GQA = NSA_HEADS // NSA_KV_HEADS
NSA_HEAD_DIM = 64
CMP_BLOCK = 32
CMP_STRIDE = 16
CMP_HIDDEN = 64
SLC_BLOCK = 64
SLC_TOP_N = 16
N_LOCAL_SLC = 2
WINDOW = 512
FORCE_SCORE = 1.0e4
N_GROUPS = 8
EXPERTS_PER_GROUP = 8
N_EXPERTS = N_GROUPS * EXPERTS_PER_GROUP
TOP_K = 2
D_EXPERT = 256
MOE_BLOCK = 128
Q_BLOCK = 128
SLC_Q_BLOCK = 32
LN_EPS = 1e-5
RMS_EPS = 1e-6
DEEPNORM_ALPHA = (2.0 * DEPTH) ** 0.25
DEEPNORM_BETA = (8.0 * DEPTH) ** -0.25

NSA_KV_COLS = 2 * NSA_KV_HEADS * NSA_HEAD_DIM
IN_SPLITS = (MLA_Q_RANK, MLA_KV_RANK, MLA_ROPE, NSA_HEADS * NSA_HEAD_DIM,
             NSA_KV_COLS, NSA_KV_COLS, NSA_KV_COLS, 3 * NSA_HEADS, D_MODEL, D_MODEL)
IN_OFFSETS = tuple(int(v) for v in np.cumsum(IN_SPLITS)[:-1])
D_IN = int(sum(IN_SPLITS))

kernel_name = 'mla_nsa_gated_hmoe_deepnorm_step'


def layer_norm(x, g, b):
    xf = x.astype(jnp.float32)
    mu = jnp.mean(xf, -1, keepdims=True)
    var = jnp.mean(jnp.square(xf - mu), -1, keepdims=True)
    return ((xf - mu) * lax.rsqrt(var + LN_EPS) * g + b).astype(x.dtype)


def rms_norm(x, g):
    xf = x.astype(jnp.float32)
    return (xf * lax.rsqrt(jnp.mean(xf * xf, -1, keepdims=True) + RMS_EPS) * g).astype(x.dtype)


def rope(x, pos):
    half = x.shape[-1] // 2
    freqs = ROPE_THETA ** (-jnp.arange(half, dtype=jnp.float32) / half)
    ang = pos.astype(jnp.float32)[..., None] * freqs
    cos, sin = jnp.cos(ang), jnp.sin(ang)
    xf = x.astype(jnp.float32)
    x1, x2 = xf[..., :half], xf[..., half:]
    return jnp.concatenate([x1 * cos - x2 * sin, x1 * sin + x2 * cos], -1).astype(x.dtype)


def masked_softmax(s, mask):
    s = jnp.where(mask, s, -jnp.inf)
    m = jnp.max(s, axis=-1, keepdims=True)
    m = jnp.where(jnp.isfinite(m), m, 0.0)
    e = jnp.exp(s - m)
    d = jnp.sum(e, axis=-1, keepdims=True)
    return e / jnp.where(d > 0.0, d, 1.0)


def alibi_slopes():
    h = jnp.arange(1, NSA_HEADS + 1, dtype=jnp.float32)
    return (2.0 ** (-8.0 * h / NSA_HEADS)).reshape(NSA_KV_HEADS, NSA_GQA)


def gather_pages(pool, layer, page_table):
    g = pool[layer, page_table]
    return g.reshape(g.shape[0], g.shape[1] * g.shape[2], *g.shape[3:])


def in_proj(x, P):
    B, S = x.shape[:2]
    cq, ckv, kr, q, kvc, kvs, kvw, gn, ga, gb = jnp.split(x @ P['w_in'], IN_OFFSETS, axis=-1)
    kv_shape = (B, S, 2, NSA_KV_HEADS, NSA_HEAD_DIM)
    return dict(cq=cq, ckv=ckv, kr=kr, q=q.reshape(B, S, NSA_HEADS, NSA_HEAD_DIM),
                kv_cmp=kvc.reshape(kv_shape), kv_slc=kvs.reshape(kv_shape), kv_win=kvw.reshape(kv_shape),
                g_nsa=jax.nn.sigmoid(gn.reshape(B, S, NSA_HEADS, 3)),
                g_a=jax.nn.sigmoid(ga), g_b=jax.nn.sigmoid(gb))


def mla_latent(z, pos, P):
    cq = rms_norm(z['cq'], P['mla_g_q'])
    q = jnp.einsum('bsr,rhd->bshd', cq, P['mla_w_uq'])
    q_nope = q[..., :MLA_NOPE]
    q_rope = rope(q[..., MLA_NOPE:], pos[:, None])
    ckv = rms_norm(z['ckv'], P['mla_g_kv'])
    krope = rope(z['kr'], pos)
    return q_nope, q_rope, ckv, krope


def mla_prompt_attn(q_nope, q_rope, ckv, krope, P):
    B, S = ckv.shape[:2]
    k_nope = jnp.einsum('bsc,chd->bshd', ckv, P['mla_w_uk'])
    v = jnp.einsum('bsc,chd->bshd', ckv, P['mla_w_uv'])
    scale = (MLA_NOPE + MLA_ROPE) ** -0.5
    kpos = jnp.arange(S)

    def block(i):
        q0 = i * Q_BLOCK
        qn = lax.dynamic_slice_in_dim(q_nope, q0, Q_BLOCK, axis=1)
        qr = lax.dynamic_slice_in_dim(q_rope, q0, Q_BLOCK, axis=1)
        s = (jnp.einsum('bqhd,bkhd->bhqk', qn, k_nope)
             + jnp.einsum('bqhd,bkd->bhqk', qr, krope)).astype(jnp.float32) * scale
        qpos = q0 + jnp.arange(Q_BLOCK)
        p = masked_softmax(s, kpos[None, :] <= qpos[:, None])
        return jnp.einsum('bhqk,bkhd->bqhd', p.astype(v.dtype), v)

    o = lax.map(block, jnp.arange(S // Q_BLOCK))
    return o.swapaxes(0, 1).reshape(B, S, MLA_HEADS * MLA_V)


def mla_sample_attn(q_nope, q_rope, ckv_all, krope_all, qpos, P):
    B, T = q_nope.shape[:2]
    L = ckv_all.shape[1]
    scale = (MLA_NOPE + MLA_ROPE) ** -0.5
    q_lat = jnp.einsum('bqhd,chd->bqhc', q_nope, P['mla_w_uk'])
    s = (jnp.einsum('bqhc,bkc->bhqk', q_lat, ckv_all)
         + jnp.einsum('bqhd,bkd->bhqk', q_rope, krope_all)).astype(jnp.float32) * scale
    p = masked_softmax(s, jnp.arange(L)[None, :] <= qpos[:, None])
    o_lat = jnp.einsum('bhqk,bkc->bqhc', p.astype(ckv_all.dtype), ckv_all)
    return jnp.einsum('bqhc,chd->bqhd', o_lat, P['mla_w_uv']).reshape(B, T, MLA_HEADS * MLA_V)


def gqa_attend(q, k, v, qpos, kpos, slopes, mask):
    B, Q = q.shape[:2]
    qg = q.reshape(B, Q, NSA_KV_HEADS, NSA_GQA, NSA_HEAD_DIM)
    dist = (qpos[:, None] - kpos[None, :]).astype(jnp.float32)
    s = (jnp.einsum('bqkgd,bskd->bkgqs', qg, k).astype(jnp.float32) * NSA_HEAD_DIM ** -0.5
         - slopes[:, :, None, None] * dist)
    p = masked_softmax(s, mask)
    o = jnp.einsum('bkgqs,bskd->bqkgd', p.astype(v.dtype), v)
    return o.reshape(B, Q, NSA_HEADS, NSA_HEAD_DIM), p


def compress(x, pe, w1, w2):
    B, L = x.shape[:2]
    n16 = L // CMP_STRIDE
    c = x[:, :n16 * CMP_STRIDE].reshape(B, n16, CMP_STRIDE, NSA_KV_HEADS, NSA_HEAD_DIM)
    w1h = w1.reshape(2, CMP_STRIDE, NSA_HEAD_DIM, CMP_HIDDEN)
    first = jnp.einsum('bnpkd,pdh->bnkh', c, w1h[0])
    second = jnp.einsum('bnpkd,pdh->bnkh', c, w1h[1])
    pe_term = jnp.einsum('pd,pdh->h', pe, w1.reshape(CMP_BLOCK, NSA_HEAD_DIM, CMP_HIDDEN))
    h = jax.nn.gelu(first[:, :-1] + second[:, 1:] + pe_term)
    return jnp.einsum('bnkh,hd->bnkd', h, w2)


def cmp_to_slc_weights(n_cmp, n_slc):
    cs = jnp.arange(n_cmp)[:, None] * CMP_STRIDE
    ss = jnp.arange(n_slc)[None, :] * SLC_BLOCK
    inter = jnp.clip(jnp.minimum(cs + CMP_BLOCK, ss + SLC_BLOCK) - jnp.maximum(cs, ss), 0)
    return inter.astype(jnp.float32) / CMP_STRIDE


def select_blocks(imp, qpos, n_slc):
    blk = jnp.arange(n_slc)[None, :]
    cur = (qpos // SLC_BLOCK)[:, None]
    valid = blk <= cur
    forced = (blk == 0) | (valid & (blk > cur - N_LOCAL_SLC))
    score = jnp.where(forced, FORCE_SCORE, jnp.where(valid, imp, -1.0))
    _, idx = lax.top_k(score, min(SLC_TOP_N, n_slc))
    return idx, idx <= cur[None, None]


def slc_attend(q, k, v, idx, sel_valid, qpos, slopes):
    B, L = k.shape[:2]
    Q = q.shape[1]
    n_sel = idx.shape[-1]
    nb = L // SLC_BLOCK
    kb = k.reshape(B, nb, SLC_BLOCK, NSA_KV_HEADS, NSA_HEAD_DIM).transpose(0, 3, 1, 2, 4)
    vb = v.reshape(B, nb, SLC_BLOCK, NSA_KV_HEADS, NSA_HEAD_DIM).transpose(0, 3, 1, 2, 4)
    chunk = SLC_Q_BLOCK if Q % SLC_Q_BLOCK == 0 else Q
    nc = Q // chunk
    bi = jnp.arange(B)[:, None, None, None]
    gi = jnp.arange(NSA_KV_HEADS)[None, :, None, None]
    offs = jnp.arange(SLC_BLOCK)
    m_keys = n_sel * SLC_BLOCK

    def run(args):
        qc, ic, vc, pc = args
        ks = kb[bi, gi, ic].reshape(B, NSA_KV_HEADS, chunk, m_keys, NSA_HEAD_DIM)
        vs = vb[bi, gi, ic].reshape(B, NSA_KV_HEADS, chunk, m_keys, NSA_HEAD_DIM)
        kpos = (ic[..., None] * SLC_BLOCK + offs).reshape(B, NSA_KV_HEADS, chunk, m_keys)
        dist = pc[None, None, :, None] - kpos
        mask = (dist >= 0) & jnp.repeat(vc, SLC_BLOCK, axis=-1)
        qg = qc.reshape(B, chunk, NSA_KV_HEADS, NSA_GQA, NSA_HEAD_DIM)
        s = (jnp.einsum('bqkgd,bkqmd->bkgqm', qg, ks).astype(jnp.float32) * NSA_HEAD_DIM ** -0.5
             - slopes[None, :, :, None, None] * dist[:, :, None].astype(jnp.float32))
        p = masked_softmax(s, mask[:, :, None])
        o = jnp.einsum('bkgqm,bkqmd->bqkgd', p.astype(vs.dtype), vs)
        return o.reshape(B, chunk, NSA_HEADS, NSA_HEAD_DIM)

    args = (q.reshape(B, nc, chunk, NSA_HEADS, NSA_HEAD_DIM).swapaxes(0, 1),
            idx.reshape(B, NSA_KV_HEADS, nc, chunk, n_sel).transpose(2, 0, 1, 3, 4),
            sel_valid.reshape(B, NSA_KV_HEADS, nc, chunk, n_sel).transpose(2, 0, 1, 3, 4),
            qpos.reshape(nc, chunk))
    o = lax.map(run, args)
    return o.swapaxes(0, 1).reshape(B, Q, NSA_HEADS, NSA_HEAD_DIM)


def nsa_cmp_slc(q, kv_cmp, kv_slc, qpos, slopes, P):
    L = kv_cmp.shape[1]
    k_c = compress(kv_cmp[:, :, 0], P['nsa_pe_k'], P['nsa_w1_k'], P['nsa_w2_k'])
    v_c = compress(kv_cmp[:, :, 1], P['nsa_pe_v'], P['nsa_w1_v'], P['nsa_w2_v'])
    n_cmp = k_c.shape[1]
    cmp_end = jnp.arange(n_cmp) * CMP_STRIDE + (CMP_BLOCK - 1)
    o_cmp, p_cmp = gqa_attend(q, k_c, v_c, qpos, cmp_end, slopes, cmp_end[None, :] <= qpos[:, None])
    n_slc = -(-L // SLC_BLOCK)
    imp = jnp.einsum('bkgqn,nj->bkqj', p_cmp, cmp_to_slc_weights(n_cmp, n_slc))
    idx, sel_valid = select_blocks(imp, qpos, n_slc)
    kv_slc = jnp.pad(kv_slc, ((0, 0), (0, n_slc * SLC_BLOCK - L), (0, 0), (0, 0), (0, 0)))
    o_slc = slc_attend(q, kv_slc[:, :, 0], kv_slc[:, :, 1], idx, sel_valid, qpos, slopes)
    return o_cmp, o_slc


def window_mask(qpos, kpos):
    d = qpos[:, None] - kpos[None, :]
    return (d >= 0) & (d <= WINDOW) & (kpos[None, :] >= 0)


def win_prompt(q, kv, slopes):
    B, S = q.shape[:2]
    kvp = jnp.pad(kv, ((0, 0), (WINDOW, 0), (0, 0), (0, 0), (0, 0)))
    span = WINDOW + Q_BLOCK

    def block(i):
        q0 = i * Q_BLOCK
        qb = lax.dynamic_slice_in_dim(q, q0, Q_BLOCK, axis=1)
        kvb = lax.dynamic_slice_in_dim(kvp, q0, span, axis=1)
        qpos = q0 + jnp.arange(Q_BLOCK)
        kpos = q0 - WINDOW + jnp.arange(span)
        o, _ = gqa_attend(qb, kvb[:, :, 0], kvb[:, :, 1], qpos, kpos, slopes, window_mask(qpos, kpos))
        return o

    o = lax.map(block, jnp.arange(S // Q_BLOCK))
    return o.swapaxes(0, 1).reshape(B, S, NSA_HEADS, NSA_HEAD_DIM)


def routed_experts(xf, expert, gate, w1, w3, w2):
    N, D = xf.shape
    A = N * TOP_K
    e_flat = expert.reshape(-1)
    tok = jnp.arange(A) // TOP_K
    order = jnp.argsort(e_flat)
    e_sorted = e_flat[order]
    counts = jnp.bincount(e_flat, length=N_EXPERTS)
    padded = (counts + MOE_BLOCK - 1) // MOE_BLOCK * MOE_BLOCK
    pad_end = jnp.cumsum(padded)
    pad_start = pad_end - padded
    start = jnp.cumsum(counts) - counts
    dest = pad_start[e_sorted] + jnp.arange(A) - start[e_sorted]
    n_blocks = -(-A // MOE_BLOCK) + N_EXPERTS
    slot_tok = jnp.full((n_blocks * MOE_BLOCK,), N, jnp.int32).at[dest].set(tok[order])
    blk_expert = jnp.minimum(jnp.searchsorted(pad_end, jnp.arange(n_blocks) * MOE_BLOCK, side='right'),
                             N_EXPERTS - 1)
    x_pad = jnp.concatenate([xf, jnp.zeros((1, D), xf.dtype)], axis=0)

    def run(args):
        ids, e = args
        xb = x_pad[ids]
        h = jax.nn.silu(xb @ w1[e]) * (xb @ w3[e])
        return h @ w2[e]

    yb = lax.map(run, (slot_tok.reshape(n_blocks, MOE_BLOCK), blk_expert)).reshape(-1, D)
    w = gate.reshape(-1)[order].astype(xf.dtype)
    return jnp.zeros_like(xf).at[tok[order]].add(w[:, None] * yb[dest])


def moe(x, P):
    B, S, D = x.shape
    xf = x.reshape(B * S, D)
    N = xf.shape[0]
    lg = (xf @ P['router_g_w']).astype(jnp.float32) + P['router_g_b'].astype(jnp.float32)
    lg_top, g_idx = lax.top_k(lg, 1)
    p_group = jnp.exp(lg_top[:, 0] - jax.nn.logsumexp(lg, axis=-1))
    le = ((xf @ P['router_e_w']).astype(jnp.float32) + P['router_e_b'].astype(jnp.float32)
          ).reshape(N, N_GROUPS, EXPERTS_PER_GROUP)
    le_g = le[jnp.arange(N), g_idx[:, 0]]
    e_top, e_idx = lax.top_k(le_g, TOP_K)
    gate = p_group[:, None] * jax.nn.softmax(e_top, axis=-1)
    expert = g_idx * EXPERTS_PER_GROUP + e_idx
    return routed_experts(xf, expert, gate, P['moe_w1'], P['moe_w3'], P['moe_w2']).reshape(B, S, D)


def merge_and_ffn(x, z, o_mla, o_cmp, o_slc, o_win, P):
    B, S = x.shape[:2]
    g = z['g_nsa']
    o_nsa = (g[..., 0:1] * o_cmp + g[..., 1:2] * o_slc + g[..., 2:3] * o_win
             ).reshape(B, S, NSA_HEADS * NSA_HEAD_DIM)
    u = z['g_a'] * (o_mla @ P['mla_w_o']) + z['g_b'] * (o_nsa @ P['nsa_w_o'])
    h = layer_norm(DEEPNORM_ALPHA * x + u @ P['w_out'], P['ln1_g'], P['ln1_b'])
    return layer_norm(DEEPNORM_ALPHA * h + moe(h, P), P['ln2_g'], P['ln2_b'])


def setup_inputs(seed: int = 0) -> dict:
    key = jax.random.key(seed)
    ks = iter(jax.random.split(key, 48))

    def nrm(shape, scale=1.0):
        return scale * jax.random.normal(next(ks), shape, jnp.float32)

    n_pages = PAST_LEN // PAGE_SIZE
    n_phys = (DEC_BATCH * n_pages * 5) // 4
    w_buf = min(WINDOW, PAST_LEN)
    kv_row = (2, NSA_KV_HEADS, NSA_HEAD_DIM)
    Ld = DEPTH
    inputs = dict(
        x_prompt=nrm((BATCH, SEQ, D_MODEL)),
        x_sample=nrm((DEC_BATCH, DEC_SEQ, D_MODEL)),
        cache_mla_ckv=nrm((Ld, n_phys, PAGE_SIZE, MLA_KV_RANK)),
        cache_mla_krope=nrm((Ld, n_phys, PAGE_SIZE, MLA_ROPE)),
        cache_nsa_cmp_kv=nrm((Ld, n_phys, PAGE_SIZE) + kv_row),
        cache_nsa_slc_kv=nrm((Ld, n_phys, PAGE_SIZE) + kv_row),
        state_nsa_win_kv=nrm((Ld, DEC_BATCH, w_buf) + kv_row),
        page_table=jax.random.permutation(next(ks), n_phys)[:DEC_BATCH * n_pages]
            .reshape(DEC_BATCH, n_pages).astype(jnp.int32),
        w_in=nrm((Ld, D_MODEL, D_IN), D_MODEL ** -0.5),
        mla_g_q=1.0 + nrm((Ld, MLA_Q_RANK), 0.01),
        mla_g_kv=1.0 + nrm((Ld, MLA_KV_RANK), 0.01),
        mla_w_uq=nrm((Ld, MLA_Q_RANK, MLA_HEADS, MLA_NOPE + MLA_ROPE), MLA_Q_RANK ** -0.5),
        mla_w_uk=nrm((Ld, MLA_KV_RANK, MLA_HEADS, MLA_NOPE), MLA_KV_RANK ** -0.5),
        mla_w_uv=nrm((Ld, MLA_KV_RANK, MLA_HEADS, MLA_V), MLA_KV_RANK ** -0.5),
        mla_w_o=nrm((Ld, MLA_HEADS * MLA_V, D_MODEL), (MLA_HEADS * MLA_V) ** -0.5),
        nsa_pe_k=nrm((Ld, CMP_BLOCK, NSA_HEAD_DIM), 0.1),
        nsa_w1_k=nrm((Ld, CMP_BLOCK * NSA_HEAD_DIM, CMP_HIDDEN), (CMP_BLOCK * NSA_HEAD_DIM) ** -0.5),
        nsa_w2_k=nrm((Ld, CMP_HIDDEN, NSA_HEAD_DIM), CMP_HIDDEN ** -0.5),
        nsa_pe_v=nrm((Ld, CMP_BLOCK, NSA_HEAD_DIM), 0.1),
        nsa_w1_v=nrm((Ld, CMP_BLOCK * NSA_HEAD_DIM, CMP_HIDDEN), (CMP_BLOCK * NSA_HEAD_DIM) ** -0.5),
        nsa_w2_v=nrm((Ld, CMP_HIDDEN, NSA_HEAD_DIM), CMP_HIDDEN ** -0.5),
        nsa_w_o=nrm((Ld, NSA_HEADS * NSA_HEAD_DIM, D_MODEL), (NSA_HEADS * NSA_HEAD_DIM) ** -0.5),
        w_out=nrm((Ld, D_MODEL, D_MODEL), D_MODEL ** -0.5 * DEEPNORM_BETA),
        ln1_g=1.0 + nrm((Ld, D_MODEL), 0.01),
        ln1_b=nrm((Ld, D_MODEL), 0.01),
        router_g_w=nrm((Ld, D_MODEL, N_GROUPS), D_MODEL ** -0.5),
        router_g_b=nrm((Ld, N_GROUPS), 0.01),
        router_e_w=nrm((Ld, D_MODEL, N_EXPERTS), D_MODEL ** -0.5),
        router_e_b=nrm((Ld, N_EXPERTS), 0.01),
        moe_w1=nrm((Ld, N_EXPERTS, D_MODEL, D_EXPERT), D_MODEL ** -0.5),
        moe_w3=nrm((Ld, N_EXPERTS, D_MODEL, D_EXPERT), D_MODEL ** -0.5),
        moe_w2=nrm((Ld, N_EXPERTS, D_EXPERT, D_MODEL), D_EXPERT ** -0.5 * DEEPNORM_BETA),
        ln2_g=1.0 + nrm((Ld, D_MODEL), 0.01),
        ln2_b=nrm((Ld, D_MODEL), 0.01),
    )
    return inputs


def reference(x_prompt, x_sample, cache_mla_ckv, cache_mla_krope, cache_nsa_cmp_kv, cache_nsa_slc_kv,
              state_nsa_win_kv, page_table, w_in, mla_g_q, mla_g_kv, mla_w_uq, mla_w_uk, mla_w_uv, mla_w_o,
              nsa_pe_k, nsa_w1_k, nsa_w2_k, nsa_pe_v, nsa_w1_v, nsa_w2_v, nsa_w_o, w_out, ln1_g, ln1_b,
              router_g_w, router_g_b, router_e_w, router_e_b, moe_w1, moe_w3, moe_w2, ln2_g, ln2_b):
    S = x_prompt.shape[1]
    T = x_sample.shape[1]
    past_len = page_table.shape[1] * PAGE_SIZE
    wbuf = state_nsa_win_kv.shape[2]
    pos_p = jnp.arange(S)
    pos_s = past_len + jnp.arange(T)
    kpos_win_s = past_len - wbuf + jnp.arange(wbuf + T)
    slopes = alibi_slopes()
    hp, hs = x_prompt, x_sample
    layer_states = []
    for l in range(DEPTH):
        P = dict(w_in=w_in[l], mla_g_q=mla_g_q[l], mla_g_kv=mla_g_kv[l], mla_w_uq=mla_w_uq[l],
                 mla_w_uk=mla_w_uk[l], mla_w_uv=mla_w_uv[l], mla_w_o=mla_w_o[l],
                 nsa_pe_k=nsa_pe_k[l], nsa_w1_k=nsa_w1_k[l], nsa_w2_k=nsa_w2_k[l],
                 nsa_pe_v=nsa_pe_v[l], nsa_w1_v=nsa_w1_v[l], nsa_w2_v=nsa_w2_v[l], nsa_w_o=nsa_w_o[l],
                 w_out=w_out[l], ln1_g=ln1_g[l], ln1_b=ln1_b[l],
                 router_g_w=router_g_w[l], router_g_b=router_g_b[l],
                 router_e_w=router_e_w[l], router_e_b=router_e_b[l],
                 moe_w1=moe_w1[l], moe_w3=moe_w3[l], moe_w2=moe_w2[l], ln2_g=ln2_g[l], ln2_b=ln2_b[l])
        z = in_proj(hp, P)
        qn, qr, ckv_p, kr_p = mla_latent(z, pos_p, P)
        o_mla = mla_prompt_attn(qn, qr, ckv_p, kr_p, P)
        o_cmp, o_slc = nsa_cmp_slc(z['q'], z['kv_cmp'], z['kv_slc'], pos_p, slopes, P)
        o_win = win_prompt(z['q'], z['kv_win'], slopes)
        win_p = z['kv_win'][:, S - min(WINDOW, S):]
        hp_next = merge_and_ffn(hp, z, o_mla, o_cmp, o_slc, o_win, P)
        zs = in_proj(hs, P)
        qn_s, qr_s, ckv_s, kr_s = mla_latent(zs, pos_s, P)
        ckv_all = jnp.concatenate([gather_pages(cache_mla_ckv, l, page_table), ckv_s], axis=1)
        kr_all = jnp.concatenate([gather_pages(cache_mla_krope, l, page_table), kr_s], axis=1)
        o_mla_s = mla_sample_attn(qn_s, qr_s, ckv_all, kr_all, pos_s, P)
        cmp_all = jnp.concatenate([gather_pages(cache_nsa_cmp_kv, l, page_table), zs['kv_cmp']], axis=1)
        slc_all = jnp.concatenate([gather_pages(cache_nsa_slc_kv, l, page_table), zs['kv_slc']], axis=1)
        o_cmp_s, o_slc_s = nsa_cmp_slc(zs['q'], cmp_all, slc_all, pos_s, slopes, P)
        win_all = jnp.concatenate([state_nsa_win_kv[l], zs['kv_win']], axis=1)
        o_win_s, _ = gqa_attend(zs['q'], win_all[:, :, 0], win_all[:, :, 1], pos_s, kpos_win_s, slopes,
                                window_mask(pos_s, kpos_win_s))
        hs_next = merge_and_ffn(hs, zs, o_mla_s, o_cmp_s, o_slc_s, o_win_s, P)
        layer_states.append((ckv_p, ckv_s, kr_p, kr_s, z['kv_cmp'], zs['kv_cmp'],
                             z['kv_slc'], zs['kv_slc'], win_p, win_all[:, T:]))
        hp, hs = hp_next, hs_next
    st = [jnp.stack(a) for a in zip(*layer_states)]
    return (hp, hs, st[0], st[1], st[2], st[3], st[4], st[5], st[6], st[7], st[8], st[9])
```

```python
import functools

import numpy as np
import jax
import jax.numpy as jnp
from jax import lax
from jax.experimental import pallas as pl
from jax.experimental.pallas import tpu as pltpu

F32 = jnp.float32
BF16 = jnp.bfloat16
I32 = jnp.int32

PAGE = 128
MLA_HEADS = 8
MLA_NOPE = 64
MLA_ROPE = 32
MLA_V = 64
MLA_Q_RANK = 256
MLA_KV_RANK = 256
ROPE_THETA = 10000.0
NSA_HEADS = 8
NSA_KV_HEADS = 2
NSA_GQA = NSA_HEADS // NSA_KV_HEADS
NSA_HD = 64
CMP_BLOCK = 32
CMP_STRIDE = 16
CMP_HIDDEN = 64
SLC_BLOCK = 64
SLC_TOP_N = 16
N_LOCAL_SLC = 2
WINDOW = 512
FORCE_SCORE = 1.0e4
N_GROUPS = 8
EXPERTS_PER_GROUP = 8
N_EXPERTS = N_GROUPS * EXPERTS_PER_GROUP
TOP_K = 2
D_EXPERT = 256
MOE_BLOCK = 128
LN_EPS = 1e-5
RMS_EPS = 1e-6
DEPTH = 1
DEEPNORM_ALPHA = (2.0 * DEPTH) ** 0.25
NSA_KV_COLS = 2 * NSA_KV_HEADS * NSA_HD

SLC_SHIFT = SLC_BLOCK.bit_length() - 1
GQA_SHIFT = NSA_GQA.bit_length() - 1
LANE = 128
HALF = LANE // 2
VMEM_LIMIT = 56 * 1024 * 1024
NEG = -1e30

_C_CQ, _C_CKV, _C_Q, _C_KVC, _C_KVS, _C_KVW, _C_KRP, _C_KRS, _C_GN, _C_GA = (
    0, 256, 512, 1024, 1280, 1536, 1792, 1920, 2048, 2176)


def _cparams(sem):
    return pltpu.CompilerParams(dimension_semantics=sem, vmem_limit_bytes=VMEM_LIMIT)


def _dot(a, b):
    return jnp.dot(a, b, preferred_element_type=F32)


def _dot_nt(a, b):
    return lax.dot_general(a, b, (((1,), (1,)), ((), ())), preferred_element_type=F32)


def _split_dot(a, w_hi, w_lo=None):
    a_hi = a.astype(BF16)
    a_lo = (a - a_hi.astype(F32)).astype(BF16)
    r = _dot(a_hi, w_hi) + _dot(a_lo, w_hi)
    if w_lo is not None:
        r = r + _dot(a_hi, w_lo)
    return r


def _sigmoid(x):
    return 1.0 / (1.0 + jnp.exp(-x))


def _kv_variants(x0, x1):
    lo = lax.broadcasted_iota(I32, x0.shape, 1) < HALF
    r0 = pltpu.roll(x0, HALF, 1)
    r1 = pltpu.roll(x1, HALF, 1)
    return (jnp.where(lo, x0, r1), jnp.where(lo, x1, r0), jnp.where(lo, r0, x1), jnp.where(lo, r1, x0))


def _inproj_kernel(x_ref, cs_ref, w_ref, wuq_ref, wuqs_ref, wuk_ref, wuv_ref, gq_ref, gkv_ref,
                   qm_ref, km_ref, vm_ref, ckv_ref, kr_ref, qn_ref, kvc_ref, kvs_ref, kvw_ref,
                   kvsv_ref, kvwv_ref, gn_ref, ga_ref, gb_ref, *, mla_scale):
    xb = x_ref[...].astype(BF16)

    def proj(a, b):
        return _dot(xb, w_ref[:, a:b])

    cos = cs_ref[:, :LANE]
    sin = cs_ref[:, LANE:]

    def rms(z, g):
        return z * lax.rsqrt(jnp.mean(z * z, axis=-1, keepdims=True) + RMS_EPS) * g

    cq = rms(proj(_C_CQ, _C_CKV), gq_ref[...]).astype(BF16)
    q = _dot(cq, wuq_ref[...])
    qs = _dot(cq, wuqs_ref[...])
    ckv = rms(proj(_C_CKV, _C_Q), gkv_ref[...])
    ckv_ref[...] = ckv
    ckvb = ckv.astype(BF16)
    kr = proj(_C_KRP, _C_KRS) * cos + proj(_C_KRS, _C_GN) * sin
    kr_ref[...] = kr[:, MLA_NOPE:MLA_NOPE + MLA_ROPE]
    kn = _dot(ckvb, wuk_ref[...])
    vm_ref[...] = _dot(ckvb, wuv_ref[...]).astype(BF16)
    for h in range(MLA_HEADS):
        sl = slice(h * LANE, (h + 1) * LANE)
        qm_ref[:, sl] = ((q[:, sl] * cos + qs[:, sl] * sin) * mla_scale).astype(BF16)
        km_ref[:, sl] = (kn[:, sl] + kr).astype(BF16)
    qn = proj(_C_Q, _C_KVC) * (NSA_HD ** -0.5)
    lo = lax.broadcasted_iota(I32, (qn.shape[0], LANE), 1) < HALF
    for j in range(NSA_HEADS // 2):
        blk = qn[:, j * LANE:(j + 1) * LANE]
        qn_ref[:, (2 * j) * LANE:(2 * j + 1) * LANE] = jnp.where(lo, blk, 0.0).astype(BF16)
        qn_ref[:, (2 * j + 1) * LANE:(2 * j + 2) * LANE] = jnp.where(lo, 0.0, blk).astype(BF16)
    kvc_ref[...] = proj(_C_KVC, _C_KVS)
    for src, dst, var in ((_C_KVS, kvs_ref, kvsv_ref), (_C_KVW, kvw_ref, kvwv_ref)):
        z = proj(src, src + NSA_KV_COLS)
        dst[...] = z
        for i, t in enumerate(_kv_variants(z[:, :LANE], z[:, LANE:])):
            var[:, i * LANE:(i + 1) * LANE] = t.astype(BF16)
    gn_ref[...] = _sigmoid(proj(_C_GN, _C_GA))
    d = ga_ref.shape[1]
    ga_ref[...] = _sigmoid(proj(_C_GA, _C_GA + d)).astype(BF16)
    gb_ref[...] = _sigmoid(proj(_C_GA + d, _C_GA + 2 * d)).astype(BF16)


def _in_proj(x, cs, pw, tm):
    n, d = x.shape
    n_cs = cs.shape[0] // tm
    row = lambda i: (i, 0)
    full = lambda i: (0, 0)
    wcols = pw['w_all'].shape[1]
    outs = [
        ('qm', 8 * LANE, BF16), ('km', 8 * LANE, BF16), ('vm', 8 * LANE, BF16),
        ('ckv', MLA_KV_RANK, F32), ('kr', MLA_ROPE, F32), ('qn', 8 * LANE, BF16),
        ('kvc', NSA_KV_COLS, F32), ('kvs', NSA_KV_COLS, F32), ('kvw', NSA_KV_COLS, F32),
        ('kvsv', 4 * LANE, BF16), ('kvwv', 4 * LANE, BF16), ('gn', LANE, F32),
        ('ga', d, BF16), ('gb', d, BF16)]
    res = pl.pallas_call(
        functools.partial(_inproj_kernel, mla_scale=(MLA_NOPE + MLA_ROPE) ** -0.5),
        grid=(n // tm,),
        in_specs=[pl.BlockSpec((tm, d), row),
                  pl.BlockSpec((tm, 2 * LANE), lambda i: (i % n_cs, 0)),
                  pl.BlockSpec((d, wcols), full),
                  pl.BlockSpec((MLA_Q_RANK, 8 * LANE), full),
                  pl.BlockSpec((MLA_Q_RANK, 8 * LANE), full),
                  pl.BlockSpec((MLA_KV_RANK, 8 * LANE), full),
                  pl.BlockSpec((MLA_KV_RANK, 8 * LANE), full),
                  pl.BlockSpec((1, MLA_Q_RANK), full),
                  pl.BlockSpec((1, MLA_KV_RANK), full)],
        out_specs=[pl.BlockSpec((tm, c), row) for _, c, _ in outs],
        out_shape=[jax.ShapeDtypeStruct((n, c), t) for _, c, t in outs],
        compiler_params=_cparams(("parallel",)),
        name="in_proj",
    )(x, cs, pw['w_all'], pw['wuq'], pw['wuqs'], pw['wuk'], pw['wuv'], pw['gq'], pw['gkv'])
    return {k: v for (k, _, _), v in zip(outs, res)}


def _flash_kernel(*refs, mode, tq, tk, alibi, swap):
    if mode == 'select':
        slopes_ref, q_ref, k_ref, v_ref, sel_ref, e_ref, o_ref, m_scr, l_scr, acc_scr = refs
    else:
        slopes_ref, q_ref, k_ref, v_ref, o_ref, m_scr, l_scr, acc_scr = refs
    hp = pl.program_id(1)
    q0 = pl.program_id(2) * tq
    dmat = (lax.broadcasted_iota(I32, (tq, tk), 0) - lax.broadcasted_iota(I32, (tq, tk), 1))
    lo_t = jnp.maximum(q0 - WINDOW, 0) // tk if mode == 'window' else 0
    hi_t = (q0 + tq + tk - 1) // tk
    if mode == 'select':
        sel = sel_ref[...]
    for e in range(2):
        q = q_ref[:, e * LANE:(e + 1) * LANE]
        m_scr[e] = jnp.full((tq, 1), NEG, F32)
        l_scr[e] = jnp.zeros((tq, 1), F32)
        acc_scr[e] = jnp.zeros((tq, LANE), F32)
        if alibi:
            slope = slopes_ref[2 * hp + e]

        def body(kt, carry, e=e, q=q):
            k0 = pl.multiple_of(kt * tk, tk)
            k = k_ref[pl.ds(k0, tk), e * LANE:(e + 1) * LANE]
            v = v_ref[pl.ds(k0, tk), e * LANE:(e + 1) * LANE]
            s = _dot_nt(q, k)
            dist = dmat + (q0 - k0)
            ok = dist >= 0
            if mode == 'window':
                ok = ok & (dist <= WINDOW)
            if alibi:
                s = s - slope * dist.astype(F32)
            if mode == 'select':
                ok = ok & (_dot(sel, e_ref[:, pl.ds(k0, tk)]) > 0.5)
            s = jnp.where(ok, s, NEG)
            m_prev = m_scr[e]
            m_new = jnp.maximum(m_prev, jnp.max(s, axis=1, keepdims=True))
            alpha = jnp.exp(m_prev - m_new)
            p = jnp.exp(s - m_new)
            l_scr[e] = alpha * l_scr[e] + jnp.sum(p, axis=1, keepdims=True)
            acc_scr[e] = alpha * acc_scr[e] + _dot(p.astype(BF16), v)
            m_scr[e] = m_new
            return carry

        lax.fori_loop(lo_t, hi_t, body, 0)
    o0 = acc_scr[0] / l_scr[0]
    o1 = acc_scr[1] / l_scr[1]
    lo = lax.broadcasted_iota(I32, (tq, LANE), 1) < HALF
    o_ref[...] = (jnp.where(lo, o1, o0) if swap else jnp.where(lo, o0, o1)).astype(o_ref.dtype)


def _flash(q, k, v, slopes, b, s, *, mode, kv_shared, alibi, swap, sel=None, emat=None):
    tq = min(256, s)
    tk = min(256, s)
    nq = s // tq
    kv_idx = (lambda bi, hp, qi: (bi, hp // 2)) if kv_shared else (lambda bi, hp, qi: (bi, hp))
    in_specs = [pl.BlockSpec(memory_space=pltpu.SMEM),
                pl.BlockSpec((tq, 2 * LANE), lambda bi, hp, qi: (bi * nq + qi, hp)),
                pl.BlockSpec((s, 2 * LANE), kv_idx),
                pl.BlockSpec((s, 2 * LANE), kv_idx)]
    args = [slopes, q, k, v]
    if mode == 'select':
        in_specs += [pl.BlockSpec((None, None, tq, LANE), lambda bi, hp, qi: (bi, hp // 2, qi, 0)),
                     pl.BlockSpec((LANE, s), lambda bi, hp, qi: (0, 0))]
        args += [sel, emat]
    return pl.pallas_call(
        functools.partial(_flash_kernel, mode=mode, tq=tq, tk=tk, alibi=alibi, swap=swap),
        grid=(b, 4, nq),
        in_specs=in_specs,
        out_specs=pl.BlockSpec((tq, LANE), lambda bi, hp, qi: (bi * nq + qi, hp)),
        out_shape=jax.ShapeDtypeStruct((b * s, 4 * LANE), BF16),
        scratch_shapes=[pltpu.VMEM((2, tq, 1), F32), pltpu.VMEM((2, tq, 1), F32),
                        pltpu.VMEM((2, tq, LANE), F32)],
        compiler_params=_cparams(("parallel", "parallel", "arbitrary")),
        name="flash_" + mode,
    )(*args)


def _cmp_fs_kernel(*refs, n_in):
    refs = refs[len(refs) - n_in - 2:]
    x_refs, w_ref, o_ref = refs[:n_in], refs[n_in], refs[n_in + 1]
    for j, xr in enumerate(x_refs):
        x = xr[...]
        x = x.reshape(x.shape[-2], x.shape[-1]).astype(BF16)
        r = x.shape[0]
        o_ref[j * r:(j + 1) * r, :] = _dot(x, w_ref[...])


def _cmp_fs_dense(x, w_big):
    r, c = x.shape
    tr = min(256, r)
    return pl.pallas_call(
        functools.partial(_cmp_fs_kernel, n_in=1),
        grid=(r // tr,),
        in_specs=[pl.BlockSpec((tr, c), lambda i: (i, 0)), pl.BlockSpec(w_big.shape, lambda i: (0, 0))],
        out_specs=pl.BlockSpec((tr, 4 * LANE), lambda i: (i, 0)),
        out_shape=jax.ShapeDtypeStruct((r, 4 * LANE), F32),
        compiler_params=_cparams(("parallel",)),
        name="cmp_fs",
    )(x, w_big)


def _cmp_fs_paged(cache, page_table, w_big):
    n_phys = cache.shape[0]
    b, n_pages = page_table.shape
    rows = PAGE // CMP_STRIDE
    c4 = cache.reshape(n_phys, rows, CMP_STRIDE * NSA_KV_COLS)
    pp = min(16, n_pages)
    nc = n_pages // pp
    in_specs = [pl.BlockSpec((None, rows, c4.shape[2]),
                             functools.partial(lambda bi, ci, pt, j: (pt[bi, ci * pp + j], 0, 0), j=j))
                for j in range(pp)]
    in_specs.append(pl.BlockSpec(w_big.shape, lambda bi, ci, pt: (0, 0)))
    return pl.pallas_call(
        functools.partial(_cmp_fs_kernel, n_in=pp),
        grid_spec=pltpu.PrefetchScalarGridSpec(
            num_scalar_prefetch=1, grid=(b, nc), in_specs=in_specs,
            out_specs=pl.BlockSpec((pp * rows, 4 * LANE), lambda bi, ci, pt: (bi * nc + ci, 0))),
        out_shape=jax.ShapeDtypeStruct((b * n_pages * rows, 4 * LANE), F32),
        compiler_params=_cparams(("parallel", "arbitrary")),
        name="cmp_fs_paged",
    )(page_table, *([c4] * pp), w_big)


def _gelu_tanh(x):
    return 0.5 * x * (1.0 + jnp.tanh(0.7978845608028654 * (x + 0.044715 * x * x * x)))


def _cmp_finish_kernel(fs_ref, pe_ref, w1_ref, w2_ref, o_ref):
    fs = fs_ref[...]
    n = fs.shape[0]
    pe = _split_dot(pe_ref[...], w1_ref[0], w1_ref[1])
    h = _gelu_tanh(fs[:, :2 * LANE] + pltpu.roll(fs[:, 2 * LANE:], n - 1, 0) + pe)
    kc = _dot(h.astype(BF16), w2_ref[...])
    for i, t in enumerate(_kv_variants(kc[:, :LANE], kc[:, LANE:])):
        o_ref[:, i * LANE:(i + 1) * LANE] = t.astype(BF16)


def _cmp_finish(fs, nb, pw):
    r = fs.shape[0]
    full2 = lambda i: (0, 0)
    return pl.pallas_call(
        _cmp_finish_kernel,
        grid=(r // nb,),
        in_specs=[pl.BlockSpec((nb, 4 * LANE), lambda i: (i, 0)),
                  pl.BlockSpec(pw['pe_flat'].shape, full2),
                  pl.BlockSpec(pw['w1_pe'].shape, lambda i: (0, 0, 0)),
                  pl.BlockSpec(pw['w2_cmp'].shape, full2)],
        out_specs=pl.BlockSpec((nb, 4 * LANE), lambda i: (i, 0)),
        out_shape=jax.ShapeDtypeStruct((r, 4 * LANE), BF16),
        compiler_params=_cparams(("parallel",)),
        name="cmp_finish",
    )(fs, pw['pe_flat'], pw['w1_pe'], pw['w2_cmp'])


def _masked_softmax(s, ok):
    s = jnp.where(ok, s, -jnp.inf)
    m = jnp.max(s, axis=-1, keepdims=True)
    m = jnp.where(m > -jnp.inf, m, 0.0)
    e = jnp.exp(s - m)
    d = jnp.sum(e, axis=-1, keepdims=True)
    return e / jnp.where(d > 0.0, d, 1.0)


def _select_rank(imp, cur, n_slc):
    r, w = imp.shape
    blk = lax.broadcasted_iota(I32, (r, w), 1)
    valid = (blk <= cur) & (blk < n_slc)
    forced = (blk == 0) | (valid & (blk > cur - N_LOCAL_SLC))
    score = jnp.where(forced, FORCE_SCORE, jnp.where(valid, imp, -1.0))
    score = jnp.where(blk < n_slc, score, -2.0)

    def body(i, rank):
        col = jnp.sum(jnp.where(blk == i, score, 0.0), axis=1, keepdims=True)
        beats = (col > score) | ((col == score) & (i < blk))
        return rank + jnp.where(beats, 1.0, 0.0)

    rank = lax.fori_loop(0, n_slc, body, jnp.zeros((r, w), F32))
    return rank, valid


def _cmp_select_kernel(slopes_ref, q_ref, kc_ref, wmap_ref, o_ref, sel_ref, *, tq, n_slc):
    kvh = pl.program_id(1)
    q0 = pl.program_id(2) * tq
    nk = kc_ref.shape[0]
    qpos = q0 + lax.broadcasted_iota(I32, (tq, nk), 0)
    kend = lax.broadcasted_iota(I32, (tq, nk), 1) * CMP_STRIDE + (CMP_BLOCK - 1)
    ok = kend <= qpos
    dist = (qpos - kend).astype(F32)
    psum = jnp.zeros((tq, nk), F32)
    outs = []
    for g in range(NSA_GQA):
        kv = kc_ref[:, (g % 2) * LANE:(g % 2 + 1) * LANE]
        s = _dot_nt(q_ref[:, g * LANE:(g + 1) * LANE], kv) - slopes_ref[kvh * NSA_GQA + g] * dist
        p = _masked_softmax(s, ok)
        psum = psum + p
        outs.append(_dot(p.astype(BF16), kv))
    lo = lax.broadcasted_iota(I32, (tq, LANE), 1) < HALF
    o_ref[:, :LANE] = jnp.where(lo, outs[1], outs[0]).astype(o_ref.dtype)
    o_ref[:, LANE:] = jnp.where(lo, outs[3], outs[2]).astype(o_ref.dtype)
    imp = _split_dot(psum, wmap_ref[...])
    cur = (q0 + lax.broadcasted_iota(I32, (tq, 1), 0)) >> SLC_SHIFT
    rank, valid = _select_rank(imp, cur, n_slc)
    sel_ref[...] = jnp.where((rank < SLC_TOP_N) & valid, 1.0, 0.0).astype(sel_ref.dtype)


def _cmp_select(qn, kc_var, slopes, wmap, b, s, n_slc):
    tq = min(256, s)
    nq = s // tq
    nk = kc_var.shape[0] // b
    return pl.pallas_call(
        functools.partial(_cmp_select_kernel, tq=tq, n_slc=n_slc),
        grid=(b, NSA_KV_HEADS, nq),
        in_specs=[pl.BlockSpec(memory_space=pltpu.SMEM),
                  pl.BlockSpec((tq, 4 * LANE), lambda bi, kh, qi: (bi * nq + qi, kh)),
                  pl.BlockSpec((nk, 2 * LANE), lambda bi, kh, qi: (bi, kh)),
                  pl.BlockSpec(wmap.shape, lambda bi, kh, qi: (0, 0))],
        out_specs=[pl.BlockSpec((tq, 2 * LANE), lambda bi, kh, qi: (bi * nq + qi, kh)),
                   pl.BlockSpec((None, None, tq, LANE), lambda bi, kh, qi: (bi, kh, qi, 0))],
        out_shape=[jax.ShapeDtypeStruct((b * s, 4 * LANE), BF16),
                   jax.ShapeDtypeStruct((b, NSA_KV_HEADS, s, LANE), BF16)],
        compiler_params=_cparams(("parallel", "parallel", "parallel")),
        name="cmp_select",
    )(slopes, qn, kc_var, wmap)


def _layer_norm(x, g, b):
    mu = jnp.mean(x, axis=-1, keepdims=True)
    xc = x - mu
    var = jnp.mean(xc * xc, axis=-1, keepdims=True)
    return xc * lax.rsqrt(var + LN_EPS) * g + b


def _merge_kernel(x_ref, omla_ref, ocmp_ref, oslc_ref, owin_ref, gn_ref, ga_ref, gb_ref,
                  wo_ref, wn_ref, wout_ref, e3_ref, g1_ref, b1_ref, wr_ref, br_ref,
                  h_ref, rt_ref):
    gn = gn_ref[...]
    w = 4 * LANE
    ge = _split_dot(gn, e3_ref[...])
    o_nsa = (ge[:, :w] * ocmp_ref[...].astype(F32) + ge[:, w:2 * w] * oslc_ref[...].astype(F32)
             + ge[:, 2 * w:] * owin_ref[...].astype(F32))
    u = (ga_ref[...].astype(F32) * _dot(omla_ref[...], wo_ref[...])
         + gb_ref[...].astype(F32) * _dot(o_nsa.astype(BF16), wn_ref[...]))
    h = _layer_norm(DEEPNORM_ALPHA * x_ref[...] + _dot(u.astype(BF16), wout_ref[...]),
                    g1_ref[...], b1_ref[...])
    h_ref[...] = h
    lg = _split_dot(h, wr_ref[0], wr_ref[1]) + br_ref[...]
    tm = lg.shape[0]
    lane = lax.broadcasted_iota(I32, (tm, LANE), 1)
    is_g = lane < N_GROUPS
    gm = jnp.max(jnp.where(is_g, lg, -jnp.inf), axis=1, keepdims=True)
    gidx = jnp.min(jnp.where(is_g & (lg == gm), lane, LANE), axis=1, keepdims=True)
    p_group = 1.0 / jnp.sum(jnp.where(is_g, jnp.exp(lg - gm), 0.0), axis=1, keepdims=True)
    in_g = (lane >= HALF) & (((lane - HALF) >> 3) == gidx)
    m1 = jnp.max(jnp.where(in_g, lg, -jnp.inf), axis=1, keepdims=True)
    i1 = jnp.min(jnp.where(in_g & (lg == m1), lane, 2 * LANE), axis=1, keepdims=True)
    rest = in_g & (lane != i1)
    m2 = jnp.max(jnp.where(rest, lg, -jnp.inf), axis=1, keepdims=True)
    i2 = jnp.min(jnp.where(rest & (lg == m2), lane, 2 * LANE), axis=1, keepdims=True)
    t = jnp.exp(m2 - m1)
    g1 = p_group / (1.0 + t)
    g2 = p_group * t / (1.0 + t)
    rt = jnp.where(lane == 0, (i1 - HALF).astype(F32),
                   jnp.where(lane == 1, (i2 - HALF).astype(F32),
                             jnp.where(lane == 2, g1, jnp.where(lane == 3, g2, 0.0))))
    rt_ref[...] = rt


def _merge(x, omla, ocmp, oslc, owin, z, pw, tm):
    n, d = x.shape
    row = lambda i: (i, 0)
    full = lambda i: (0, 0)
    w = 4 * LANE
    return pl.pallas_call(
        _merge_kernel,
        grid=(n // tm,),
        in_specs=[pl.BlockSpec((tm, d), row), pl.BlockSpec((tm, w), row), pl.BlockSpec((tm, w), row),
                  pl.BlockSpec((tm, w), row), pl.BlockSpec((tm, w), row), pl.BlockSpec((tm, LANE), row),
                  pl.BlockSpec((tm, d), row), pl.BlockSpec((tm, d), row),
                  pl.BlockSpec((w, d), full), pl.BlockSpec((w, d), full), pl.BlockSpec((d, d), full),
                  pl.BlockSpec((LANE, 3 * w), full), pl.BlockSpec((1, d), full), pl.BlockSpec((1, d), full),
                  pl.BlockSpec((2, d, LANE), lambda i: (0, 0, 0)), pl.BlockSpec((1, LANE), full)],
        out_specs=[pl.BlockSpec((tm, d), row), pl.BlockSpec((tm, LANE), row)],
        out_shape=[jax.ShapeDtypeStruct((n, d), F32), jax.ShapeDtypeStruct((n, LANE), F32)],
        compiler_params=_cparams(("parallel",)),
        name="merge_router",
    )(x, omla, ocmp, oslc, owin, z['gn'], z['ga'], z['gb'], pw['w_o'], pw['w_on'], pw['w_out'],
      pw['e3'], pw['ln1_g'], pw['ln1_b'], pw['w_router'], pw['b_router'])


def _row_copy(src_hbm, dst_vmem, sem, src_row, dst_row):
    return pltpu.make_async_copy(src_hbm.at[pl.ds(src_row, 1)], dst_vmem.at[pl.ds(dst_row, 1)], sem)


def _moe_expert_kernel(be_ref, ids_ref, h_hbm, w1_ref, w3_ref, w2_ref, y_ref, xbuf, sem):
    del be_ref

    def start(s, c):
        _row_copy(h_hbm, xbuf, sem, ids_ref[0, 0, s], s).start()
        return c

    lax.fori_loop(0, MOE_BLOCK, start, 0)

    def wait(s, c):
        _row_copy(h_hbm, xbuf, sem, 0, s).wait()
        return c

    lax.fori_loop(0, MOE_BLOCK, wait, 0)
    xb = xbuf[...].astype(BF16)
    a = _dot(xb, w1_ref[...].astype(BF16))
    hmid = a * _sigmoid(a) * _dot(xb, w3_ref[...].astype(BF16))
    y_ref[...] = _dot(hmid.astype(BF16), w2_ref[...].astype(BF16))


def _moe_experts(h, slot_ids, blk_expert, w1, w3, w2):
    n, d = h.shape
    n_blocks = blk_expert.shape[0]
    de = w1.shape[-1]
    return pl.pallas_call(
        _moe_expert_kernel,
        grid_spec=pltpu.PrefetchScalarGridSpec(
            num_scalar_prefetch=1, grid=(n_blocks,),
            in_specs=[pl.BlockSpec((1, 1, MOE_BLOCK), lambda i, be: (i, 0, 0), memory_space=pltpu.SMEM),
                      pl.BlockSpec(memory_space=pl.ANY),
                      pl.BlockSpec((None, None, d, de), lambda i, be: (0, be[i], 0, 0)),
                      pl.BlockSpec((None, None, d, de), lambda i, be: (0, be[i], 0, 0)),
                      pl.BlockSpec((None, None, de, d), lambda i, be: (0, be[i], 0, 0))],
            out_specs=pl.BlockSpec((MOE_BLOCK, d), lambda i, be: (i, 0)),
            scratch_shapes=[pltpu.VMEM((MOE_BLOCK, d), F32), pltpu.SemaphoreType.DMA(())]),
        out_shape=jax.ShapeDtypeStruct((n_blocks * MOE_BLOCK, d), F32),
        compiler_params=_cparams(("arbitrary",)),
        name="moe_experts",
    )(blk_expert, slot_ids.reshape(n_blocks, 1, MOE_BLOCK), h, w1, w3, w2)


def _moe_combine_kernel(dest_ref, y_hbm, h_ref, rt_ref, g2_ref, b2_ref, o_ref, ybuf, sem, *, tm):
    def start(t, c):
        _row_copy(y_hbm, ybuf.at[0], sem, dest_ref[0, 0, 2 * t], t).start()
        _row_copy(y_hbm, ybuf.at[1], sem, dest_ref[0, 0, 2 * t + 1], t).start()
        return c

    lax.fori_loop(0, tm, start, 0)

    def wait(t, c):
        _row_copy(y_hbm, ybuf.at[0], sem, 0, t).wait()
        _row_copy(y_hbm, ybuf.at[1], sem, 0, t).wait()
        return c

    lax.fori_loop(0, tm, wait, 0)
    rt = rt_ref[...]
    lane = lax.broadcasted_iota(I32, rt.shape, 1)
    g1 = jnp.sum(jnp.where(lane == 2, rt, 0.0), axis=1, keepdims=True)
    g2 = jnp.sum(jnp.where(lane == 3, rt, 0.0), axis=1, keepdims=True)
    moe = g1 * ybuf[0] + g2 * ybuf[1]
    o_ref[...] = _layer_norm(DEEPNORM_ALPHA * h_ref[...] + moe, g2_ref[...], b2_ref[...])


def _moe_combine(y, dest, h, rt, ln_g, ln_b):
    n, d = h.shape
    tm = min(128, n)
    row = lambda i: (i, 0)
    full = lambda i: (0, 0)
    return pl.pallas_call(
        functools.partial(_moe_combine_kernel, tm=tm),
        grid=(n // tm,),
        in_specs=[pl.BlockSpec((1, 1, 2 * tm), lambda i: (i, 0, 0), memory_space=pltpu.SMEM),
                  pl.BlockSpec(memory_space=pl.ANY),
                  pl.BlockSpec((tm, d), row), pl.BlockSpec((tm, LANE), row),
                  pl.BlockSpec((1, d), full), pl.BlockSpec((1, d), full)],
        out_specs=pl.BlockSpec((tm, d), row),
        out_shape=jax.ShapeDtypeStruct((n, d), F32),
        scratch_shapes=[pltpu.VMEM((2, tm, d), F32), pltpu.SemaphoreType.DMA(())],
        compiler_params=_cparams(("arbitrary",)),
        name="moe_combine",
    )(dest.reshape(n // tm, 1, 2 * tm), y, h, rt, ln_g, ln_b)


def _moe(h, rt, pw, w1, w3, w2):
    n = h.shape[0]
    a = n * TOP_K
    expert = rt[:, :TOP_K].astype(I32)
    e_flat = expert.reshape(-1)
    order = jnp.argsort(e_flat)
    e_sorted = e_flat[order]
    counts = jnp.bincount(e_flat, length=N_EXPERTS)
    padded = (counts + MOE_BLOCK - 1) // MOE_BLOCK * MOE_BLOCK
    pad_end = jnp.cumsum(padded)
    pad_start = pad_end - padded
    start = jnp.cumsum(counts) - counts
    dest_sorted = (pad_start[e_sorted] + jnp.arange(a) - start[e_sorted]).astype(I32)
    n_blocks = -(-a // MOE_BLOCK) + N_EXPERTS
    tok_sorted = (order // TOP_K).astype(I32)
    slot_ids = jnp.zeros((n_blocks * MOE_BLOCK,), I32).at[dest_sorted].set(tok_sorted)
    blk_expert = jnp.minimum(jnp.searchsorted(pad_end, jnp.arange(n_blocks) * MOE_BLOCK, side='right'),
                             N_EXPERTS - 1).astype(I32)
    dest = jnp.zeros((a,), I32).at[order].set(dest_sorted)
    y = _moe_experts(h, slot_ids, blk_expert, w1, w3, w2)
    return _moe_combine(y, dest, h, rt, pw['ln2_g'], pw['ln2_b'])


def _qlat_kernel(q_ref, w_ref, o_ref):
    o_ref[...] = _dot(q_ref[...], w_ref[...]).astype(o_ref.dtype)


def _q_latent(qm, wabs):
    b = qm.shape[0]
    return pl.pallas_call(
        _qlat_kernel,
        grid=(MLA_HEADS,),
        in_specs=[pl.BlockSpec((b, LANE), lambda h: (0, h)),
                  pl.BlockSpec((None, LANE, MLA_KV_RANK), lambda h: (h, 0, 0))],
        out_specs=pl.BlockSpec((b, MLA_KV_RANK), lambda h: (0, h)),
        out_shape=jax.ShapeDtypeStruct((b, MLA_HEADS * MLA_KV_RANK), BF16),
        compiler_params=_cparams(("parallel",)),
        name="q_latent",
    )(qm, wabs)


def _mla_decode_kernel(*refs, pp):
    pt_ref = refs[0]
    del pt_ref
    ql_ref, qm_ref, psel_ref, cn_ref, kn_ref = refs[1:6]
    c_refs = refs[6:6 + pp]
    k_refs = refs[6 + pp:6 + 2 * pp]
    o_ref, m_scr, l_scr, acc_scr = refs[6 + 2 * pp:]
    ci = pl.program_id(1)
    ql = ql_ref[...]
    qr = _dot(qm_ref[...], psel_ref[...]).astype(BF16)

    @pl.when(ci == 0)
    def _():
        cn = cn_ref[...].astype(BF16).astype(F32)
        kn = kn_ref[...].astype(BF16).astype(F32)
        m_scr[...] = (jnp.sum(ql.astype(F32) * cn, axis=1, keepdims=True)
                      + jnp.sum(qr.astype(F32) * kn, axis=1, keepdims=True))
        l_scr[...] = jnp.ones_like(l_scr)
        acc_scr[...] = jnp.broadcast_to(cn, acc_scr.shape)

    cs = [c[...].astype(BF16) for c in c_refs]
    s = jnp.concatenate([_dot_nt(ql, c) + _dot_nt(qr, k[...].astype(BF16)) for c, k in zip(cs, k_refs)],
                        axis=1)
    m_prev = m_scr[...]
    m_new = jnp.maximum(m_prev, jnp.max(s, axis=1, keepdims=True))
    alpha = jnp.exp(m_prev - m_new)
    p = jnp.exp(s - m_new)
    l_scr[...] = alpha * l_scr[...] + jnp.sum(p, axis=1, keepdims=True)
    pb = p.astype(BF16)
    acc = alpha * acc_scr[...]
    for j, c in enumerate(cs):
        acc = acc + _dot(pb[:, j * PAGE:(j + 1) * PAGE], c)
    acc_scr[...] = acc
    m_scr[...] = m_new

    @pl.when(ci == pl.num_programs(1) - 1)
    def _():
        o_ref[...] = (acc_scr[...] / l_scr[...]).astype(o_ref.dtype)


def _mla_decode(qlat, qm, psel, ckv_new, kr_new, cache_ckv, cache_kr, page_table):
    b, n_pages = page_table.shape
    pp = min(16, n_pages)
    nc = n_pages // pp
    r = MLA_KV_RANK

    def page_spec(width, j):
        return pl.BlockSpec((None, PAGE, width), lambda bi, ci, pt: (pt[bi, ci * pp + j], 0, 0))

    in_specs = [pl.BlockSpec((None, MLA_HEADS, r), lambda bi, ci, pt: (bi, 0, 0)),
                pl.BlockSpec((None, MLA_HEADS, LANE), lambda bi, ci, pt: (bi, 0, 0)),
                pl.BlockSpec(psel.shape, lambda bi, ci, pt: (0, 0)),
                pl.BlockSpec((None, 1, r), lambda bi, ci, pt: (bi, 0, 0)),
                pl.BlockSpec((None, 1, MLA_ROPE), lambda bi, ci, pt: (bi, 0, 0))]
    in_specs += [page_spec(r, j) for j in range(pp)] + [page_spec(MLA_ROPE, j) for j in range(pp)]
    return pl.pallas_call(
        functools.partial(_mla_decode_kernel, pp=pp),
        grid_spec=pltpu.PrefetchScalarGridSpec(
            num_scalar_prefetch=1, grid=(b, nc), in_specs=in_specs,
            out_specs=pl.BlockSpec((None, MLA_HEADS, r), lambda bi, ci, pt: (bi, 0, 0)),
            scratch_shapes=[pltpu.VMEM((MLA_HEADS, 1), F32), pltpu.VMEM((MLA_HEADS, 1), F32),
                            pltpu.VMEM((MLA_HEADS, r), F32)]),
        out_shape=jax.ShapeDtypeStruct((b, MLA_HEADS, r), BF16),
        compiler_params=_cparams(("parallel", "arbitrary")),
        name="mla_decode",
    )(page_table, qlat.reshape(b, MLA_HEADS, r), qm.reshape(b, MLA_HEADS, LANE), psel,
      ckv_new.reshape(b, 1, r), kr_new.reshape(b, 1, MLA_ROPE),
      *([cache_ckv] * pp), *([cache_kr] * pp))


def _oproj_kernel(o_ref, w_ref, out_ref):
    out_ref[...] = (_dot(o_ref[:, :MLA_KV_RANK], w_ref[:, :LANE])
                    + _dot(o_ref[:, MLA_KV_RANK:], w_ref[:, LANE:])).astype(out_ref.dtype)


def _mla_out_up(o_lat, wuv):
    b = o_lat.shape[0]
    return pl.pallas_call(
        _oproj_kernel,
        grid=(MLA_HEADS // 2,),
        in_specs=[pl.BlockSpec((b, 2 * MLA_KV_RANK), lambda j: (0, j)),
                  pl.BlockSpec((MLA_KV_RANK, 2 * LANE), lambda j: (0, j))],
        out_specs=pl.BlockSpec((b, LANE), lambda j: (0, j)),
        out_shape=jax.ShapeDtypeStruct((b, 4 * LANE), BF16),
        compiler_params=_cparams(("parallel",)),
        name="mla_out_up",
    )(o_lat, wuv)


def _attend8(q8, variants, bias, ok):
    hrow = lax.broadcasted_iota(I32, (MLA_HEADS, 1), 0)
    vsel = (hrow >> GQA_SHIFT) * 2 + (hrow & 1)
    s = jnp.zeros(bias.shape, F32)
    for c, kv in enumerate(variants):
        s = s + jnp.where(vsel == c, _dot_nt(q8, kv), 0.0)
    p = _masked_softmax(s + bias, ok)
    pb = p.astype(BF16)
    o = jnp.zeros((MLA_HEADS, LANE), F32)
    for c, kv in enumerate(variants):
        o = o + jnp.where(vsel == c, _dot(pb, kv), 0.0)
    return o, p


def _slope_col():
    h = lax.broadcasted_iota(I32, (NSA_HEADS, 1), 0)
    return jnp.exp2(-(h + 1).astype(F32) * (8.0 / NSA_HEADS))


def _sample_cmp_win_kernel(q_ref, kc_ref, wmap_ref, st_ref, kvw_ref, ocmp_ref, imp_ref, owin_ref, wout_ref,
                           *, past_len):
    q8 = q_ref[...]
    slope = _slope_col()
    nk = kc_ref.shape[0]
    kend = lax.broadcasted_iota(I32, (NSA_HEADS, nk), 1) * CMP_STRIDE + (CMP_BLOCK - 1)
    ok = kend <= past_len
    bias = -slope * (past_len - kend).astype(F32)
    kc = kc_ref[...]
    o, p = _attend8(q8, [kc[:, i * LANE:(i + 1) * LANE] for i in range(4)], bias, ok)
    ocmp_ref[...] = o.astype(ocmp_ref.dtype)
    hrow = lax.broadcasted_iota(I32, (NSA_HEADS, 1), 0)
    for kh in range(NSA_KV_HEADS):
        psum = jnp.sum(jnp.where((hrow >> GQA_SHIFT) == kh, p, 0.0), axis=0, keepdims=True)
        imp_ref[kh:kh + 1, :] = _split_dot(psum, wmap_ref[...])
    st = st_ref[...]
    wbuf = st.shape[0]
    new = kvw_ref[...]
    rows = lax.broadcasted_iota(I32, (wbuf, 1), 0)
    shifted = jnp.where(rows == wbuf - 1, new, pltpu.roll(st, wbuf - 1, 0))
    wout_ref[...] = shifted
    ext = jnp.concatenate([st, jnp.broadcast_to(new, (LANE, st.shape[1]))], axis=0)
    nkw = wbuf + LANE
    j = lax.broadcasted_iota(I32, (NSA_HEADS, nkw), 1)
    dist = jnp.where(j < wbuf, wbuf - j, 0)
    okw = (j <= wbuf) & (dist <= WINDOW) & (past_len - dist >= 0)
    var = [t.astype(BF16) for t in _kv_variants(ext[:, :LANE], ext[:, LANE:])]
    ow, _ = _attend8(q8, var, -slope * dist.astype(F32), okw)
    owin_ref[...] = ow.astype(owin_ref.dtype)


def _sample_cmp_win(qn, kc_var, wmap, state, kvw, past_len):
    b = qn.shape[0]
    nk = kc_var.shape[0] // b
    wbuf = state.shape[1]
    wcols = wmap.shape[1]
    per_b3 = lambda bi: (bi, 0, 0)
    return pl.pallas_call(
        functools.partial(_sample_cmp_win_kernel, past_len=past_len),
        grid=(b,),
        in_specs=[pl.BlockSpec((None, NSA_HEADS, LANE), per_b3),
                  pl.BlockSpec((nk, 4 * LANE), lambda bi: (bi, 0)),
                  pl.BlockSpec(wmap.shape, lambda bi: (0, 0)),
                  pl.BlockSpec((None, wbuf, NSA_KV_COLS), per_b3),
                  pl.BlockSpec((None, 1, NSA_KV_COLS), per_b3)],
        out_specs=[pl.BlockSpec((None, NSA_HEADS, LANE), per_b3),
                   pl.BlockSpec((None, NSA_KV_HEADS, wcols), per_b3),
                   pl.BlockSpec((None, NSA_HEADS, LANE), per_b3),
                   pl.BlockSpec((None, wbuf, NSA_KV_COLS), per_b3)],
        out_shape=[jax.ShapeDtypeStruct((b, NSA_HEADS, LANE), F32),
                   jax.ShapeDtypeStruct((b, NSA_KV_HEADS, wcols), F32),
                   jax.ShapeDtypeStruct((b, NSA_HEADS, LANE), F32),
                   jax.ShapeDtypeStruct((b, wbuf, NSA_KV_COLS), F32)],
        compiler_params=_cparams(("parallel",)),
        name="sample_cmp_win",
    )(qn.reshape(b, NSA_HEADS, LANE), kc_var, wmap, state, kvw.reshape(b, 1, NSA_KV_COLS))


def _sample_select_kernel(imp_ref, idx_ref, *, cur, n_slc):
    imp = imp_ref[...]
    rank, valid = _select_rank(imp, cur, n_slc)
    r, w = imp.shape
    blk = lax.broadcasted_iota(I32, (r, w), 1).astype(F32)
    lane = lax.broadcasted_iota(I32, (r, LANE), 1)
    out = jnp.zeros((r, LANE), F32)
    for t in range(SLC_TOP_N):
        pick = jnp.sum(jnp.where((rank == t) & valid, blk, 0.0), axis=1, keepdims=True)
        out = jnp.where(lane == t, pick, out)
    idx_ref[...] = out.astype(I32)


def _sample_select(imp, cur, n_slc):
    r = imp.shape[0]
    return pl.pallas_call(
        functools.partial(_sample_select_kernel, cur=cur, n_slc=n_slc),
        grid=(1,),
        in_specs=[pl.BlockSpec(imp.shape, lambda i: (0, 0))],
        out_specs=pl.BlockSpec((r, LANE), lambda i: (0, 0)),
        out_shape=jax.ShapeDtypeStruct((r, LANE), I32),
        compiler_params=_cparams(("arbitrary",)),
        name="sample_select",
    )(imp)


def _sample_slc_kernel(*refs, n_sel, past_len):
    blk_ref, phys_ref = refs[0], refs[1]
    del phys_ref
    q_ref, new_ref = refs[2], refs[3]
    c_refs = refs[4:4 + 2 * n_sel]
    o_ref = refs[4 + 2 * n_sel]
    bi = pl.program_id(0)
    q8 = q_ref[...]
    slope = _slope_col()
    hrow = lax.broadcasted_iota(I32, (NSA_HEADS, 1), 0)
    new = new_ref[...]
    n_tiles = NSA_KV_HEADS * n_sel
    n_cache = n_tiles * SLC_BLOCK
    keys = jnp.concatenate([c[...] for c in c_refs] + [jnp.broadcast_to(new, (LANE, new.shape[1]))], axis=0)
    col = lax.broadcasted_iota(I32, (NSA_HEADS, n_cache + LANE), 1)
    tile = col >> SLC_SHIFT
    blkv = jnp.zeros(col.shape, I32)
    for t in range(n_tiles):
        blkv = jnp.where(tile == t, blk_ref[bi, t], blkv)
    kpos = blkv * SLC_BLOCK + (col & (SLC_BLOCK - 1))
    in_cache = col < n_cache
    mine = (hrow >> GQA_SHIFT) == jnp.where(tile < n_sel, 0, 1)
    dist = jnp.where(in_cache, past_len - kpos, 0)
    ok = (in_cache & mine & (kpos < past_len)) | (col == n_cache)
    var = [t.astype(BF16) for t in _kv_variants(keys[:, :LANE], keys[:, LANE:])]
    o, _ = _attend8(q8, var, -slope * dist.astype(F32), ok)
    o_ref[...] = o.astype(o_ref.dtype)


def _sample_slc(qn, kvs_new, cache_slc, blk_idx, phys_blk, past_len):
    b = qn.shape[0]
    n_sel = blk_idx.shape[1] // NSA_KV_HEADS
    c3 = cache_slc.reshape(-1, SLC_BLOCK, NSA_KV_COLS)
    per_b3 = lambda bi, blk, ph: (bi, 0, 0)
    in_specs = [pl.BlockSpec((None, NSA_HEADS, LANE), per_b3),
                pl.BlockSpec((None, 1, NSA_KV_COLS), per_b3)]
    in_specs += [pl.BlockSpec((None, SLC_BLOCK, NSA_KV_COLS),
                              functools.partial(lambda bi, blk, ph, j: (ph[bi, j], 0, 0), j=j))
                 for j in range(NSA_KV_HEADS * n_sel)]
    return pl.pallas_call(
        functools.partial(_sample_slc_kernel, n_sel=n_sel, past_len=past_len),
        grid_spec=pltpu.PrefetchScalarGridSpec(
            num_scalar_prefetch=2, grid=(b,), in_specs=in_specs,
            out_specs=pl.BlockSpec((None, NSA_HEADS, LANE), per_b3)),
        out_shape=jax.ShapeDtypeStruct((b, NSA_HEADS, LANE), F32),
        compiler_params=_cparams(("parallel",)),
        name="sample_slc",
    )(blk_idx, phys_blk, qn.reshape(b, NSA_HEADS, LANE), kvs_new.reshape(b, 1, NSA_KV_COLS),
      *([c3] * (NSA_KV_HEADS * n_sel)))


def _pair_swap(o8):
    b = o8.shape[0]
    o = o8.reshape(b, NSA_HEADS // 2, 2, LANE)
    lo = jnp.arange(LANE) < HALF
    return jnp.where(lo, o[:, :, 1], o[:, :, 0]).reshape(b, 4 * LANE).astype(BF16)


def _rope_table(pos):
    half = MLA_ROPE // 2
    freqs = ROPE_THETA ** (-jnp.arange(half, dtype=F32) / half)
    ang = pos.astype(F32)[:, None] * freqs
    cos, sin = jnp.cos(ang), jnp.sin(ang)
    n = pos.shape[0]
    one, zero = jnp.ones((n, MLA_NOPE), F32), jnp.zeros((n, MLA_NOPE), F32)
    pad = jnp.zeros((n, LANE - MLA_NOPE - MLA_ROPE), F32)
    return jnp.concatenate([one, cos, cos, pad, zero, -sin, sin, pad], axis=1)


def _cmp_to_slc(n_cmp, n_slc, rows, cols):
    cs = np.arange(n_cmp)[:, None] * CMP_STRIDE
    ss = np.arange(n_slc)[None, :] * SLC_BLOCK
    inter = np.clip(np.minimum(cs + CMP_BLOCK, ss + SLC_BLOCK) - np.maximum(cs, ss), 0, None)
    w = np.zeros((rows, cols), np.float32)
    w[:n_cmp, :n_slc] = inter.astype(np.float32) / CMP_STRIDE
    return jnp.asarray(w, BF16)


def _prep_weights(w_in, mla_g_q, mla_g_kv, mla_w_uq, mla_w_uk, mla_w_uv, mla_w_o, nsa_pe_k, nsa_w1_k,
                  nsa_w2_k, nsa_pe_v, nsa_w1_v, nsa_w2_v, nsa_w_o, w_out, ln1_g, ln1_b, router_g_w,
                  router_g_b, router_e_w, router_e_b, ln2_g, ln2_b):
    d = w_in.shape[0]
    splits = (MLA_Q_RANK, MLA_KV_RANK, MLA_ROPE, NSA_HEADS * NSA_HD, NSA_KV_COLS, NSA_KV_COLS, NSA_KV_COLS,
              3 * NSA_HEADS, d, d)
    offs = np.cumsum(splits)[:-1].tolist()
    cq, ckv, kr, q, kvc, kvs, kvw, gn, ga, gb = jnp.split(w_in, offs, axis=1)
    r2 = MLA_ROPE // 2
    zc = lambda n: jnp.zeros((d, n), F32)
    tail = LANE - MLA_NOPE - MLA_ROPE
    kr_pad = jnp.concatenate([zc(MLA_NOPE), kr, zc(tail)], axis=1)
    kr_sw = jnp.concatenate([zc(MLA_NOPE), kr[:, r2:], kr[:, :r2], zc(tail)], axis=1)
    gn_pad = jnp.concatenate([gn, zc(LANE - gn.shape[1])], axis=1)
    w_all = jnp.concatenate([cq, ckv, q, kvc, kvs, kvw, kr_pad, kr_sw, gn_pad, ga, gb], axis=1).astype(BF16)

    def pad_heads(w, lo_cols):
        r, hh, c = w.shape
        out = jnp.zeros((r, hh, LANE), F32).at[:, :, lo_cols:lo_cols + c].set(w)
        return out

    nope, rope = mla_w_uq[:, :, :MLA_NOPE], mla_w_uq[:, :, MLA_NOPE:]
    wuq = jnp.concatenate([nope, rope, jnp.zeros(nope.shape[:2] + (tail,), F32)], axis=2)
    wuqs = jnp.concatenate([jnp.zeros_like(nope), rope[:, :, r2:], rope[:, :, :r2],
                            jnp.zeros(nope.shape[:2] + (tail,), F32)], axis=2)
    wuk = pad_heads(mla_w_uk, 0)
    odd = (jnp.arange(MLA_HEADS) % 2 == 1)[None, :, None]
    wuv = jnp.where(odd, pad_heads(mla_w_uv, HALF), pad_heads(mla_w_uv, 0))
    flat = lambda w: w.reshape(w.shape[0], -1).astype(BF16)
    wabs = jnp.zeros((MLA_HEADS, LANE, MLA_KV_RANK), F32).at[:, :MLA_NOPE, :].set(
        jnp.transpose(mla_w_uk, (1, 2, 0))).astype(BF16)
    psel = jnp.zeros((LANE, MLA_ROPE), F32).at[MLA_NOPE:MLA_NOPE + MLA_ROPE].set(jnp.eye(MLA_ROPE)).astype(BF16)

    def w1_cols(w1):
        return jnp.transpose(w1.reshape(2, CMP_STRIDE, NSA_HD, CMP_HIDDEN), (1, 2, 0, 3))

    wt = jnp.stack([w1_cols(nsa_w1_k), w1_cols(nsa_w1_v)])
    eye2 = jnp.eye(2, dtype=F32)
    w_big = jnp.einsum('tpdfh,tu,kv->ptkdfuvh', wt, eye2, eye2).reshape(
        CMP_STRIDE * NSA_KV_COLS, 2 * 2 * NSA_KV_HEADS * CMP_HIDDEN).astype(BF16)
    zpe = jnp.zeros((CMP_BLOCK * NSA_HD, CMP_HIDDEN), F32)
    w1_pe = jnp.concatenate([
        jnp.concatenate([nsa_w1_k, nsa_w1_k, zpe, zpe], axis=1),
        jnp.concatenate([zpe, zpe, nsa_w1_v, nsa_w1_v], axis=1)], axis=0)
    pe_flat = jnp.concatenate([nsa_pe_k.reshape(1, -1), nsa_pe_v.reshape(1, -1)], axis=1)
    w1_hi = w1_pe.astype(BF16)
    w1_lo = (w1_pe - w1_hi.astype(F32)).astype(BF16)
    z64 = jnp.zeros((CMP_HIDDEN, NSA_HD), F32)
    rows = [[nsa_w2_k, z64, z64, z64], [z64, nsa_w2_k, z64, z64], [z64, z64, nsa_w2_v, z64],
            [z64, z64, z64, nsa_w2_v]]
    w2_cmp = jnp.concatenate([jnp.concatenate(r, axis=1) for r in rows], axis=0).astype(BF16)

    perm = np.arange(NSA_HEADS * NSA_HD).reshape(NSA_HEADS // 2, 2, NSA_HD)[:, ::-1].reshape(-1)
    w_on = nsa_w_o[perm].astype(BF16)
    e3 = np.zeros((LANE, 3, NSA_HEADS * NSA_HD), np.float32)
    for h in range(NSA_HEADS):
        pos = (h // 2) * LANE + (0 if h % 2 else HALF)
        for j in range(3):
            e3[3 * h + j, j, pos:pos + NSA_HD] = 1.0
    e3 = jnp.asarray(e3.reshape(LANE, -1), BF16)
    w_r = jnp.concatenate([router_g_w, zc(HALF - N_GROUPS), router_e_w], axis=1)
    w_r_hi = w_r.astype(BF16)
    w_r_lo = (w_r - w_r_hi.astype(F32)).astype(BF16)
    b_r = jnp.concatenate([router_g_b, jnp.zeros((HALF - N_GROUPS,), F32), router_e_b])[None, :]
    return dict(
        w_all=w_all, wuq=flat(wuq), wuqs=flat(wuqs), wuk=flat(wuk), wuv=flat(wuv),
        gq=mla_g_q[None, :], gkv=mla_g_kv[None, :], wabs=wabs, psel=psel, w_big=w_big,
        pe_flat=pe_flat, w1_pe=jnp.stack([w1_hi, w1_lo]), w2_cmp=w2_cmp,
        w_o=mla_w_o.astype(BF16), w_on=w_on, w_out=w_out.astype(BF16), e3=e3,
        ln1_g=ln1_g[None, :], ln1_b=ln1_b[None, :], w_router=jnp.stack([w_r_hi, w_r_lo]), b_router=b_r,
        ln2_g=ln2_g[None, :], ln2_b=ln2_b[None, :])


def _prompt_layer(x, pw, slopes, moe_w):
    b, s, d = x.shape
    n = b * s
    tm = min(256, s)
    z = _in_proj(x.reshape(n, d), _rope_table(jnp.arange(s)), pw, tm)
    o_mla = _flash(z['qm'], z['km'], z['vm'], slopes, b, s, mode='causal', kv_shared=False, alibi=False,
                   swap=False)
    nb = s // CMP_STRIDE
    fs = _cmp_fs_dense(z['kvc'].reshape(b * nb, CMP_STRIDE * NSA_KV_COLS), pw['w_big'])
    kc_var = _cmp_finish(fs, nb, pw)
    n_slc = -(-s // SLC_BLOCK)
    wmap = _cmp_to_slc(nb - 1, n_slc, nb, LANE)
    o_cmp, sel = _cmp_select(z['qn'], kc_var, slopes, wmap, b, s, n_slc)
    emat = jnp.asarray(np.arange(LANE)[:, None] == (np.arange(s)[None, :] // SLC_BLOCK), BF16)
    o_slc = _flash(z['qn'], z['kvsv'], z['kvsv'], slopes, b, s, mode='select', kv_shared=True, alibi=True,
                   swap=True, sel=sel, emat=emat)
    o_win = _flash(z['qn'], z['kvwv'], z['kvwv'], slopes, b, s, mode='window', kv_shared=True, alibi=True,
                   swap=True)
    h, rt = _merge(x.reshape(n, d), o_mla, o_cmp, o_slc, o_win, z, pw, tm)
    y = _moe(h, rt, pw, *moe_w)
    return y.reshape(b, s, d), z


def _sample_layer(x, pw, slopes, moe_w, cache_ckv, cache_kr, cache_cmp, cache_slc, state_win, page_table):
    b, t, d = x.shape
    n_pages = page_table.shape[1]
    past_len = n_pages * PAGE
    z = _in_proj(x.reshape(b, d), _rope_table(jnp.full((b,), past_len)), pw, b)
    qlat = _q_latent(z['qm'], pw['wabs'])
    o_lat = _mla_decode(qlat, z['qm'], pw['psel'], z['ckv'], z['kr'], cache_ckv, cache_kr, page_table)
    o_mla = _mla_out_up(o_lat.reshape(b, MLA_HEADS * MLA_KV_RANK), pw['wuv'])
    nb = past_len // CMP_STRIDE
    fs = _cmp_fs_paged(cache_cmp, page_table, pw['w_big'])
    kc_var = _cmp_finish(fs, nb, pw)
    n_slc = -(-(past_len + t) // SLC_BLOCK)
    wcols = -(-n_slc // LANE) * LANE
    wmap = _cmp_to_slc(nb - 1, n_slc, nb, wcols)
    o_cmp8, imp, o_win8, win_out = _sample_cmp_win(z['qn'], kc_var, wmap, state_win, z['kvw'], past_len)
    cur = past_len // SLC_BLOCK
    idx = _sample_select(imp.reshape(b * NSA_KV_HEADS, wcols), cur, n_slc)[:, :SLC_TOP_N]
    blk = idx.reshape(b, NSA_KV_HEADS * SLC_TOP_N)
    safe = jnp.minimum(blk, past_len // SLC_BLOCK - 1)
    per_page = PAGE // SLC_BLOCK
    phys = (jnp.take_along_axis(page_table, safe // per_page, axis=1) * per_page + safe % per_page).astype(I32)
    o_slc8 = _sample_slc(z['qn'], z['kvs'], cache_slc, blk, phys, past_len)
    h, rt = _merge(x.reshape(b, d), o_mla, _pair_swap(o_cmp8), _pair_swap(o_slc8), _pair_swap(o_win8),
                   z, pw, b)
    y = _moe(h, rt, pw, *moe_w)
    return y.reshape(b, t, d), z, win_out


def kernel(x_prompt, x_sample, cache_mla_ckv, cache_mla_krope, cache_nsa_cmp_kv, cache_nsa_slc_kv,
           state_nsa_win_kv, page_table, w_in, mla_g_q, mla_g_kv, mla_w_uq, mla_w_uk, mla_w_uv, mla_w_o,
           nsa_pe_k, nsa_w1_k, nsa_w2_k, nsa_pe_v, nsa_w1_v, nsa_w2_v, nsa_w_o, w_out, ln1_g, ln1_b,
           router_g_w, router_g_b, router_e_w, router_e_b, moe_w1, moe_w3, moe_w2, ln2_g, ln2_b):
    assert w_in.shape[0] == DEPTH and x_sample.shape[1] == 1
    b, s, d = x_prompt.shape
    bs = x_sample.shape[0]
    kv_shape = (2, NSA_KV_HEADS, NSA_HD)
    pw = _prep_weights(w_in[0], mla_g_q[0], mla_g_kv[0], mla_w_uq[0], mla_w_uk[0], mla_w_uv[0], mla_w_o[0],
                       nsa_pe_k[0], nsa_w1_k[0], nsa_w2_k[0], nsa_pe_v[0], nsa_w1_v[0], nsa_w2_v[0],
                       nsa_w_o[0], w_out[0], ln1_g[0], ln1_b[0], router_g_w[0], router_g_b[0],
                       router_e_w[0], router_e_b[0], ln2_g[0], ln2_b[0])
    slopes = 2.0 ** (-8.0 * jnp.arange(1, NSA_HEADS + 1, dtype=F32) / NSA_HEADS)
    moe_w = (moe_w1, moe_w3, moe_w2)
    y_p, zp = _prompt_layer(x_prompt, pw, slopes, moe_w)
    y_s, zs, win_s = _sample_layer(
        x_sample, pw, slopes, moe_w, cache_mla_ckv[0], cache_mla_krope[0],
        cache_nsa_cmp_kv[0].reshape(-1, PAGE, NSA_KV_COLS), cache_nsa_slc_kv[0].reshape(-1, PAGE, NSA_KV_COLS),
        state_nsa_win_kv[0].reshape(bs, -1, NSA_KV_COLS), page_table)
    wp = min(WINDOW, s)
    return (y_p, y_s,
            zp['ckv'].reshape(1, b, s, MLA_KV_RANK), zs['ckv'].reshape(1, bs, 1, MLA_KV_RANK),
            zp['kr'].reshape(1, b, s, MLA_ROPE), zs['kr'].reshape(1, bs, 1, MLA_ROPE),
            zp['kvc'].reshape((1, b, s) + kv_shape), zs['kvc'].reshape((1, bs, 1) + kv_shape),
            zp['kvs'].reshape((1, b, s) + kv_shape), zs['kvs'].reshape((1, bs, 1) + kv_shape),
            zp['kvw'].reshape(b, s, NSA_KV_COLS)[:, s - wp:].reshape((1, b, wp) + kv_shape),
            win_s.reshape((1, bs, -1) + kv_shape))
```

```python
import functools

import numpy as np
import jax
import jax.numpy as jnp
from jax import lax
from jax.experimental import pallas as pl
from jax.experimental.pallas import tpu as pltpu

F32 = jnp.float32
BF16 = jnp.bfloat16
I32 = jnp.int32

PAGE = 128
MLA_HEADS = 8
MLA_NOPE = 64
MLA_ROPE = 32
MLA_V = 64
MLA_Q_RANK = 256
MLA_KV_RANK = 256
ROPE_THETA = 10000.0
NSA_HEADS = 8
NSA_KV_HEADS = 2
NSA_GQA = NSA_HEADS // NSA_KV_HEADS
NSA_HD = 64
CMP_BLOCK = 32
CMP_STRIDE = 16
CMP_HIDDEN = 64
SLC_BLOCK = 64
SLC_TOP_N = 16
N_LOCAL_SLC = 2
WINDOW = 512
FORCE_SCORE = 1.0e4
N_GROUPS = 8
EXPERTS_PER_GROUP = 8
N_EXPERTS = N_GROUPS * EXPERTS_PER_GROUP
TOP_K = 2
D_EXPERT = 256
MOE_BLOCK = 128
LN_EPS = 1e-5
RMS_EPS = 1e-6
DEPTH = 1
DEEPNORM_ALPHA = (2.0 * DEPTH) ** 0.25
NSA_KV_COLS = 2 * NSA_KV_HEADS * NSA_HD

PAGE_SHIFT = PAGE.bit_length() - 1
SLC_SHIFT = SLC_BLOCK.bit_length() - 1
GQA_SHIFT = NSA_GQA.bit_length() - 1
LANE = 128
HALF = LANE // 2
VMEM_LIMIT = 56 * 1024 * 1024
NEG = -1e30
LOG2E = 1.4426950408889634

_C_CQ, _C_CKV, _C_Q, _C_KVC, _C_KVS, _C_KVW, _C_KRP, _C_KRS, _C_GN, _C_GA = (
    0, 256, 512, 1024, 1280, 1536, 1792, 1920, 2048, 2176)


def _cparams(sem):
    return pltpu.CompilerParams(dimension_semantics=sem, vmem_limit_bytes=VMEM_LIMIT)


def _dot(a, b):
    return jnp.dot(a, b, preferred_element_type=F32)


def _dot_nt(a, b):
    return lax.dot_general(a, b, (((1,), (1,)), ((), ())), preferred_element_type=F32)


def _split_dot(a, w_hi, w_lo=None):
    a_hi = a.astype(BF16)
    a_lo = (a - a_hi.astype(F32)).astype(BF16)
    r = _dot(a_hi, w_hi) + _dot(a_lo, w_hi)
    if w_lo is not None:
        r = r + _dot(a_hi, w_lo)
    return r


def _sigmoid(x):
    return 1.0 / (1.0 + jnp.exp(-x))


def _kv_variants(x0, x1):
    lo = lax.broadcasted_iota(I32, x0.shape, 1) < HALF
    r0 = pltpu.roll(x0, HALF, 1)
    r1 = pltpu.roll(x1, HALF, 1)
    return (jnp.where(lo, x0, r1), jnp.where(lo, x1, r0), jnp.where(lo, r0, x1), jnp.where(lo, r1, x0))


def _inproj_kernel(x_ref, cs_ref, w_ref, wuq_ref, wuqs_ref, wuk_ref, wuv_ref, gq_ref, gkv_ref,
                   qm_ref, km_ref, vm_ref, ckv_ref, kr_ref, qn_ref, kvc_ref, kvs_ref, kvw_ref,
                   kvsv_ref, kvwv_ref, gn_ref, ga_ref, gb_ref, qnat_ref, *, mla_scale):
    xb = x_ref[...].astype(BF16)

    def proj(a, b):
        return _dot(xb, w_ref[:, a:b])

    cos = cs_ref[:, :LANE]
    sin = cs_ref[:, LANE:]

    def rms(z, g):
        return z * lax.rsqrt(jnp.mean(z * z, axis=-1, keepdims=True) + RMS_EPS) * g

    cq = rms(proj(_C_CQ, _C_CKV), gq_ref[...]).astype(BF16)
    q = _dot(cq, wuq_ref[...])
    qs = _dot(cq, wuqs_ref[...])
    ckv = rms(proj(_C_CKV, _C_Q), gkv_ref[...])
    ckv_ref[...] = ckv
    ckvb = ckv.astype(BF16)
    kr = proj(_C_KRP, _C_KRS) * cos + proj(_C_KRS, _C_GN) * sin
    kr_ref[...] = kr[:, MLA_NOPE:MLA_NOPE + MLA_ROPE]
    kn = _dot(ckvb, wuk_ref[...])
    vm_ref[...] = _dot(ckvb, wuv_ref[...]).astype(BF16)
    for h in range(MLA_HEADS):
        sl = slice(h * LANE, (h + 1) * LANE)
        qm_ref[:, sl] = ((q[:, sl] * cos + qs[:, sl] * sin) * mla_scale).astype(BF16)
        km_ref[:, sl] = (kn[:, sl] + kr).astype(BF16)
    qn = proj(_C_Q, _C_KVC) * (NSA_HD ** -0.5 * LOG2E)
    qnat_ref[...] = qn.astype(BF16)
    lo = lax.broadcasted_iota(I32, (qn.shape[0], LANE), 1) < HALF
    for j in range(NSA_HEADS // 2):
        blk = qn[:, j * LANE:(j + 1) * LANE]
        qn_ref[:, (2 * j) * LANE:(2 * j + 1) * LANE] = jnp.where(lo, blk, 0.0).astype(BF16)
        qn_ref[:, (2 * j + 1) * LANE:(2 * j + 2) * LANE] = jnp.where(lo, 0.0, blk).astype(BF16)
    kvc_ref[...] = proj(_C_KVC, _C_KVS)
    for src, dst, var in ((_C_KVS, kvs_ref, kvsv_ref), (_C_KVW, kvw_ref, kvwv_ref)):
        z = proj(src, src + NSA_KV_COLS)
        dst[...] = z
        for i, t in enumerate(_kv_variants(z[:, :LANE], z[:, LANE:])):
            var[:, i * LANE:(i + 1) * LANE] = t.astype(BF16)
    gn_ref[...] = _sigmoid(proj(_C_GN, _C_GA))
    d = ga_ref.shape[1]
    ga_ref[...] = _sigmoid(proj(_C_GA, _C_GA + d)).astype(BF16)
    gb_ref[...] = _sigmoid(proj(_C_GA + d, _C_GA + 2 * d)).astype(BF16)


def _in_proj(x, cs, pw, tm):
    n, d = x.shape
    n_cs = cs.shape[0] // tm
    row = lambda i: (i, 0)
    full = lambda i: (0, 0)
    wcols = pw['w_all'].shape[1]
    outs = [
        ('qm', 8 * LANE, BF16), ('km', 8 * LANE, BF16), ('vm', 8 * LANE, BF16),
        ('ckv', MLA_KV_RANK, F32), ('kr', MLA_ROPE, F32), ('qn', 8 * LANE, BF16),
        ('kvc', NSA_KV_COLS, F32), ('kvs', NSA_KV_COLS, F32), ('kvw', NSA_KV_COLS, F32),
        ('kvsv', 4 * LANE, BF16), ('kvwv', 4 * LANE, BF16), ('gn', LANE, F32),
        ('ga', d, BF16), ('gb', d, BF16), ('qnat', NSA_HEADS * NSA_HD, BF16)]
    res = pl.pallas_call(
        functools.partial(_inproj_kernel, mla_scale=(MLA_NOPE + MLA_ROPE) ** -0.5 * LOG2E),
        grid=(n // tm,),
        in_specs=[pl.BlockSpec((tm, d), row),
                  pl.BlockSpec((tm, 2 * LANE), lambda i: (i % n_cs, 0)),
                  pl.BlockSpec((d, wcols), full),
                  pl.BlockSpec((MLA_Q_RANK, 8 * LANE), full),
                  pl.BlockSpec((MLA_Q_RANK, 8 * LANE), full),
                  pl.BlockSpec((MLA_KV_RANK, 8 * LANE), full),
                  pl.BlockSpec((MLA_KV_RANK, 8 * LANE), full),
                  pl.BlockSpec((1, MLA_Q_RANK), full),
                  pl.BlockSpec((1, MLA_KV_RANK), full)],
        out_specs=[pl.BlockSpec((tm, c), row) for _, c, _ in outs],
        out_shape=[jax.ShapeDtypeStruct((n, c), t) for _, c, t in outs],
        compiler_params=_cparams(("parallel",)),
        name="in_proj",
    )(x, cs, pw['w_all'], pw['wuq'], pw['wuqs'], pw['wuk'], pw['wuv'], pw['gq'], pw['gkv'])
    return {k: v for (k, _, _), v in zip(outs, res)}


def _flash_kernel(*refs, mode, t, alibi, swap):
    if mode == 'select':
        slopes_ref, q_ref, k_ref, v_ref, sel_ref, e_ref, o_ref, m_scr, l_scr, acc_scr = refs
    else:
        slopes_ref, q_ref, k_ref, v_ref, o_ref, m_scr, l_scr, acc_scr = refs
    hp = pl.program_id(1)
    qi = pl.program_id(2)
    q0 = qi * t
    dmat = lax.broadcasted_iota(I32, (t, t), 0) - lax.broadcasted_iota(I32, (t, t), 1)
    col = lax.broadcasted_iota(I32, (1, t), 1)
    qs = [q_ref[:, e * LANE:(e + 1) * LANE] for e in range(2)]
    for e in range(2):
        m_scr[e] = jnp.full((t, 1), NEG, F32)
        l_scr[e] = jnp.zeros((t, 1), F32)
        acc_scr[e] = jnp.zeros((t, LANE), F32)

    def step(kt, edge):
        k0 = pl.multiple_of(kt * t, t)
        ok = None
        if edge:
            dist = dmat + (q0 - k0)
            ok = dist >= 0
            if mode == 'window':
                ok = ok & (dist <= WINDOW)
        if mode == 'select':
            picked = _dot(sel_ref[...], e_ref[:, pl.ds(k0, t)]) > 0.5
            ok = picked if ok is None else ok & picked
        for e in range(2):
            k = k_ref[pl.ds(k0, t), e * LANE:(e + 1) * LANE]
            v = v_ref[pl.ds(k0, t), e * LANE:(e + 1) * LANE]
            s = _dot_nt(qs[e], k)
            if alibi:
                s = s + (slopes_ref[2 * hp + e] * LOG2E) * (k0 - q0 + col).astype(F32)
            if ok is not None:
                s = jnp.where(ok, s, NEG)
            m_prev = m_scr[e]
            m_new = jnp.maximum(m_prev, jnp.max(s, axis=1, keepdims=True))
            alpha = jnp.exp2(m_prev - m_new)
            p = jnp.exp2(s - m_new)
            l_scr[e] = alpha * l_scr[e] + jnp.sum(p, axis=1, keepdims=True)
            acc_scr[e] = alpha * acc_scr[e] + _dot(p.astype(BF16), v)
            m_scr[e] = m_new

    def loop(lo, hi, edge):
        def body(kt, c):
            step(kt, edge)
            return c
        lax.fori_loop(lo, hi, body, 0)

    if mode == 'window':
        loop(jnp.maximum(q0 - WINDOW, 0) // t, qi + 1, True)
    else:
        loop(0, qi, False)
        step(qi, True)
    o0 = acc_scr[0] / l_scr[0]
    o1 = acc_scr[1] / l_scr[1]
    lo = lax.broadcasted_iota(I32, (t, LANE), 1) < HALF
    o_ref[...] = (jnp.where(lo, o1, o0) if swap else jnp.where(lo, o0, o1)).astype(o_ref.dtype)


def _flash(q, k, v, slopes, b, s, *, mode, kv_shared, alibi, swap, sel=None, emat=None):
    tq = min(512, s)
    nq = s // tq
    kv_idx = (lambda bi, hp, qi: (bi, hp // 2)) if kv_shared else (lambda bi, hp, qi: (bi, hp))
    in_specs = [pl.BlockSpec(memory_space=pltpu.SMEM),
                pl.BlockSpec((tq, 2 * LANE), lambda bi, hp, qi: (bi * nq + qi, hp)),
                pl.BlockSpec((s, 2 * LANE), kv_idx),
                pl.BlockSpec((s, 2 * LANE), kv_idx)]
    args = [slopes, q, k, v]
    if mode == 'select':
        in_specs += [pl.BlockSpec((None, None, tq, LANE), lambda bi, hp, qi: (bi, hp // 2, qi, 0)),
                     pl.BlockSpec((LANE, s), lambda bi, hp, qi: (0, 0))]
        args += [sel, emat]
    return pl.pallas_call(
        functools.partial(_flash_kernel, mode=mode, t=tq, alibi=alibi, swap=swap),
        grid=(b, 4, nq),
        in_specs=in_specs,
        out_specs=pl.BlockSpec((tq, LANE), lambda bi, hp, qi: (bi * nq + qi, hp)),
        out_shape=jax.ShapeDtypeStruct((b * s, 4 * LANE), BF16),
        scratch_shapes=[pltpu.VMEM((2, tq, 1), F32), pltpu.VMEM((2, tq, 1), F32),
                        pltpu.VMEM((2, tq, LANE), F32)],
        compiler_params=_cparams(("parallel", "parallel", "arbitrary")),
        name="flash_" + mode,
    )(*args)


def _cmp_fs_kernel(*refs, n_pages):
    if n_pages:
        x_refs = refs[1:1 + n_pages]
        w_ref, o_ref, lo, hi = refs[1 + n_pages:]
        for j, xr in enumerate(x_refs):
            lo[j * PAGE:(j + 1) * PAGE, :] = xr[:LANE, :].T
            hi[j * PAGE:(j + 1) * PAGE, :] = xr[LANE:, :].T
    else:
        lo, hi, w_ref, o_ref = refs
    r = o_ref.shape[0]
    acc = jnp.zeros(o_ref.shape, F32)
    for p in range(CMP_STRIDE):
        rows = pl.ds(p, r, stride=CMP_STRIDE)
        x = jnp.concatenate([lo[rows, :], hi[rows, :]], axis=1)
        acc = acc + _dot(x.astype(BF16), w_ref[p])
    o_ref[...] = acc


def _cmp_fs_dense(x, w_big):
    n, c = x.shape
    tr = min(256, n // CMP_STRIDE)
    return pl.pallas_call(
        functools.partial(_cmp_fs_kernel, n_pages=0),
        grid=(n // CMP_STRIDE // tr,),
        in_specs=[pl.BlockSpec((tr * CMP_STRIDE, LANE), lambda i: (i, 0)),
                  pl.BlockSpec((tr * CMP_STRIDE, LANE), lambda i: (i, 1)),
                  pl.BlockSpec(w_big.shape, lambda i: (0, 0, 0))],
        out_specs=pl.BlockSpec((tr, 4 * LANE), lambda i: (i, 0)),
        out_shape=jax.ShapeDtypeStruct((n // CMP_STRIDE, 4 * LANE), F32),
        compiler_params=_cparams(("parallel",)),
        name="cmp_fs",
    )(x, x, w_big)


def _cmp_fs_paged(cache_t, page_table, w_big):
    b, n_pages = page_table.shape
    rows = PAGE // CMP_STRIDE
    pp = min(32, n_pages)
    nc = n_pages // pp
    in_specs = [pl.BlockSpec((None, NSA_KV_COLS, PAGE),
                             functools.partial(lambda bi, ci, pt, j: (pt[bi, ci * pp + j], 0, 0), j=j))
                for j in range(pp)]
    in_specs.append(pl.BlockSpec(w_big.shape, lambda bi, ci, pt: (0, 0, 0)))
    return pl.pallas_call(
        functools.partial(_cmp_fs_kernel, n_pages=pp),
        grid_spec=pltpu.PrefetchScalarGridSpec(
            num_scalar_prefetch=1, grid=(b, nc), in_specs=in_specs,
            out_specs=pl.BlockSpec((pp * rows, 4 * LANE), lambda bi, ci, pt: (bi * nc + ci, 0)),
            scratch_shapes=[pltpu.VMEM((pp * PAGE, LANE), F32), pltpu.VMEM((pp * PAGE, LANE), F32)]),
        out_shape=jax.ShapeDtypeStruct((b * n_pages * rows, 4 * LANE), F32),
        compiler_params=_cparams(("parallel", "arbitrary")),
        name="cmp_fs_paged",
    )(page_table, *([cache_t] * pp), w_big)


def _gelu_tanh(x):
    return 0.5 * x * (1.0 + jnp.tanh(0.7978845608028654 * (x + 0.044715 * x * x * x)))


def _cmp_finish_kernel(fs_ref, pe_ref, w1_ref, w2_ref, o_ref):
    fs = fs_ref[...]
    n = fs.shape[0]
    pe = _split_dot(pe_ref[...], w1_ref[0], w1_ref[1])
    h = _gelu_tanh(fs[:, :2 * LANE] + pltpu.roll(fs[:, 2 * LANE:], n - 1, 0) + pe)
    kc = _dot(h.astype(BF16), w2_ref[...])
    for i, t in enumerate(_kv_variants(kc[:, :LANE], kc[:, LANE:])):
        o_ref[:, i * LANE:(i + 1) * LANE] = t.astype(BF16)


def _cmp_finish(fs, nb, pw):
    r = fs.shape[0]
    full2 = lambda i: (0, 0)
    return pl.pallas_call(
        _cmp_finish_kernel,
        grid=(r // nb,),
        in_specs=[pl.BlockSpec((nb, 4 * LANE), lambda i: (i, 0)),
                  pl.BlockSpec(pw['pe_flat'].shape, full2),
                  pl.BlockSpec(pw['w1_pe'].shape, lambda i: (0, 0, 0)),
                  pl.BlockSpec(pw['w2_cmp'].shape, full2)],
        out_specs=pl.BlockSpec((nb, 4 * LANE), lambda i: (i, 0)),
        out_shape=jax.ShapeDtypeStruct((r, 4 * LANE), BF16),
        compiler_params=_cparams(("parallel",)),
        name="cmp_finish",
    )(fs, pw['pe_flat'], pw['w1_pe'], pw['w2_cmp'])


def _masked_softmax(s, ok):
    s = jnp.where(ok, s, -jnp.inf)
    m = jnp.max(s, axis=-1, keepdims=True)
    m = jnp.where(m > -jnp.inf, m, 0.0)
    e = jnp.exp2(s - m)
    d = jnp.sum(e, axis=-1, keepdims=True)
    return e / jnp.where(d > 0.0, d, 1.0)


def _select_rank(imp, cur, n_slc):
    r, w = imp.shape
    blk = lax.broadcasted_iota(I32, (r, w), 1)
    valid = (blk <= cur) & (blk < n_slc)
    forced = (blk == 0) | (valid & (blk > cur - N_LOCAL_SLC))
    score = jnp.where(forced, FORCE_SCORE, jnp.where(valid, imp, -1.0))
    score = jnp.where(blk < n_slc, score, -2.0)

    def body(i, rank):
        col = jnp.sum(jnp.where(blk == i, score, 0.0), axis=1, keepdims=True)
        beats = (col > score) | ((col == score) & (i < blk))
        return rank + jnp.where(beats, 1.0, 0.0)

    rank = lax.fori_loop(0, n_slc, body, jnp.zeros((r, w), F32), unroll=True if n_slc <= 32 else 4)
    return rank, valid


def _cmp_select_kernel(slopes_ref, q_ref, kc_ref, wmap_ref, o_ref, sel_ref, *, tq, n_slc):
    kvh = pl.program_id(1)
    q0 = pl.program_id(2) * tq
    nk = kc_ref.shape[0]
    qpos = q0 + lax.broadcasted_iota(I32, (tq, nk), 0)
    kend = lax.broadcasted_iota(I32, (tq, nk), 1) * CMP_STRIDE + (CMP_BLOCK - 1)
    ok = kend <= qpos
    dist = (qpos - kend).astype(F32)
    psum = jnp.zeros((tq, nk), F32)
    outs = []
    for g in range(NSA_GQA):
        kv = kc_ref[:, (g % 2) * LANE:(g % 2 + 1) * LANE]
        s = _dot_nt(q_ref[:, g * LANE:(g + 1) * LANE], kv) - (slopes_ref[kvh * NSA_GQA + g] * LOG2E) * dist
        p = _masked_softmax(s, ok)
        psum = psum + p
        outs.append(_dot(p.astype(BF16), kv))
    lo = lax.broadcasted_iota(I32, (tq, LANE), 1) < HALF
    o_ref[:, :LANE] = jnp.where(lo, outs[1], outs[0]).astype(o_ref.dtype)
    o_ref[:, LANE:] = jnp.where(lo, outs[3], outs[2]).astype(o_ref.dtype)
    imp = _split_dot(psum, wmap_ref[...])
    cur = (q0 + lax.broadcasted_iota(I32, (tq, 1), 0)) >> SLC_SHIFT
    rank, valid = _select_rank(imp, cur, n_slc)
    sel_ref[...] = jnp.where((rank < SLC_TOP_N) & valid, 1.0, 0.0).astype(sel_ref.dtype)


def _cmp_select(qn, kc_var, slopes, wmap, b, s, n_slc):
    tq = min(256, s)
    nq = s // tq
    nk = kc_var.shape[0] // b
    return pl.pallas_call(
        functools.partial(_cmp_select_kernel, tq=tq, n_slc=n_slc),
        grid=(b, NSA_KV_HEADS, nq),
        in_specs=[pl.BlockSpec(memory_space=pltpu.SMEM),
                  pl.BlockSpec((tq, 4 * LANE), lambda bi, kh, qi: (bi * nq + qi, kh)),
                  pl.BlockSpec((nk, 2 * LANE), lambda bi, kh, qi: (bi, kh)),
                  pl.BlockSpec(wmap.shape, lambda bi, kh, qi: (0, 0))],
        out_specs=[pl.BlockSpec((tq, 2 * LANE), lambda bi, kh, qi: (bi * nq + qi, kh)),
                   pl.BlockSpec((None, None, tq, LANE), lambda bi, kh, qi: (bi, kh, qi, 0))],
        out_shape=[jax.ShapeDtypeStruct((b * s, 4 * LANE), BF16),
                   jax.ShapeDtypeStruct((b, NSA_KV_HEADS, s, LANE), BF16)],
        compiler_params=_cparams(("parallel", "parallel", "parallel")),
        name="cmp_select",
    )(slopes, qn, kc_var, wmap)


def _layer_norm(x, g, b):
    mu = jnp.mean(x, axis=-1, keepdims=True)
    xc = x - mu
    var = jnp.mean(xc * xc, axis=-1, keepdims=True)
    return xc * lax.rsqrt(var + LN_EPS) * g + b


def _merge_kernel(x_ref, omla_ref, ocmp_ref, oslc_ref, owin_ref, gn_ref, ga_ref, gb_ref,
                  wo_ref, wn_ref, wout_ref, e3_ref, g1_ref, b1_ref, wr_ref, br_ref,
                  h_ref, rt_ref):
    gn = gn_ref[...]
    w = 4 * LANE
    ge = _split_dot(gn, e3_ref[...])
    o_nsa = (ge[:, :w] * ocmp_ref[...].astype(F32) + ge[:, w:2 * w] * oslc_ref[...].astype(F32)
             + ge[:, 2 * w:] * owin_ref[...].astype(F32))
    u = (ga_ref[...].astype(F32) * _dot(omla_ref[...], wo_ref[...])
         + gb_ref[...].astype(F32) * _dot(o_nsa.astype(BF16), wn_ref[...]))
    h = _layer_norm(DEEPNORM_ALPHA * x_ref[...] + _dot(u.astype(BF16), wout_ref[...]),
                    g1_ref[...], b1_ref[...])
    h_ref[...] = h
    lg = _split_dot(h, wr_ref[0], wr_ref[1]) + br_ref[...]
    tm = lg.shape[0]
    lane = lax.broadcasted_iota(I32, (tm, LANE), 1)
    is_g = lane < N_GROUPS
    gm = jnp.max(jnp.where(is_g, lg, -jnp.inf), axis=1, keepdims=True)
    gidx = jnp.min(jnp.where(is_g & (lg == gm), lane, LANE), axis=1, keepdims=True)
    p_group = 1.0 / jnp.sum(jnp.where(is_g, jnp.exp(lg - gm), 0.0), axis=1, keepdims=True)
    in_g = (lane >= HALF) & (((lane - HALF) >> 3) == gidx)
    m1 = jnp.max(jnp.where(in_g, lg, -jnp.inf), axis=1, keepdims=True)
    i1 = jnp.min(jnp.where(in_g & (lg == m1), lane, 2 * LANE), axis=1, keepdims=True)
    rest = in_g & (lane != i1)
    m2 = jnp.max(jnp.where(rest, lg, -jnp.inf), axis=1, keepdims=True)
    i2 = jnp.min(jnp.where(rest & (lg == m2), lane, 2 * LANE), axis=1, keepdims=True)
    t = jnp.exp(m2 - m1)
    g1 = p_group / (1.0 + t)
    g2 = p_group * t / (1.0 + t)
    rt = jnp.where(lane == 0, (i1 - HALF).astype(F32),
                   jnp.where(lane == 1, (i2 - HALF).astype(F32),
                             jnp.where(lane == 2, g1, jnp.where(lane == 3, g2, 0.0))))
    rt_ref[...] = rt


def _merge(x, omla, ocmp, oslc, owin, z, pw, tm):
    n, d = x.shape
    row = lambda i: (i, 0)
    full = lambda i: (0, 0)
    w = 4 * LANE
    return pl.pallas_call(
        _merge_kernel,
        grid=(n // tm,),
        in_specs=[pl.BlockSpec((tm, d), row), pl.BlockSpec((tm, w), row), pl.BlockSpec((tm, w), row),
                  pl.BlockSpec((tm, w), row), pl.BlockSpec((tm, w), row), pl.BlockSpec((tm, LANE), row),
                  pl.BlockSpec((tm, d), row), pl.BlockSpec((tm, d), row),
                  pl.BlockSpec((w, d), full), pl.BlockSpec((w, d), full), pl.BlockSpec((d, d), full),
                  pl.BlockSpec((LANE, 3 * w), full), pl.BlockSpec((1, d), full), pl.BlockSpec((1, d), full),
                  pl.BlockSpec((2, d, LANE), lambda i: (0, 0, 0)), pl.BlockSpec((1, LANE), full)],
        out_specs=[pl.BlockSpec((tm, d), row), pl.BlockSpec((tm, LANE), row)],
        out_shape=[jax.ShapeDtypeStruct((n, d), F32), jax.ShapeDtypeStruct((n, LANE), F32)],
        compiler_params=_cparams(("parallel",)),
        name="merge_router",
    )(x, omla, ocmp, oslc, owin, z['gn'], z['ga'], z['gb'], pw['w_o'], pw['w_on'], pw['w_out'],
      pw['e3'], pw['ln1_g'], pw['ln1_b'], pw['w_router'], pw['b_router'])


def _row_copy(src_hbm, dst_vmem, sem, src_row, dst_row):
    return pltpu.make_async_copy(src_hbm.at[pl.ds(src_row, 1)], dst_vmem.at[pl.ds(dst_row, 1)], sem)


def _moe_expert_kernel(be_ref, ids_ref, h_hbm, w1_ref, w3_ref, w2_ref, y_ref, xbuf, sem):
    del be_ref

    def start(s, c):
        _row_copy(h_hbm, xbuf, sem, ids_ref[0, 0, s], s).start()
        return c

    lax.fori_loop(0, MOE_BLOCK, start, 0)

    def wait(s, c):
        _row_copy(h_hbm, xbuf, sem, 0, s).wait()
        return c

    lax.fori_loop(0, MOE_BLOCK, wait, 0)
    xb = xbuf[...].astype(BF16)
    a = _dot(xb, w1_ref[...].astype(BF16))
    hmid = a * _sigmoid(a) * _dot(xb, w3_ref[...].astype(BF16))
    y_ref[...] = _dot(hmid.astype(BF16), w2_ref[...].astype(BF16))


def _moe_experts(h, slot_ids, blk_expert, w1, w3, w2):
    n, d = h.shape
    n_blocks = blk_expert.shape[0]
    de = w1.shape[-1]
    return pl.pallas_call(
        _moe_expert_kernel,
        grid_spec=pltpu.PrefetchScalarGridSpec(
            num_scalar_prefetch=1, grid=(n_blocks,),
            in_specs=[pl.BlockSpec((1, 1, MOE_BLOCK), lambda i, be: (i, 0, 0), memory_space=pltpu.SMEM),
                      pl.BlockSpec(memory_space=pl.ANY),
                      pl.BlockSpec((None, None, d, de), lambda i, be: (0, be[i], 0, 0)),
                      pl.BlockSpec((None, None, d, de), lambda i, be: (0, be[i], 0, 0)),
                      pl.BlockSpec((None, None, de, d), lambda i, be: (0, be[i], 0, 0))],
            out_specs=pl.BlockSpec((MOE_BLOCK, d), lambda i, be: (i, 0)),
            scratch_shapes=[pltpu.VMEM((MOE_BLOCK, d), F32), pltpu.SemaphoreType.DMA(())]),
        out_shape=jax.ShapeDtypeStruct((n_blocks * MOE_BLOCK, d), F32),
        compiler_params=_cparams(("arbitrary",)),
        name="moe_experts",
    )(blk_expert, slot_ids.reshape(n_blocks, 1, MOE_BLOCK), h, w1, w3, w2)


def _moe_combine_kernel(dest_ref, y_hbm, h_ref, rt_ref, g2_ref, b2_ref, o_ref, ybuf, sem, *, tm):
    def start(t, c):
        _row_copy(y_hbm, ybuf.at[0], sem, dest_ref[0, 0, 2 * t], t).start()
        _row_copy(y_hbm, ybuf.at[1], sem, dest_ref[0, 0, 2 * t + 1], t).start()
        return c

    lax.fori_loop(0, tm, start, 0)

    def wait(t, c):
        _row_copy(y_hbm, ybuf.at[0], sem, 0, t).wait()
        _row_copy(y_hbm, ybuf.at[1], sem, 0, t).wait()
        return c

    lax.fori_loop(0, tm, wait, 0)
    rt = rt_ref[...]
    lane = lax.broadcasted_iota(I32, rt.shape, 1)
    g1 = jnp.sum(jnp.where(lane == 2, rt, 0.0), axis=1, keepdims=True)
    g2 = jnp.sum(jnp.where(lane == 3, rt, 0.0), axis=1, keepdims=True)
    moe = g1 * ybuf[0] + g2 * ybuf[1]
    o_ref[...] = _layer_norm(DEEPNORM_ALPHA * h_ref[...] + moe, g2_ref[...], b2_ref[...])


def _moe_combine(y, dest, h, rt, ln_g, ln_b):
    n, d = h.shape
    tm = min(128, n)
    row = lambda i: (i, 0)
    full = lambda i: (0, 0)
    return pl.pallas_call(
        functools.partial(_moe_combine_kernel, tm=tm),
        grid=(n // tm,),
        in_specs=[pl.BlockSpec((1, 1, 2 * tm), lambda i: (i, 0, 0), memory_space=pltpu.SMEM),
                  pl.BlockSpec(memory_space=pl.ANY),
                  pl.BlockSpec((tm, d), row), pl.BlockSpec((tm, LANE), row),
                  pl.BlockSpec((1, d), full), pl.BlockSpec((1, d), full)],
        out_specs=pl.BlockSpec((tm, d), row),
        out_shape=jax.ShapeDtypeStruct((n, d), F32),
        scratch_shapes=[pltpu.VMEM((2, tm, d), F32), pltpu.SemaphoreType.DMA(())],
        compiler_params=_cparams(("arbitrary",)),
        name="moe_combine",
    )(dest.reshape(n // tm, 1, 2 * tm), y, h, rt, ln_g, ln_b)


def _moe(h, rt, pw, w1, w3, w2):
    n = h.shape[0]
    a = n * TOP_K
    expert = rt[:, :TOP_K].astype(I32)
    e_flat = expert.reshape(-1)
    order = jnp.argsort(e_flat)
    e_sorted = e_flat[order]
    counts = jnp.bincount(e_flat, length=N_EXPERTS)
    padded = (counts + MOE_BLOCK - 1) // MOE_BLOCK * MOE_BLOCK
    pad_end = jnp.cumsum(padded)
    pad_start = pad_end - padded
    start = jnp.cumsum(counts) - counts
    dest_sorted = (pad_start[e_sorted] + jnp.arange(a) - start[e_sorted]).astype(I32)
    n_blocks = -(-a // MOE_BLOCK) + N_EXPERTS
    tok_sorted = (order // TOP_K).astype(I32)
    slot_ids = jnp.zeros((n_blocks * MOE_BLOCK,), I32).at[dest_sorted].set(tok_sorted)
    blk_start = jnp.arange(n_blocks) * MOE_BLOCK
    blk_expert = jnp.minimum(jnp.sum(pad_end[None, :] <= blk_start[:, None], axis=1), N_EXPERTS - 1).astype(I32)
    dest = jnp.zeros((a,), I32).at[order].set(dest_sorted)
    y = _moe_experts(h, slot_ids, blk_expert, w1, w3, w2)
    return _moe_combine(y, dest, h, rt, pw['ln2_g'], pw['ln2_b'])


def _qlat_kernel(q_ref, w_ref, o_ref):
    o_ref[...] = _dot(q_ref[...], w_ref[...]).astype(o_ref.dtype)


def _q_latent(qm, wabs):
    b = qm.shape[0]
    return pl.pallas_call(
        _qlat_kernel,
        grid=(MLA_HEADS,),
        in_specs=[pl.BlockSpec((b, LANE), lambda h: (0, h)),
                  pl.BlockSpec((None, LANE, MLA_KV_RANK), lambda h: (h, 0, 0))],
        out_specs=pl.BlockSpec((b, MLA_KV_RANK), lambda h: (0, h)),
        out_shape=jax.ShapeDtypeStruct((b, MLA_HEADS * MLA_KV_RANK), BF16),
        compiler_params=_cparams(("parallel",)),
        name="q_latent",
    )(qm, wabs)


def _mla_decode_kernel(*refs, pp):
    pt_ref = refs[0]
    del pt_ref
    ql_ref, qlt_ref, qr_ref, cn_ref, kn_ref = refs[1:6]
    c_refs = refs[6:6 + pp]
    k_refs = refs[6 + pp:6 + 2 * pp]
    o_ref, m_scr, l_scr, acc_scr = refs[6 + 2 * pp:]
    ci = pl.program_id(1)
    qlt = qlt_ref[...]
    qr = qr_ref[...]

    @pl.when(ci == 0)
    def _():
        cn = cn_ref[...].astype(BF16).astype(F32)
        kn = kn_ref[...].astype(BF16).astype(F32)
        m_scr[...] = (jnp.sum(ql_ref[...].astype(F32) * cn, axis=1, keepdims=True)
                      + jnp.sum(qr.astype(F32) * kn, axis=1, keepdims=True))
        l_scr[...] = jnp.ones_like(l_scr)
        acc_scr[...] = jnp.broadcast_to(cn, acc_scr.shape)

    cs = [c[...].astype(BF16) for c in c_refs]
    s = jnp.concatenate([_dot(c, qlt).T[:MLA_HEADS] + _dot(qr, k[...].astype(BF16))
                         for c, k in zip(cs, k_refs)], axis=1)
    m_prev = m_scr[...]
    m_new = jnp.maximum(m_prev, jnp.max(s, axis=1, keepdims=True))
    alpha = jnp.exp2(m_prev - m_new)
    p = jnp.exp2(s - m_new)
    l_scr[...] = alpha * l_scr[...] + jnp.sum(p, axis=1, keepdims=True)
    pb = p.astype(BF16)
    acc = alpha * acc_scr[...]
    for j, c in enumerate(cs):
        acc = acc + _dot(pb[:, j * PAGE:(j + 1) * PAGE], c)
    acc_scr[...] = acc
    m_scr[...] = m_new

    @pl.when(ci == pl.num_programs(1) - 1)
    def _():
        o_ref[...] = (acc_scr[...] / l_scr[...]).astype(o_ref.dtype)


def _mla_decode(qlat, qr, ckv_new, kr_new, cache_ckv, cache_kr_t, page_table):
    b, n_pages = page_table.shape
    pp = min(16, n_pages)
    nc = n_pages // pp
    r = MLA_KV_RANK
    ql3 = qlat.reshape(b, MLA_HEADS, r)
    qlt = jnp.pad(jnp.transpose(ql3, (0, 2, 1)), ((0, 0), (0, 0), (0, LANE - MLA_HEADS)))

    def page_spec(rows, cols, j):
        return pl.BlockSpec((None, rows, cols), lambda bi, ci, pt: (pt[bi, ci * pp + j], 0, 0))

    in_specs = [pl.BlockSpec((None, MLA_HEADS, r), lambda bi, ci, pt: (bi, 0, 0)),
                pl.BlockSpec((None, r, LANE), lambda bi, ci, pt: (bi, 0, 0)),
                pl.BlockSpec((None, MLA_HEADS, MLA_ROPE), lambda bi, ci, pt: (bi, 0, 0)),
                pl.BlockSpec((None, 1, r), lambda bi, ci, pt: (bi, 0, 0)),
                pl.BlockSpec((None, 1, MLA_ROPE), lambda bi, ci, pt: (bi, 0, 0))]
    in_specs += [page_spec(PAGE, r, j) for j in range(pp)] + [page_spec(MLA_ROPE, PAGE, j) for j in range(pp)]
    return pl.pallas_call(
        functools.partial(_mla_decode_kernel, pp=pp),
        grid_spec=pltpu.PrefetchScalarGridSpec(
            num_scalar_prefetch=1, grid=(b, nc), in_specs=in_specs,
            out_specs=pl.BlockSpec((None, MLA_HEADS, r), lambda bi, ci, pt: (bi, 0, 0)),
            scratch_shapes=[pltpu.VMEM((MLA_HEADS, 1), F32), pltpu.VMEM((MLA_HEADS, 1), F32),
                            pltpu.VMEM((MLA_HEADS, r), F32)]),
        out_shape=jax.ShapeDtypeStruct((b, MLA_HEADS, r), BF16),
        compiler_params=_cparams(("parallel", "arbitrary")),
        name="mla_decode",
    )(page_table, ql3, qlt, qr, ckv_new.reshape(b, 1, r), kr_new.reshape(b, 1, MLA_ROPE),
      *([cache_ckv] * pp), *([cache_kr_t] * pp))


def _oproj_kernel(o_ref, w_ref, out_ref):
    out_ref[...] = (_dot(o_ref[:, :MLA_KV_RANK], w_ref[:, :LANE])
                    + _dot(o_ref[:, MLA_KV_RANK:], w_ref[:, LANE:])).astype(out_ref.dtype)


def _mla_out_up(o_lat, wuv):
    b = o_lat.shape[0]
    return pl.pallas_call(
        _oproj_kernel,
        grid=(MLA_HEADS // 2,),
        in_specs=[pl.BlockSpec((b, 2 * MLA_KV_RANK), lambda j: (0, j)),
                  pl.BlockSpec((MLA_KV_RANK, 2 * LANE), lambda j: (0, j))],
        out_specs=pl.BlockSpec((b, LANE), lambda j: (0, j)),
        out_shape=jax.ShapeDtypeStruct((b, 4 * LANE), BF16),
        compiler_params=_cparams(("parallel",)),
        name="mla_out_up",
    )(o_lat, wuv)


def _attend8(q8, variants, bias, ok):
    hrow = lax.broadcasted_iota(I32, (MLA_HEADS, 1), 0)
    vsel = (hrow >> GQA_SHIFT) * 2 + (hrow & 1)
    s = jnp.zeros(bias.shape, F32)
    for c, kv in enumerate(variants):
        s = s + jnp.where(vsel == c, _dot_nt(q8, kv), 0.0)
    p = _masked_softmax(s + bias, ok)
    pb = p.astype(BF16)
    o = jnp.zeros((MLA_HEADS, LANE), F32)
    for c, kv in enumerate(variants):
        o = o + jnp.where(vsel == c, _dot(pb, kv), 0.0)
    return o, p


def _slope_col():
    h = lax.broadcasted_iota(I32, (NSA_HEADS, 1), 0)
    return jnp.exp2(-(h + 1).astype(F32) * (8.0 / NSA_HEADS)) * LOG2E


def _attend_t(q8, kts, vts, bias, ok, new_row):
    hrow = lax.broadcasted_iota(I32, (NSA_HEADS, 1), 0)
    first = (hrow >> GQA_SHIFT) == 0
    s = jnp.where(first, _dot(q8, kts[0]), _dot(q8, kts[1])) + bias
    s = jnp.where(ok, s, NEG)
    new8 = jnp.broadcast_to(new_row, (NSA_HEADS, new_row.shape[1])).astype(BF16).astype(F32)
    kk, vv = new8[:, :LANE], new8[:, LANE:]
    k_new = jnp.where(first, kk, pltpu.roll(kk, HALF, 1))[:, :NSA_HD]
    v_new = jnp.where(first, vv, pltpu.roll(vv, HALF, 1))[:, :NSA_HD]
    s_new = jnp.sum(q8.astype(F32) * k_new, axis=1, keepdims=True)
    m = jnp.maximum(jnp.max(s, axis=1, keepdims=True), s_new)
    p = jnp.exp2(s - m)
    p_new = jnp.exp2(s_new - m)
    l = jnp.sum(p, axis=1, keepdims=True) + p_new
    pb = p.astype(BF16)
    o = jnp.where(first, _dot_nt(pb, vts[0]), _dot_nt(pb, vts[1])) + p_new * v_new
    return o / l


def _sample_cmp_win_kernel(qx_ref, q_ref, kc_ref, wmap_ref, st_ref, kvw_ref, newt_ref,
                           ocmp_ref, imp_ref, owin_ref, wout_ref, *, past_len):
    q8 = qx_ref[...]
    slope = _slope_col()
    nk = kc_ref.shape[0]
    kend = lax.broadcasted_iota(I32, (NSA_HEADS, nk), 1) * CMP_STRIDE + (CMP_BLOCK - 1)
    ok = kend <= past_len
    bias = -slope * (past_len - kend).astype(F32)
    kc = kc_ref[...]
    o, p = _attend8(q8, [kc[:, i * LANE:(i + 1) * LANE] for i in range(4)], bias, ok)
    ocmp_ref[...] = o.astype(ocmp_ref.dtype)
    hrow = lax.broadcasted_iota(I32, (NSA_HEADS, 1), 0)
    for kh in range(NSA_KV_HEADS):
        psum = jnp.sum(jnp.where((hrow >> GQA_SHIFT) == kh, p, 0.0), axis=0, keepdims=True)
        imp_ref[kh:kh + 1, :] = _split_dot(psum, wmap_ref[...])
    st = st_ref[...]
    wbuf = st.shape[1]
    stb = st.astype(BF16)
    j = lax.broadcasted_iota(I32, (NSA_HEADS, wbuf), 1)
    dist = wbuf - j
    okw = (dist <= WINDOW) & (past_len - dist >= 0)
    owin_ref[...] = _attend_t(q_ref[...], [stb[:NSA_HD], stb[NSA_HD:2 * NSA_HD]],
                              [stb[2 * NSA_HD:3 * NSA_HD], stb[3 * NSA_HD:]],
                              -slope * dist.astype(F32), okw, kvw_ref[...])
    newt = newt_ref[...]
    pick = lax.broadcasted_iota(I32, newt.shape, 1) == pl.program_id(0)
    new_col = jnp.sum(jnp.where(pick, newt, 0.0), axis=1, keepdims=True)
    nblk = wbuf // LANE
    rolled = [pltpu.roll(st[:, k * LANE:(k + 1) * LANE], LANE - 1, 1) for k in range(nblk)]
    keep = lax.broadcasted_iota(I32, (st.shape[0], LANE), 1) < LANE - 1
    for k in range(nblk):
        nxt = rolled[k + 1] if k + 1 < nblk else jnp.broadcast_to(new_col, rolled[k].shape)
        wout_ref[:, k * LANE:(k + 1) * LANE] = jnp.where(keep, rolled[k], nxt)


def _sample_cmp_win(qx, q8, kc_var, wmap, state_t, kvw, past_len):
    b = qx.shape[0]
    nk = kc_var.shape[0] // b
    wbuf = state_t.shape[2]
    wcols = wmap.shape[1]
    per_b3 = lambda bi: (bi, 0, 0)
    return pl.pallas_call(
        functools.partial(_sample_cmp_win_kernel, past_len=past_len),
        grid=(b,),
        in_specs=[pl.BlockSpec((None, NSA_HEADS, LANE), per_b3),
                  pl.BlockSpec((None, NSA_HEADS, NSA_HD), per_b3),
                  pl.BlockSpec((nk, 4 * LANE), lambda bi: (bi, 0)),
                  pl.BlockSpec(wmap.shape, lambda bi: (0, 0)),
                  pl.BlockSpec((None, NSA_KV_COLS, wbuf), per_b3),
                  pl.BlockSpec((None, 1, NSA_KV_COLS), per_b3),
                  pl.BlockSpec((NSA_KV_COLS, b), lambda bi: (0, 0))],
        out_specs=[pl.BlockSpec((None, NSA_HEADS, LANE), per_b3),
                   pl.BlockSpec((None, NSA_KV_HEADS, wcols), per_b3),
                   pl.BlockSpec((None, NSA_HEADS, NSA_HD), per_b3),
                   pl.BlockSpec((None, NSA_KV_COLS, wbuf), per_b3)],
        out_shape=[jax.ShapeDtypeStruct((b, NSA_HEADS, LANE), F32),
                   jax.ShapeDtypeStruct((b, NSA_KV_HEADS, wcols), F32),
                   jax.ShapeDtypeStruct((b, NSA_HEADS, NSA_HD), F32),
                   jax.ShapeDtypeStruct((b, NSA_KV_COLS, wbuf), F32)],
        compiler_params=_cparams(("parallel",)),
        name="sample_cmp_win",
    )(qx.reshape(b, NSA_HEADS, LANE), q8, kc_var, wmap, state_t, kvw.reshape(b, 1, NSA_KV_COLS),
      jnp.transpose(kvw))


def _sample_select_kernel(imp_ref, idx_ref, *, cur, n_slc):
    imp = imp_ref[...]
    rank, valid = _select_rank(imp, cur, n_slc)
    r, w = imp.shape
    blk = lax.broadcasted_iota(I32, (r, w), 1).astype(F32)
    lane = lax.broadcasted_iota(I32, (r, LANE), 1)
    out = jnp.zeros((r, LANE), F32)
    for t in range(SLC_TOP_N):
        pick = jnp.sum(jnp.where((rank == t) & valid, blk, 0.0), axis=1, keepdims=True)
        out = jnp.where(lane == t, pick, out)
    idx_ref[...] = out.astype(I32)


def _sample_select(imp, cur, n_slc):
    r = imp.shape[0]
    return pl.pallas_call(
        functools.partial(_sample_select_kernel, cur=cur, n_slc=n_slc),
        grid=(1,),
        in_specs=[pl.BlockSpec(imp.shape, lambda i: (0, 0))],
        out_specs=pl.BlockSpec((r, LANE), lambda i: (0, 0)),
        out_shape=jax.ShapeDtypeStruct((r, LANE), I32),
        compiler_params=_cparams(("arbitrary",)),
        name="sample_select",
    )(imp)


def _sample_slc_kernel(*refs, n_sel, past_len):
    blk_ref, page_ref = refs[0], refs[1]
    del page_ref
    q_ref, new_ref = refs[2], refs[3]
    c_refs = refs[4:4 + NSA_KV_HEADS * n_sel]
    o_ref = refs[4 + NSA_KV_HEADS * n_sel]
    bi = pl.program_id(0)
    hrow = lax.broadcasted_iota(I32, (NSA_HEADS, 1), 0)
    first = (hrow >> GQA_SHIFT) == 0
    nk = n_sel * PAGE
    col = lax.broadcasted_iota(I32, (NSA_HEADS, nk), 1)
    tile = col >> PAGE_SHIFT
    row = col & (PAGE - 1)
    kts, vts, blks = [], [], []
    for kh in range(NSA_KV_HEADS):
        mine = c_refs[kh * n_sel:(kh + 1) * n_sel]
        kts.append(jnp.concatenate([c[kh * NSA_HD:(kh + 1) * NSA_HD, :] for c in mine], axis=1).astype(BF16))
        vts.append(jnp.concatenate([c[(2 + kh) * NSA_HD:(3 + kh) * NSA_HD, :] for c in mine],
                                   axis=1).astype(BF16))
        blkv = jnp.zeros(col.shape, I32)
        for t in range(n_sel):
            blkv = jnp.where(tile == t, blk_ref[bi, kh * n_sel + t], blkv)
        blks.append(blkv)
    blkv = jnp.where(first, blks[0], blks[1])
    kpos = (blkv >> 1) * PAGE + row
    ok = ((row >> SLC_SHIFT) == (blkv & 1)) & (kpos < past_len)
    dist = past_len - kpos
    o_ref[...] = _attend_t(q_ref[...], kts, vts, -_slope_col() * dist.astype(F32), ok, new_ref[...])


def _sample_slc(q8, kvs_new, cache_slc_t, blk_idx, page_idx, past_len):
    b = q8.shape[0]
    n_sel = blk_idx.shape[1] // NSA_KV_HEADS
    per_b3 = lambda bi, blk, pg: (bi, 0, 0)
    in_specs = [pl.BlockSpec((None, NSA_HEADS, NSA_HD), per_b3),
                pl.BlockSpec((None, 1, NSA_KV_COLS), per_b3)]
    in_specs += [pl.BlockSpec((None, NSA_KV_COLS, PAGE),
                              functools.partial(lambda bi, blk, pg, j: (pg[bi, j], 0, 0), j=j))
                 for j in range(NSA_KV_HEADS * n_sel)]
    return pl.pallas_call(
        functools.partial(_sample_slc_kernel, n_sel=n_sel, past_len=past_len),
        grid_spec=pltpu.PrefetchScalarGridSpec(
            num_scalar_prefetch=2, grid=(b,), in_specs=in_specs,
            out_specs=pl.BlockSpec((None, NSA_HEADS, NSA_HD), per_b3)),
        out_shape=jax.ShapeDtypeStruct((b, NSA_HEADS, NSA_HD), F32),
        compiler_params=_cparams(("parallel",)),
        name="sample_slc",
    )(blk_idx, page_idx, q8, kvs_new.reshape(b, 1, NSA_KV_COLS), *([cache_slc_t] * (NSA_KV_HEADS * n_sel)))


def _heads_to_swapped(o8):
    b = o8.shape[0]
    return o8.reshape(b, NSA_HEADS // 2, 2, NSA_HD)[:, :, ::-1].reshape(b, NSA_HEADS * NSA_HD).astype(BF16)


def _pair_swap(o8):
    b = o8.shape[0]
    o = o8.reshape(b, NSA_HEADS // 2, 2, LANE)
    lo = jnp.arange(LANE) < HALF
    return jnp.where(lo, o[:, :, 1], o[:, :, 0]).reshape(b, 4 * LANE).astype(BF16)


def _rope_table(pos):
    half = MLA_ROPE // 2
    freqs = ROPE_THETA ** (-jnp.arange(half, dtype=F32) / half)
    ang = pos.astype(F32)[:, None] * freqs
    cos, sin = jnp.cos(ang), jnp.sin(ang)
    n = pos.shape[0]
    one, zero = jnp.ones((n, MLA_NOPE), F32), jnp.zeros((n, MLA_NOPE), F32)
    pad = jnp.zeros((n, LANE - MLA_NOPE - MLA_ROPE), F32)
    return jnp.concatenate([one, cos, cos, pad, zero, -sin, sin, pad], axis=1)


def _cmp_to_slc(n_cmp, n_slc, rows, cols):
    cs = np.arange(n_cmp)[:, None] * CMP_STRIDE
    ss = np.arange(n_slc)[None, :] * SLC_BLOCK
    inter = np.clip(np.minimum(cs + CMP_BLOCK, ss + SLC_BLOCK) - np.maximum(cs, ss), 0, None)
    w = np.zeros((rows, cols), np.float32)
    w[:n_cmp, :n_slc] = inter.astype(np.float32) / CMP_STRIDE
    return jnp.asarray(w, BF16)


def _prep_weights(w_in, mla_g_q, mla_g_kv, mla_w_uq, mla_w_uk, mla_w_uv, mla_w_o, nsa_pe_k, nsa_w1_k,
                  nsa_w2_k, nsa_pe_v, nsa_w1_v, nsa_w2_v, nsa_w_o, w_out, ln1_g, ln1_b, router_g_w,
                  router_g_b, router_e_w, router_e_b, ln2_g, ln2_b):
    d = w_in.shape[0]
    splits = (MLA_Q_RANK, MLA_KV_RANK, MLA_ROPE, NSA_HEADS * NSA_HD, NSA_KV_COLS, NSA_KV_COLS, NSA_KV_COLS,
              3 * NSA_HEADS, d, d)
    offs = np.cumsum(splits)[:-1].tolist()
    cq, ckv, kr, q, kvc, kvs, kvw, gn, ga, gb = jnp.split(w_in, offs, axis=1)
    r2 = MLA_ROPE // 2
    zc = lambda n: jnp.zeros((d, n), F32)
    tail = LANE - MLA_NOPE - MLA_ROPE
    kr_pad = jnp.concatenate([zc(MLA_NOPE), kr, zc(tail)], axis=1)
    kr_sw = jnp.concatenate([zc(MLA_NOPE), kr[:, r2:], kr[:, :r2], zc(tail)], axis=1)
    gn_pad = jnp.concatenate([gn, zc(LANE - gn.shape[1])], axis=1)
    w_all = jnp.concatenate([cq, ckv, q, kvc, kvs, kvw, kr_pad, kr_sw, gn_pad, ga, gb], axis=1).astype(BF16)

    def pad_heads(w, lo_cols):
        r, hh, c = w.shape
        out = jnp.zeros((r, hh, LANE), F32).at[:, :, lo_cols:lo_cols + c].set(w)
        return out

    nope, rope = mla_w_uq[:, :, :MLA_NOPE], mla_w_uq[:, :, MLA_NOPE:]
    wuq = jnp.concatenate([nope, rope, jnp.zeros(nope.shape[:2] + (tail,), F32)], axis=2)
    wuqs = jnp.concatenate([jnp.zeros_like(nope), rope[:, :, r2:], rope[:, :, :r2],
                            jnp.zeros(nope.shape[:2] + (tail,), F32)], axis=2)
    wuk = pad_heads(mla_w_uk, 0)
    odd = (jnp.arange(MLA_HEADS) % 2 == 1)[None, :, None]
    wuv = jnp.where(odd, pad_heads(mla_w_uv, HALF), pad_heads(mla_w_uv, 0))
    flat = lambda w: w.reshape(w.shape[0], -1).astype(BF16)
    wabs = jnp.zeros((MLA_HEADS, LANE, MLA_KV_RANK), F32).at[:, :MLA_NOPE, :].set(
        jnp.transpose(mla_w_uk, (1, 2, 0))).astype(BF16)
    psel = jnp.zeros((LANE, MLA_ROPE), F32).at[MLA_NOPE:MLA_NOPE + MLA_ROPE].set(jnp.eye(MLA_ROPE)).astype(BF16)

    def w1_cols(w1):
        return jnp.transpose(w1.reshape(2, CMP_STRIDE, NSA_HD, CMP_HIDDEN), (1, 2, 0, 3))

    wt = jnp.stack([w1_cols(nsa_w1_k), w1_cols(nsa_w1_v)])
    eye2 = jnp.eye(2, dtype=F32)
    w_big = jnp.einsum('tpdfh,tu,kv->ptkdfuvh', wt, eye2, eye2).reshape(
        CMP_STRIDE, NSA_KV_COLS, 2 * 2 * NSA_KV_HEADS * CMP_HIDDEN).astype(BF16)
    zpe = jnp.zeros((CMP_BLOCK * NSA_HD, CMP_HIDDEN), F32)
    w1_pe = jnp.concatenate([
        jnp.concatenate([nsa_w1_k, nsa_w1_k, zpe, zpe], axis=1),
        jnp.concatenate([zpe, zpe, nsa_w1_v, nsa_w1_v], axis=1)], axis=0)
    pe_flat = jnp.concatenate([nsa_pe_k.reshape(1, -1), nsa_pe_v.reshape(1, -1)], axis=1)
    w1_hi = w1_pe.astype(BF16)
    w1_lo = (w1_pe - w1_hi.astype(F32)).astype(BF16)
    z64 = jnp.zeros((CMP_HIDDEN, NSA_HD), F32)
    rows = [[nsa_w2_k, z64, z64, z64], [z64, nsa_w2_k, z64, z64], [z64, z64, nsa_w2_v, z64],
            [z64, z64, z64, nsa_w2_v]]
    w2_cmp = jnp.concatenate([jnp.concatenate(r, axis=1) for r in rows], axis=0).astype(BF16)

    perm = np.arange(NSA_HEADS * NSA_HD).reshape(NSA_HEADS // 2, 2, NSA_HD)[:, ::-1].reshape(-1)
    w_on = nsa_w_o[perm].astype(BF16)
    e3 = np.zeros((LANE, 3, NSA_HEADS * NSA_HD), np.float32)
    for h in range(NSA_HEADS):
        pos = (h // 2) * LANE + (0 if h % 2 else HALF)
        for j in range(3):
            e3[3 * h + j, j, pos:pos + NSA_HD] = 1.0
    e3 = jnp.asarray(e3.reshape(LANE, -1), BF16)
    w_r = jnp.concatenate([router_g_w, zc(HALF - N_GROUPS), router_e_w], axis=1)
    w_r_hi = w_r.astype(BF16)
    w_r_lo = (w_r - w_r_hi.astype(F32)).astype(BF16)
    b_r = jnp.concatenate([router_g_b, jnp.zeros((HALF - N_GROUPS,), F32), router_e_b])[None, :]
    return dict(
        w_all=w_all, wuq=flat(wuq), wuqs=flat(wuqs), wuk=flat(wuk), wuv=flat(wuv),
        gq=mla_g_q[None, :], gkv=mla_g_kv[None, :], wabs=wabs, psel=psel, w_big=w_big,
        pe_flat=pe_flat, w1_pe=jnp.stack([w1_hi, w1_lo]), w2_cmp=w2_cmp,
        w_o=mla_w_o.astype(BF16), w_on=w_on, w_out=w_out.astype(BF16), e3=e3,
        ln1_g=ln1_g[None, :], ln1_b=ln1_b[None, :], w_router=jnp.stack([w_r_hi, w_r_lo]), b_router=b_r,
        ln2_g=ln2_g[None, :], ln2_b=ln2_b[None, :])


def _prompt_layer(x, pw, slopes, moe_w):
    b, s, d = x.shape
    n = b * s
    tm = min(256, s)
    z = _in_proj(x.reshape(n, d), _rope_table(jnp.arange(s)), pw, tm)
    o_mla = _flash(z['qm'], z['km'], z['vm'], slopes, b, s, mode='causal', kv_shared=False, alibi=False,
                   swap=False)
    nb = s // CMP_STRIDE
    fs = _cmp_fs_dense(z['kvc'], pw['w_big'])
    kc_var = _cmp_finish(fs, nb, pw)
    n_slc = -(-s // SLC_BLOCK)
    wmap = _cmp_to_slc(nb - 1, n_slc, nb, LANE)
    o_cmp, sel = _cmp_select(z['qn'], kc_var, slopes, wmap, b, s, n_slc)
    emat = jnp.asarray(np.arange(LANE)[:, None] == (np.arange(s)[None, :] // SLC_BLOCK), BF16)
    o_slc = _flash(z['qn'], z['kvsv'], z['kvsv'], slopes, b, s, mode='select', kv_shared=True, alibi=True,
                   swap=True, sel=sel, emat=emat)
    o_win = _flash(z['qn'], z['kvwv'], z['kvwv'], slopes, b, s, mode='window', kv_shared=True, alibi=True,
                   swap=True)
    h, rt = _merge(x.reshape(n, d), o_mla, o_cmp, o_slc, o_win, z, pw, tm)
    y = _moe(h, rt, pw, *moe_w)
    return y.reshape(b, s, d), z


def _sample_layer(x, pw, slopes, moe_w, cache_ckv, cache_kr_t, cache_cmp_t, cache_slc_t, state_t, page_table):
    b, t, d = x.shape
    n_pages = page_table.shape[1]
    past_len = n_pages * PAGE
    z = _in_proj(x.reshape(b, d), _rope_table(jnp.full((b,), past_len)), pw, b)
    qlat = _q_latent(z['qm'], pw['wabs'])
    qr = z['qm'].reshape(b, MLA_HEADS, LANE)[:, :, MLA_NOPE:MLA_NOPE + MLA_ROPE]
    o_lat = _mla_decode(qlat, qr, z['ckv'], z['kr'], cache_ckv, cache_kr_t, page_table)
    o_mla = _mla_out_up(o_lat.reshape(b, MLA_HEADS * MLA_KV_RANK), pw['wuv'])
    nb = past_len // CMP_STRIDE
    fs = _cmp_fs_paged(cache_cmp_t, page_table, pw['w_big'])
    kc_var = _cmp_finish(fs, nb, pw)
    n_slc = -(-(past_len + t) // SLC_BLOCK)
    assert n_slc >= SLC_TOP_N and past_len % SLC_BLOCK == 0
    wcols = -(-n_slc // LANE) * LANE
    wmap = _cmp_to_slc(nb - 1, n_slc, nb, wcols)
    q8 = z['qnat'].reshape(b, NSA_HEADS, NSA_HD)
    o_cmp8, imp, o_win8, win_out = _sample_cmp_win(z['qn'], q8, kc_var, wmap, state_t, z['kvw'], past_len)
    cur = past_len // SLC_BLOCK
    idx = _sample_select(imp.reshape(b * NSA_KV_HEADS, wcols), cur, n_slc)[:, :SLC_TOP_N]
    blk = idx.reshape(b, NSA_KV_HEADS * SLC_TOP_N)
    safe = jnp.minimum(blk, past_len // SLC_BLOCK - 1)
    pages = jnp.take_along_axis(page_table, safe // (PAGE // SLC_BLOCK), axis=1).astype(I32)
    o_slc8 = _sample_slc(q8, z['kvs'], cache_slc_t, blk, pages, past_len)
    h, rt = _merge(x.reshape(b, d), o_mla, _pair_swap(o_cmp8), _heads_to_swapped(o_slc8),
                   _heads_to_swapped(o_win8), z, pw, b)
    y = _moe(h, rt, pw, *moe_w)
    return y.reshape(b, t, d), z, win_out


def kernel(x_prompt, x_sample, cache_mla_ckv, cache_mla_krope, cache_nsa_cmp_kv, cache_nsa_slc_kv,
           state_nsa_win_kv, page_table, w_in, mla_g_q, mla_g_kv, mla_w_uq, mla_w_uk, mla_w_uv, mla_w_o,
           nsa_pe_k, nsa_w1_k, nsa_w2_k, nsa_pe_v, nsa_w1_v, nsa_w2_v, nsa_w_o, w_out, ln1_g, ln1_b,
           router_g_w, router_g_b, router_e_w, router_e_b, moe_w1, moe_w3, moe_w2, ln2_g, ln2_b):
    assert w_in.shape[0] == DEPTH and x_sample.shape[1] == 1
    b, s, d = x_prompt.shape
    bs = x_sample.shape[0]
    kv_shape = (2, NSA_KV_HEADS, NSA_HD)
    pw = _prep_weights(w_in[0], mla_g_q[0], mla_g_kv[0], mla_w_uq[0], mla_w_uk[0], mla_w_uv[0], mla_w_o[0],
                       nsa_pe_k[0], nsa_w1_k[0], nsa_w2_k[0], nsa_pe_v[0], nsa_w1_v[0], nsa_w2_v[0],
                       nsa_w_o[0], w_out[0], ln1_g[0], ln1_b[0], router_g_w[0], router_g_b[0],
                       router_e_w[0], router_e_b[0], ln2_g[0], ln2_b[0])
    slopes = 2.0 ** (-8.0 * jnp.arange(1, NSA_HEADS + 1, dtype=F32) / NSA_HEADS)
    moe_w = (moe_w1, moe_w3, moe_w2)
    y_p, zp = _prompt_layer(x_prompt, pw, slopes, moe_w)
    def rows_last(c):
        return jnp.transpose(c, (0, 2, 3, 4, 1)).reshape(c.shape[0], NSA_KV_COLS, c.shape[1])

    y_s, zs, win_t = _sample_layer(
        x_sample, pw, slopes, moe_w, cache_mla_ckv[0], jnp.transpose(cache_mla_krope[0], (0, 2, 1)),
        rows_last(cache_nsa_cmp_kv[0]), rows_last(cache_nsa_slc_kv[0]), rows_last(state_nsa_win_kv[0]),
        page_table)
    win_s = jnp.transpose(win_t.reshape((bs,) + kv_shape + (-1,)), (0, 4, 1, 2, 3))
    wp = min(WINDOW, s)
    return (y_p, y_s,
            zp['ckv'].reshape(1, b, s, MLA_KV_RANK), zs['ckv'].reshape(1, bs, 1, MLA_KV_RANK),
            zp['kr'].reshape(1, b, s, MLA_ROPE), zs['kr'].reshape(1, bs, 1, MLA_ROPE),
            zp['kvc'].reshape((1, b, s) + kv_shape), zs['kvc'].reshape((1, bs, 1) + kv_shape),
            zp['kvs'].reshape((1, b, s) + kv_shape), zs['kvs'].reshape((1, bs, 1) + kv_shape),
            zp['kvw'].reshape(b, s, NSA_KV_COLS)[:, s - wp:].reshape((1, b, wp) + kv_shape),
            win_s.reshape((1, bs, -1) + kv_shape))
```

```python
import functools

import numpy as np
import jax
import jax.numpy as jnp
from jax import lax
from jax.experimental import pallas as pl
from jax.experimental.pallas import tpu as pltpu

F32 = jnp.float32
BF16 = jnp.bfloat16
I32 = jnp.int32

PAGE = 128
MLA_HEADS = 8
MLA_NOPE = 64
MLA_ROPE = 32
MLA_V = 64
MLA_Q_RANK = 256
MLA_KV_RANK = 256
ROPE_THETA = 10000.0
NSA_HEADS = 8
NSA_KV_HEADS = 2
NSA_GQA = NSA_HEADS // NSA_KV_HEADS
NSA_HD = 64
CMP_BLOCK = 32
CMP_STRIDE = 16
CMP_HIDDEN = 64
SLC_BLOCK = 64
SLC_TOP_N = 16
N_LOCAL_SLC = 2
WINDOW = 512
FORCE_SCORE = 1.0e4
N_GROUPS = 8
EXPERTS_PER_GROUP = 8
N_EXPERTS = N_GROUPS * EXPERTS_PER_GROUP
TOP_K = 2
D_EXPERT = 256
MOE_BLOCK = 128
LN_EPS = 1e-5
RMS_EPS = 1e-6
DEPTH = 1
DEEPNORM_ALPHA = (2.0 * DEPTH) ** 0.25
NSA_KV_COLS = 2 * NSA_KV_HEADS * NSA_HD

PAGE_SHIFT = PAGE.bit_length() - 1
SLC_SHIFT = SLC_BLOCK.bit_length() - 1
GQA_SHIFT = NSA_GQA.bit_length() - 1
LANE = 128
HALF = LANE // 2
VMEM_LIMIT = 56 * 1024 * 1024
NEG = -1e30
LOG2E = 1.4426950408889634

_C_CQ, _C_CKV, _C_Q, _C_KVC, _C_KVS, _C_KVW, _C_KRP, _C_KRS, _C_GN, _C_GA = (
    0, 256, 512, 1024, 1280, 1536, 1792, 1920, 2048, 2176)


def _cparams(sem):
    return pltpu.CompilerParams(dimension_semantics=sem, vmem_limit_bytes=VMEM_LIMIT)


def _dot(a, b):
    return jnp.dot(a, b, preferred_element_type=F32)


def _dot_nt(a, b):
    return lax.dot_general(a, b, (((1,), (1,)), ((), ())), preferred_element_type=F32)


def _split_dot(a, w_hi, w_lo=None):
    a_hi = a.astype(BF16)
    a_lo = (a - a_hi.astype(F32)).astype(BF16)
    r = _dot(a_hi, w_hi) + _dot(a_lo, w_hi)
    if w_lo is not None:
        r = r + _dot(a_hi, w_lo)
    return r


def _sigmoid(x):
    return 1.0 / (1.0 + jnp.exp(-x))


def _kv_variants(x0, x1):
    lo = lax.broadcasted_iota(I32, x0.shape, 1) < HALF
    r0 = pltpu.roll(x0, HALF, 1)
    r1 = pltpu.roll(x1, HALF, 1)
    return (jnp.where(lo, x0, r1), jnp.where(lo, x1, r0), jnp.where(lo, r0, x1), jnp.where(lo, r1, x0))


def _inproj_kernel(x_ref, cs_ref, w_ref, wuq_ref, wuqs_ref, wuk_ref, wuv_ref, gq_ref, gkv_ref,
                   qm_ref, km_ref, vm_ref, ckv_ref, kr_ref, qn_ref, kvc_ref, kvs_ref, kvw_ref,
                   kvsv_ref, kvwv_ref, gn_ref, ga_ref, gb_ref, qnat_ref, kvsvt_ref, kvwvt_ref, *, mla_scale):
    xb = x_ref[...].astype(BF16)

    def proj(a, b):
        return _dot(xb, w_ref[:, a:b])

    cos = cs_ref[:, :LANE]
    sin = cs_ref[:, LANE:]

    def rms(z, g):
        return z * lax.rsqrt(jnp.mean(z * z, axis=-1, keepdims=True) + RMS_EPS) * g

    cq = rms(proj(_C_CQ, _C_CKV), gq_ref[...]).astype(BF16)
    q = _dot(cq, wuq_ref[...])
    qs = _dot(cq, wuqs_ref[...])
    ckv = rms(proj(_C_CKV, _C_Q), gkv_ref[...])
    ckv_ref[...] = ckv
    ckvb = ckv.astype(BF16)
    kr = proj(_C_KRP, _C_KRS) * cos + proj(_C_KRS, _C_GN) * sin
    kr_ref[...] = kr[:, MLA_NOPE:MLA_NOPE + MLA_ROPE]
    kn = _dot(ckvb, wuk_ref[...])
    v = _dot(ckvb, wuv_ref[...])
    for h in range(MLA_HEADS):
        sl = slice(h * LANE, (h + 1) * LANE)
        qm_ref[:, sl] = ((q[:, sl] * cos + qs[:, sl] * sin) * mla_scale).astype(BF16)
        km_ref[:, sl] = (kn[:, sl] + kr).astype(BF16)
        vm_ref[sl, :] = v[:, sl].T.astype(BF16)
    qn = proj(_C_Q, _C_KVC) * (NSA_HD ** -0.5 * LOG2E)
    qnat_ref[...] = qn.astype(BF16)
    lo = lax.broadcasted_iota(I32, (qn.shape[0], LANE), 1) < HALF
    for j in range(NSA_HEADS // 2):
        blk = qn[:, j * LANE:(j + 1) * LANE]
        qn_ref[:, (2 * j) * LANE:(2 * j + 1) * LANE] = jnp.where(lo, blk, 0.0).astype(BF16)
        qn_ref[:, (2 * j + 1) * LANE:(2 * j + 2) * LANE] = jnp.where(lo, 0.0, blk).astype(BF16)
    kvc_ref[...] = proj(_C_KVC, _C_KVS)
    for src, dst, var, vart in ((_C_KVS, kvs_ref, kvsv_ref, kvsvt_ref), (_C_KVW, kvw_ref, kvwv_ref, kvwvt_ref)):
        z = proj(src, src + NSA_KV_COLS)
        dst[...] = z
        for i, t in enumerate(_kv_variants(z[:, :LANE], z[:, LANE:])):
            var[:, i * LANE:(i + 1) * LANE] = t.astype(BF16)
            vart[i * LANE:(i + 1) * LANE, :] = t.T.astype(BF16)
    gn_ref[...] = _sigmoid(proj(_C_GN, _C_GA))
    d = ga_ref.shape[1]
    ga_ref[...] = _sigmoid(proj(_C_GA, _C_GA + d)).astype(BF16)
    gb_ref[...] = _sigmoid(proj(_C_GA + d, _C_GA + 2 * d)).astype(BF16)


def _in_proj(x, cs, pw, tm):
    n, d = x.shape
    n_cs = cs.shape[0] // tm
    row = lambda i: (i, 0)
    full = lambda i: (0, 0)
    wcols = pw['w_all'].shape[1]
    outs = [
        ('qm', 8 * LANE, BF16), ('km', 8 * LANE, BF16), ('vm', 8 * LANE, BF16),
        ('ckv', MLA_KV_RANK, F32), ('kr', MLA_ROPE, F32), ('qn', 8 * LANE, BF16),
        ('kvc', NSA_KV_COLS, F32), ('kvs', NSA_KV_COLS, F32), ('kvw', NSA_KV_COLS, F32),
        ('kvsv', 4 * LANE, BF16), ('kvwv', 4 * LANE, BF16), ('gn', LANE, F32),
        ('ga', d, BF16), ('gb', d, BF16), ('qnat', NSA_HEADS * NSA_HD, BF16),
        ('kvsvt', 4 * LANE, BF16), ('kvwvt', 4 * LANE, BF16)]
    transposed = ('vm', 'kvsvt', 'kvwvt')
    res = pl.pallas_call(
        functools.partial(_inproj_kernel, mla_scale=(MLA_NOPE + MLA_ROPE) ** -0.5 * LOG2E),
        grid=(n // tm,),
        in_specs=[pl.BlockSpec((tm, d), row),
                  pl.BlockSpec((tm, 2 * LANE), lambda i: (i % n_cs, 0)),
                  pl.BlockSpec((d, wcols), full),
                  pl.BlockSpec((MLA_Q_RANK, 8 * LANE), full),
                  pl.BlockSpec((MLA_Q_RANK, 8 * LANE), full),
                  pl.BlockSpec((MLA_KV_RANK, 8 * LANE), full),
                  pl.BlockSpec((MLA_KV_RANK, 8 * LANE), full),
                  pl.BlockSpec((1, MLA_Q_RANK), full),
                  pl.BlockSpec((1, MLA_KV_RANK), full)],
        out_specs=[pl.BlockSpec((c, tm), lambda i: (0, i)) if k in transposed else pl.BlockSpec((tm, c), row)
                   for k, c, _ in outs],
        out_shape=[jax.ShapeDtypeStruct((c, n) if k in transposed else (n, c), t) for k, c, t in outs],
        compiler_params=_cparams(("parallel",)),
        name="in_proj",
    )(x, cs, pw['w_all'], pw['wuq'], pw['wuqs'], pw['wuk'], pw['wuv'], pw['gq'], pw['gkv'])
    return {k: v for (k, _, _), v in zip(outs, res)}


def _flash_kernel(*refs, mode, t, alibi, swap):
    if mode == 'select':
        slopes_ref, q_ref, k_ref, vt_ref, selt_ref, et_ref, o_ref, m_scr, l_scr, acc_scr = refs
    else:
        slopes_ref, q_ref, k_ref, vt_ref, o_ref, m_scr, l_scr, acc_scr = refs
    hp = pl.program_id(1)
    qi = pl.program_id(2)
    q0 = qi * t
    krow = lax.broadcasted_iota(I32, (t, t), 0)
    dmat = lax.broadcasted_iota(I32, (t, t), 1) - krow
    krow_f = krow.astype(F32)
    qs = [q_ref[:, e * LANE:(e + 1) * LANE] for e in range(2)]
    for e in range(2):
        m_scr[e] = jnp.full((1, t), NEG, F32)
        l_scr[e] = jnp.zeros((1, t), F32)
        acc_scr[e] = jnp.zeros((LANE, t), F32)

    def step(kt, edge):
        k0 = pl.multiple_of(kt * t, t)
        ok = None
        if edge:
            dist = dmat + (q0 - k0)
            ok = dist >= 0
            if mode == 'window':
                ok = ok & (dist <= WINDOW)
        if mode == 'select':
            picked = _dot(et_ref[pl.ds(k0, t), :], selt_ref[...]) > 0.5
            ok = picked if ok is None else ok & picked
        for e in range(2):
            k = k_ref[pl.ds(k0, t), e * LANE:(e + 1) * LANE]
            vt = vt_ref[e * LANE:(e + 1) * LANE, pl.ds(k0, t)]
            s = _dot_nt(k, qs[e])
            shift = 0.0
            if alibi:
                slope = slopes_ref[2 * hp + e] * LOG2E
                s = s + slope * krow_f
                shift = slope * (k0 - q0).astype(F32)
            if ok is not None:
                s = jnp.where(ok, s, NEG)
            m_prev = m_scr[e]
            m_new = jnp.maximum(m_prev, jnp.max(s, axis=0, keepdims=True) + shift)
            alpha = jnp.exp2(m_prev - m_new)
            p = jnp.exp2(s - (m_new - shift))
            l_scr[e] = alpha * l_scr[e] + jnp.sum(p, axis=0, keepdims=True)
            acc_scr[e] = alpha * acc_scr[e] + _dot(vt, p.astype(BF16))
            m_scr[e] = m_new

    def loop(lo, hi, edge):
        def body(kt, c):
            step(kt, edge)
            return c
        lax.fori_loop(lo, hi, body, 0)

    if mode == 'window':
        loop(jnp.maximum(q0 - WINDOW, 0) // t, qi + 1, True)
    else:
        loop(0, qi, False)
        step(qi, True)
    o0 = acc_scr[0] / l_scr[0]
    o1 = acc_scr[1] / l_scr[1]
    lo = lax.broadcasted_iota(I32, (LANE, t), 0) < HALF
    o_ref[...] = (jnp.where(lo, o1, o0) if swap else jnp.where(lo, o0, o1)).T.astype(o_ref.dtype)


def _flash(q, k, vt, slopes, b, s, *, mode, kv_shared, alibi, swap, selt=None, emat_t=None):
    tq = min(512, s)
    nq = s // tq
    k_idx = (lambda bi, hp, qi: (bi, hp // 2)) if kv_shared else (lambda bi, hp, qi: (bi, hp))
    vt_idx = (lambda bi, hp, qi: (hp // 2, bi)) if kv_shared else (lambda bi, hp, qi: (hp, bi))
    in_specs = [pl.BlockSpec(memory_space=pltpu.SMEM),
                pl.BlockSpec((tq, 2 * LANE), lambda bi, hp, qi: (bi * nq + qi, hp)),
                pl.BlockSpec((s, 2 * LANE), k_idx),
                pl.BlockSpec((2 * LANE, s), vt_idx)]
    args = [slopes, q, k, vt]
    if mode == 'select':
        in_specs += [pl.BlockSpec((None, None, LANE, tq), lambda bi, hp, qi: (bi, hp // 2, 0, qi)),
                     pl.BlockSpec((s, LANE), lambda bi, hp, qi: (0, 0))]
        args += [selt, emat_t]
    return pl.pallas_call(
        functools.partial(_flash_kernel, mode=mode, t=tq, alibi=alibi, swap=swap),
        grid=(b, 4, nq),
        in_specs=in_specs,
        out_specs=pl.BlockSpec((tq, LANE), lambda bi, hp, qi: (bi * nq + qi, hp)),
        out_shape=jax.ShapeDtypeStruct((b * s, 4 * LANE), BF16),
        scratch_shapes=[pltpu.VMEM((2, 1, tq), F32), pltpu.VMEM((2, 1, tq), F32),
                        pltpu.VMEM((2, LANE, tq), F32)],
        compiler_params=_cparams(("parallel", "parallel", "arbitrary")),
        name="flash_" + mode,
    )(*args)


def _cmp_fs_kernel(*refs, n_pages):
    if n_pages:
        x_refs = refs[1:1 + n_pages]
        w_ref, o_ref, lo, hi = refs[1 + n_pages:]
        for j, xr in enumerate(x_refs):
            lo[j * PAGE:(j + 1) * PAGE, :] = xr[:LANE, :].T
            hi[j * PAGE:(j + 1) * PAGE, :] = xr[LANE:, :].T
    else:
        lo, hi, w_ref, o_ref = refs
    r = o_ref.shape[0]
    acc = jnp.zeros(o_ref.shape, F32)
    for p in range(CMP_STRIDE):
        rows = pl.ds(p, r, stride=CMP_STRIDE)
        x = jnp.concatenate([lo[rows, :], hi[rows, :]], axis=1)
        acc = acc + _dot(x.astype(BF16), w_ref[p])
    o_ref[...] = acc


def _cmp_fs_dense(x, w_big):
    n, c = x.shape
    tr = min(256, n // CMP_STRIDE)
    return pl.pallas_call(
        functools.partial(_cmp_fs_kernel, n_pages=0),
        grid=(n // CMP_STRIDE // tr,),
        in_specs=[pl.BlockSpec((tr * CMP_STRIDE, LANE), lambda i: (i, 0)),
                  pl.BlockSpec((tr * CMP_STRIDE, LANE), lambda i: (i, 1)),
                  pl.BlockSpec(w_big.shape, lambda i: (0, 0, 0))],
        out_specs=pl.BlockSpec((tr, 4 * LANE), lambda i: (i, 0)),
        out_shape=jax.ShapeDtypeStruct((n // CMP_STRIDE, 4 * LANE), F32),
        compiler_params=_cparams(("parallel",)),
        name="cmp_fs",
    )(x, x, w_big)


def _cmp_fs_paged(cache_t, page_table, w_big):
    b, n_pages = page_table.shape
    rows = PAGE // CMP_STRIDE
    pp = min(32, n_pages)
    nc = n_pages // pp
    in_specs = [pl.BlockSpec((None, NSA_KV_COLS, PAGE),
                             functools.partial(lambda bi, ci, pt, j: (pt[bi, ci * pp + j], 0, 0), j=j))
                for j in range(pp)]
    in_specs.append(pl.BlockSpec(w_big.shape, lambda bi, ci, pt: (0, 0, 0)))
    return pl.pallas_call(
        functools.partial(_cmp_fs_kernel, n_pages=pp),
        grid_spec=pltpu.PrefetchScalarGridSpec(
            num_scalar_prefetch=1, grid=(b, nc), in_specs=in_specs,
            out_specs=pl.BlockSpec((pp * rows, 4 * LANE), lambda bi, ci, pt: (bi * nc + ci, 0)),
            scratch_shapes=[pltpu.VMEM((pp * PAGE, LANE), F32), pltpu.VMEM((pp * PAGE, LANE), F32)]),
        out_shape=jax.ShapeDtypeStruct((b * n_pages * rows, 4 * LANE), F32),
        compiler_params=_cparams(("parallel", "arbitrary")),
        name="cmp_fs_paged",
    )(page_table, *([cache_t] * pp), w_big)


def _gelu_tanh(x):
    return 0.5 * x * (1.0 + jnp.tanh(0.7978845608028654 * (x + 0.044715 * x * x * x)))


def _cmp_finish_kernel(fs_ref, pe_ref, w1_ref, w2_ref, o_ref):
    fs = fs_ref[...]
    n = fs.shape[0]
    pe = _split_dot(pe_ref[...], w1_ref[0], w1_ref[1])
    h = _gelu_tanh(fs[:, :2 * LANE] + pltpu.roll(fs[:, 2 * LANE:], n - 1, 0) + pe)
    kc = _dot(h.astype(BF16), w2_ref[...])
    for i, t in enumerate(_kv_variants(kc[:, :LANE], kc[:, LANE:])):
        o_ref[:, i * LANE:(i + 1) * LANE] = t.astype(BF16)


def _cmp_finish(fs, nb, pw):
    r = fs.shape[0]
    full2 = lambda i: (0, 0)
    return pl.pallas_call(
        _cmp_finish_kernel,
        grid=(r // nb,),
        in_specs=[pl.BlockSpec((nb, 4 * LANE), lambda i: (i, 0)),
                  pl.BlockSpec(pw['pe_flat'].shape, full2),
                  pl.BlockSpec(pw['w1_pe'].shape, lambda i: (0, 0, 0)),
                  pl.BlockSpec(pw['w2_cmp'].shape, full2)],
        out_specs=pl.BlockSpec((nb, 4 * LANE), lambda i: (i, 0)),
        out_shape=jax.ShapeDtypeStruct((r, 4 * LANE), BF16),
        compiler_params=_cparams(("parallel",)),
        name="cmp_finish",
    )(fs, pw['pe_flat'], pw['w1_pe'], pw['w2_cmp'])


def _masked_softmax(s, ok):
    s = jnp.where(ok, s, -jnp.inf)
    m = jnp.max(s, axis=-1, keepdims=True)
    m = jnp.where(m > -jnp.inf, m, 0.0)
    e = jnp.exp2(s - m)
    d = jnp.sum(e, axis=-1, keepdims=True)
    return e / jnp.where(d > 0.0, d, 1.0)


def _select_rank(imp, cur, n_slc):
    r, w = imp.shape
    blk = lax.broadcasted_iota(I32, (r, w), 1)
    valid = (blk <= cur) & (blk < n_slc)
    forced = (blk == 0) | (valid & (blk > cur - N_LOCAL_SLC))
    score = jnp.where(forced, FORCE_SCORE, jnp.where(valid, imp, -1.0))
    score = jnp.where(blk < n_slc, score, -2.0)

    def body(i, rank):
        col = jnp.sum(jnp.where(blk == i, score, 0.0), axis=1, keepdims=True)
        beats = (col > score) | ((col == score) & (i < blk))
        return rank + jnp.where(beats, 1.0, 0.0)

    rank = lax.fori_loop(0, n_slc, body, jnp.zeros((r, w), F32), unroll=True if n_slc <= 32 else 4)
    return rank, valid


def _cmp_select_kernel(slopes_ref, q_ref, kc_ref, wmap_ref, o_ref, sel_ref, *, tq, n_slc):
    kvh = pl.program_id(1)
    q0 = pl.program_id(2) * tq
    nk = kc_ref.shape[0]
    qpos = q0 + lax.broadcasted_iota(I32, (tq, nk), 0)
    kend = lax.broadcasted_iota(I32, (tq, nk), 1) * CMP_STRIDE + (CMP_BLOCK - 1)
    ok = kend <= qpos
    dist = (qpos - kend).astype(F32)
    psum = jnp.zeros((tq, nk), F32)
    outs = []
    for g in range(NSA_GQA):
        kv = kc_ref[:, (g % 2) * LANE:(g % 2 + 1) * LANE]
        s = _dot_nt(q_ref[:, g * LANE:(g + 1) * LANE], kv) - (slopes_ref[kvh * NSA_GQA + g] * LOG2E) * dist
        p = _masked_softmax(s, ok)
        psum = psum + p
        outs.append(_dot(p.astype(BF16), kv))
    lo = lax.broadcasted_iota(I32, (tq, LANE), 1) < HALF
    o_ref[:, :LANE] = jnp.where(lo, outs[1], outs[0]).astype(o_ref.dtype)
    o_ref[:, LANE:] = jnp.where(lo, outs[3], outs[2]).astype(o_ref.dtype)
    imp = _split_dot(psum, wmap_ref[...])
    cur = (q0 + lax.broadcasted_iota(I32, (tq, 1), 0)) >> SLC_SHIFT
    rank, valid = _select_rank(imp, cur, n_slc)
    sel_ref[...] = jnp.where((rank < SLC_TOP_N) & valid, 1.0, 0.0).T.astype(sel_ref.dtype)


def _cmp_select(qn, kc_var, slopes, wmap, b, s, n_slc):
    tq = min(256, s)
    nq = s // tq
    nk = kc_var.shape[0] // b
    return pl.pallas_call(
        functools.partial(_cmp_select_kernel, tq=tq, n_slc=n_slc),
        grid=(b, NSA_KV_HEADS, nq),
        in_specs=[pl.BlockSpec(memory_space=pltpu.SMEM),
                  pl.BlockSpec((tq, 4 * LANE), lambda bi, kh, qi: (bi * nq + qi, kh)),
                  pl.BlockSpec((nk, 2 * LANE), lambda bi, kh, qi: (bi, kh)),
                  pl.BlockSpec(wmap.shape, lambda bi, kh, qi: (0, 0))],
        out_specs=[pl.BlockSpec((tq, 2 * LANE), lambda bi, kh, qi: (bi * nq + qi, kh)),
                   pl.BlockSpec((None, None, LANE, tq), lambda bi, kh, qi: (bi, kh, 0, qi))],
        out_shape=[jax.ShapeDtypeStruct((b * s, 4 * LANE), BF16),
                   jax.ShapeDtypeStruct((b, NSA_KV_HEADS, LANE, s), BF16)],
        compiler_params=_cparams(("parallel", "parallel", "parallel")),
        name="cmp_select",
    )(slopes, qn, kc_var, wmap)


def _layer_norm(x, g, b):
    mu = jnp.mean(x, axis=-1, keepdims=True)
    xc = x - mu
    var = jnp.mean(xc * xc, axis=-1, keepdims=True)
    return xc * lax.rsqrt(var + LN_EPS) * g + b


def _merge_kernel(x_ref, omla_ref, ocmp_ref, oslc_ref, owin_ref, gn_ref, ga_ref, gb_ref,
                  wo_ref, wn_ref, wout_ref, e3_ref, g1_ref, b1_ref, wr_ref, br_ref,
                  h_ref, rt_ref):
    gn = gn_ref[...]
    w = 4 * LANE
    ge = _split_dot(gn, e3_ref[...])
    o_nsa = (ge[:, :w] * ocmp_ref[...].astype(F32) + ge[:, w:2 * w] * oslc_ref[...].astype(F32)
             + ge[:, 2 * w:] * owin_ref[...].astype(F32))
    u = (ga_ref[...].astype(F32) * _dot(omla_ref[...], wo_ref[...])
         + gb_ref[...].astype(F32) * _dot(o_nsa.astype(BF16), wn_ref[...]))
    h = _layer_norm(DEEPNORM_ALPHA * x_ref[...] + _dot(u.astype(BF16), wout_ref[...]),
                    g1_ref[...], b1_ref[...])
    h_ref[...] = h
    lg = _split_dot(h, wr_ref[0], wr_ref[1]) + br_ref[...]
    tm = lg.shape[0]
    lane = lax.broadcasted_iota(I32, (tm, LANE), 1)
    is_g = lane < N_GROUPS
    gm = jnp.max(jnp.where(is_g, lg, -jnp.inf), axis=1, keepdims=True)
    gidx = jnp.min(jnp.where(is_g & (lg == gm), lane, LANE), axis=1, keepdims=True)
    p_group = 1.0 / jnp.sum(jnp.where(is_g, jnp.exp(lg - gm), 0.0), axis=1, keepdims=True)
    in_g = (lane >= HALF) & (((lane - HALF) >> 3) == gidx)
    m1 = jnp.max(jnp.where(in_g, lg, -jnp.inf), axis=1, keepdims=True)
    i1 = jnp.min(jnp.where(in_g & (lg == m1), lane, 2 * LANE), axis=1, keepdims=True)
    rest = in_g & (lane != i1)
    m2 = jnp.max(jnp.where(rest, lg, -jnp.inf), axis=1, keepdims=True)
    i2 = jnp.min(jnp.where(rest & (lg == m2), lane, 2 * LANE), axis=1, keepdims=True)
    t = jnp.exp(m2 - m1)
    g1 = p_group / (1.0 + t)
    g2 = p_group * t / (1.0 + t)
    rt = jnp.where(lane == 0, (i1 - HALF).astype(F32),
                   jnp.where(lane == 1, (i2 - HALF).astype(F32),
                             jnp.where(lane == 2, g1, jnp.where(lane == 3, g2, 0.0))))
    rt_ref[...] = rt


def _merge(x, omla, ocmp, oslc, owin, z, pw, tm):
    n, d = x.shape
    row = lambda i: (i, 0)
    full = lambda i: (0, 0)
    w = 4 * LANE
    return pl.pallas_call(
        _merge_kernel,
        grid=(n // tm,),
        in_specs=[pl.BlockSpec((tm, d), row), pl.BlockSpec((tm, w), row), pl.BlockSpec((tm, w), row),
                  pl.BlockSpec((tm, w), row), pl.BlockSpec((tm, w), row), pl.BlockSpec((tm, LANE), row),
                  pl.BlockSpec((tm, d), row), pl.BlockSpec((tm, d), row),
                  pl.BlockSpec((w, d), full), pl.BlockSpec((w, d), full), pl.BlockSpec((d, d), full),
                  pl.BlockSpec((LANE, 3 * w), full), pl.BlockSpec((1, d), full), pl.BlockSpec((1, d), full),
                  pl.BlockSpec((2, d, LANE), lambda i: (0, 0, 0)), pl.BlockSpec((1, LANE), full)],
        out_specs=[pl.BlockSpec((tm, d), row), pl.BlockSpec((tm, LANE), row)],
        out_shape=[jax.ShapeDtypeStruct((n, d), F32), jax.ShapeDtypeStruct((n, LANE), F32)],
        compiler_params=_cparams(("parallel",)),
        name="merge_router",
    )(x, omla, ocmp, oslc, owin, z['gn'], z['ga'], z['gb'], pw['w_o'], pw['w_on'], pw['w_out'],
      pw['e3'], pw['ln1_g'], pw['ln1_b'], pw['w_router'], pw['b_router'])


DMA_UNROLL = 4


def _row_copy(src, dst, sem, src_row, dst_row):
    return pltpu.make_async_copy(src.at[pl.ds(src_row, 1)], dst.at[pl.ds(dst_row, 1)], sem)


def _moe_rank_kernel(rt_ref, tri_ref, rank_ref, cnt_ref, run_scr):
    @pl.when(pl.program_id(0) == 0)
    def _():
        run_scr[...] = jnp.zeros_like(run_scr)

    rt = rt_ref[...]
    lane = lax.broadcasted_iota(I32, rt.shape, 1)
    lanef = lane.astype(F32)
    e0 = jnp.sum(jnp.where(lane == 0, rt, 0.0), axis=1, keepdims=True)
    e1 = jnp.sum(jnp.where(lane == 1, rt, 0.0), axis=1, keepdims=True)
    hit0 = lanef == e0
    hit1 = lanef == e1
    onehot = jnp.where(hit0 | hit1, 1.0, 0.0)
    before = _dot(tri_ref[...], onehot.astype(BF16)) + run_scr[...]
    r0 = jnp.sum(jnp.where(hit0, before, 0.0), axis=1, keepdims=True)
    r1 = jnp.sum(jnp.where(hit1, before, 0.0), axis=1, keepdims=True)
    rank_ref[...] = jnp.where(lane == 0, r0, jnp.where(lane == 1, r1, 0.0))
    run = run_scr[...] + jnp.sum(onehot, axis=0, keepdims=True)
    run_scr[...] = run
    cnt_ref[...] = run


def _moe_rank(rt):
    n = rt.shape[0]
    tm = min(512, n)
    tri = jnp.asarray(np.tril(np.ones((tm, tm), np.float32), -1), BF16)
    return pl.pallas_call(
        _moe_rank_kernel,
        grid=(n // tm,),
        in_specs=[pl.BlockSpec((tm, LANE), lambda i: (i, 0)), pl.BlockSpec((tm, tm), lambda i: (0, 0))],
        out_specs=[pl.BlockSpec((tm, LANE), lambda i: (i, 0)), pl.BlockSpec((1, LANE), lambda i: (0, 0))],
        out_shape=[jax.ShapeDtypeStruct((n, LANE), F32), jax.ShapeDtypeStruct((1, LANE), F32)],
        scratch_shapes=[pltpu.VMEM((1, LANE), F32)],
        compiler_params=_cparams(("arbitrary",)),
        name="moe_rank",
    )(rt, tri)


def _moe_dispatch_kernel(dest_ref, h_ref, xs_in, xs_out, sem, *, tm):
    del xs_in

    def copies(t):
        return [_row_copy(h_ref, xs_out, sem, t, dest_ref[0, 0, TOP_K * t + k]) for k in range(TOP_K)]

    def start(t, c):
        for cp in copies(t):
            cp.start()
        return c

    def wait(t, c):
        for cp in copies(t):
            cp.wait()
        return c

    lax.fori_loop(0, tm, start, 0, unroll=DMA_UNROLL)
    lax.fori_loop(0, tm, wait, 0, unroll=DMA_UNROLL)


def _moe_dispatch(h, dest, n_slots):
    n, d = h.shape
    tm = min(256, n)
    return pl.pallas_call(
        functools.partial(_moe_dispatch_kernel, tm=tm),
        grid=(n // tm,),
        in_specs=[pl.BlockSpec((1, 1, TOP_K * tm), lambda i: (i, 0, 0), memory_space=pltpu.SMEM),
                  pl.BlockSpec((tm, d), lambda i: (i, 0)),
                  pl.BlockSpec(memory_space=pl.ANY)],
        out_specs=pl.BlockSpec(memory_space=pl.ANY),
        out_shape=jax.ShapeDtypeStruct((n_slots, d), F32),
        scratch_shapes=[pltpu.SemaphoreType.DMA(())],
        input_output_aliases={2: 0},
        compiler_params=_cparams(("arbitrary",)),
        name="moe_dispatch",
    )(dest.reshape(n // tm, 1, TOP_K * tm), h, jnp.zeros((n_slots, d), F32))


def _moe_expert_kernel(be_ref, x_ref, w1_ref, w3_ref, w2_ref, y_ref):
    del be_ref
    xb = x_ref[...].astype(BF16)
    a = _dot(xb, w1_ref[...].astype(BF16))
    hmid = a * _sigmoid(a) * _dot(xb, w3_ref[...].astype(BF16))
    y_ref[...] = _dot(hmid.astype(BF16), w2_ref[...].astype(BF16))


def _moe_experts(xs, blk_expert, w1, w3, w2):
    d = xs.shape[1]
    n_blocks = blk_expert.shape[0]
    de = w1.shape[-1]
    return pl.pallas_call(
        _moe_expert_kernel,
        grid_spec=pltpu.PrefetchScalarGridSpec(
            num_scalar_prefetch=1, grid=(n_blocks,),
            in_specs=[pl.BlockSpec((MOE_BLOCK, d), lambda i, be: (i, 0)),
                      pl.BlockSpec((None, None, d, de), lambda i, be: (0, be[i], 0, 0)),
                      pl.BlockSpec((None, None, d, de), lambda i, be: (0, be[i], 0, 0)),
                      pl.BlockSpec((None, None, de, d), lambda i, be: (0, be[i], 0, 0))],
            out_specs=pl.BlockSpec((MOE_BLOCK, d), lambda i, be: (i, 0))),
        out_shape=jax.ShapeDtypeStruct((n_blocks * MOE_BLOCK, d), F32),
        compiler_params=_cparams(("parallel",)),
        name="moe_experts",
    )(blk_expert, xs, w1, w3, w2)


def _moe_combine_kernel(dest_ref, nxt_ref, y_hbm, h_ref, rt_ref, g2_ref, b2_ref, o_ref, ybuf, sems, *, tm):
    i = pl.program_id(0)
    slot = i % 2

    def copies(ref, s, t):
        return [_row_copy(y_hbm, ybuf.at[s, k], sems.at[s], ref[0, 0, TOP_K * t + k], t) for k in range(TOP_K)]

    def start_all(ref, s):
        def body(t, c):
            for cp in copies(ref, s, t):
                cp.start()
            return c
        lax.fori_loop(0, tm, body, 0, unroll=DMA_UNROLL)

    @pl.when(i == 0)
    def _():
        start_all(dest_ref, 0)

    @pl.when(i + 1 < pl.num_programs(0))
    def _():
        start_all(nxt_ref, 1 - slot)

    def wait(t, c):
        for cp in copies(dest_ref, slot, t):
            cp.wait()
        return c

    lax.fori_loop(0, tm, wait, 0, unroll=DMA_UNROLL)
    y0 = ybuf[slot, 0]
    y1 = ybuf[slot, 1]
    rt = rt_ref[...]
    lane = lax.broadcasted_iota(I32, rt.shape, 1)
    g1 = jnp.sum(jnp.where(lane == 2, rt, 0.0), axis=1, keepdims=True)
    g2 = jnp.sum(jnp.where(lane == 3, rt, 0.0), axis=1, keepdims=True)
    moe = g1 * y0 + g2 * y1
    o_ref[...] = _layer_norm(DEEPNORM_ALPHA * h_ref[...] + moe, g2_ref[...], b2_ref[...])


def _moe_combine(y, dest, h, rt, ln_g, ln_b):
    n, d = h.shape
    tm = min(128, n)
    nt = n // tm
    row = lambda i: (i, 0)
    full = lambda i: (0, 0)
    dest3 = dest.reshape(nt, 1, TOP_K * tm)
    return pl.pallas_call(
        functools.partial(_moe_combine_kernel, tm=tm),
        grid=(nt,),
        in_specs=[pl.BlockSpec((1, 1, TOP_K * tm), lambda i: (i, 0, 0), memory_space=pltpu.SMEM),
                  pl.BlockSpec((1, 1, TOP_K * tm), lambda i: (jnp.minimum(i + 1, nt - 1), 0, 0),
                               memory_space=pltpu.SMEM),
                  pl.BlockSpec(memory_space=pl.ANY),
                  pl.BlockSpec((tm, d), row), pl.BlockSpec((tm, LANE), row),
                  pl.BlockSpec((1, d), full), pl.BlockSpec((1, d), full)],
        out_specs=pl.BlockSpec((tm, d), row),
        out_shape=jax.ShapeDtypeStruct((n, d), F32),
        scratch_shapes=[pltpu.VMEM((2, TOP_K, tm, d), F32), pltpu.SemaphoreType.DMA((2,))],
        compiler_params=_cparams(("arbitrary",)),
        name="moe_combine",
    )(dest3, dest3, y, h, rt, ln_g, ln_b)


def _moe(h, rt, pw, w1, w3, w2):
    n = h.shape[0]
    a = n * TOP_K
    rank, cnt = _moe_rank(rt)
    counts = cnt[0, :N_EXPERTS].astype(I32)
    padded = (counts + MOE_BLOCK - 1) // MOE_BLOCK * MOE_BLOCK
    pad_end = jnp.cumsum(padded)
    pad_start = pad_end - padded
    expert = rt[:, :TOP_K].astype(I32)
    dest = (pad_start[expert] + rank[:, :TOP_K].astype(I32)).reshape(-1)
    n_blocks = -(-a // MOE_BLOCK) + N_EXPERTS
    blk_start = jnp.arange(n_blocks) * MOE_BLOCK
    blk_expert = jnp.minimum(jnp.sum(pad_end[None, :] <= blk_start[:, None], axis=1), N_EXPERTS - 1).astype(I32)
    xs = _moe_dispatch(h, dest, n_blocks * MOE_BLOCK)
    y = _moe_experts(xs, blk_expert, w1, w3, w2)
    return _moe_combine(y, dest, h, rt, pw['ln2_g'], pw['ln2_b'])


def _qlat_kernel(q_ref, w_ref, o_ref):
    o_ref[...] = _dot(q_ref[...], w_ref[...]).astype(o_ref.dtype)


def _q_latent(qm, wabs):
    b = qm.shape[0]
    return pl.pallas_call(
        _qlat_kernel,
        grid=(MLA_HEADS,),
        in_specs=[pl.BlockSpec((b, LANE), lambda h: (0, h)),
                  pl.BlockSpec((None, LANE, MLA_KV_RANK), lambda h: (h, 0, 0))],
        out_specs=pl.BlockSpec((b, MLA_KV_RANK), lambda h: (0, h)),
        out_shape=jax.ShapeDtypeStruct((b, MLA_HEADS * MLA_KV_RANK), BF16),
        compiler_params=_cparams(("parallel",)),
        name="q_latent",
    )(qm, wabs)


def _mla_decode_kernel(*refs, pp):
    pt_ref = refs[0]
    del pt_ref
    ql_ref, qlt_ref, qr_ref, cn_ref, kn_ref = refs[1:6]
    c_refs = refs[6:6 + pp]
    k_refs = refs[6 + pp:6 + 2 * pp]
    o_ref, m_scr, l_scr, acc_scr = refs[6 + 2 * pp:]
    ci = pl.program_id(1)
    qlt = qlt_ref[...]
    qr = qr_ref[...]

    @pl.when(ci == 0)
    def _():
        cn = cn_ref[...].astype(BF16).astype(F32)
        kn = kn_ref[...].astype(BF16).astype(F32)
        m_scr[...] = (jnp.sum(ql_ref[...].astype(F32) * cn, axis=1, keepdims=True)
                      + jnp.sum(qr.astype(F32) * kn, axis=1, keepdims=True))
        l_scr[...] = jnp.ones_like(l_scr)
        acc_scr[...] = jnp.broadcast_to(cn, acc_scr.shape)

    cs = [c[...].astype(BF16) for c in c_refs]
    s = jnp.concatenate([_dot(c, qlt).T[:MLA_HEADS] + _dot(qr, k[...].astype(BF16))
                         for c, k in zip(cs, k_refs)], axis=1)
    m_prev = m_scr[...]
    m_new = jnp.maximum(m_prev, jnp.max(s, axis=1, keepdims=True))
    alpha = jnp.exp2(m_prev - m_new)
    p = jnp.exp2(s - m_new)
    l_scr[...] = alpha * l_scr[...] + jnp.sum(p, axis=1, keepdims=True)
    pb = p.astype(BF16)
    acc = alpha * acc_scr[...]
    for j, c in enumerate(cs):
        acc = acc + _dot(pb[:, j * PAGE:(j + 1) * PAGE], c)
    acc_scr[...] = acc
    m_scr[...] = m_new

    @pl.when(ci == pl.num_programs(1) - 1)
    def _():
        o_ref[...] = (acc_scr[...] / l_scr[...]).astype(o_ref.dtype)


def _mla_decode(qlat, qr, ckv_new, kr_new, cache_ckv, cache_kr_t, page_table):
    b, n_pages = page_table.shape
    pp = min(16, n_pages)
    nc = n_pages // pp
    r = MLA_KV_RANK
    ql3 = qlat.reshape(b, MLA_HEADS, r)
    qlt = jnp.pad(jnp.transpose(ql3, (0, 2, 1)), ((0, 0), (0, 0), (0, LANE - MLA_HEADS)))

    def page_spec(rows, cols, j):
        return pl.BlockSpec((None, rows, cols), lambda bi, ci, pt: (pt[bi, ci * pp + j], 0, 0))

    in_specs = [pl.BlockSpec((None, MLA_HEADS, r), lambda bi, ci, pt: (bi, 0, 0)),
                pl.BlockSpec((None, r, LANE), lambda bi, ci, pt: (bi, 0, 0)),
                pl.BlockSpec((None, MLA_HEADS, MLA_ROPE), lambda bi, ci, pt: (bi, 0, 0)),
                pl.BlockSpec((None, 1, r), lambda bi, ci, pt: (bi, 0, 0)),
                pl.BlockSpec((None, 1, MLA_ROPE), lambda bi, ci, pt: (bi, 0, 0))]
    in_specs += [page_spec(PAGE, r, j) for j in range(pp)] + [page_spec(MLA_ROPE, PAGE, j) for j in range(pp)]
    return pl.pallas_call(
        functools.partial(_mla_decode_kernel, pp=pp),
        grid_spec=pltpu.PrefetchScalarGridSpec(
            num_scalar_prefetch=1, grid=(b, nc), in_specs=in_specs,
            out_specs=pl.BlockSpec((None, MLA_HEADS, r), lambda bi, ci, pt: (bi, 0, 0)),
            scratch_shapes=[pltpu.VMEM((MLA_HEADS, 1), F32), pltpu.VMEM((MLA_HEADS, 1), F32),
                            pltpu.VMEM((MLA_HEADS, r), F32)]),
        out_shape=jax.ShapeDtypeStruct((b, MLA_HEADS, r), BF16),
        compiler_params=_cparams(("parallel", "arbitrary")),
        name="mla_decode",
    )(page_table, ql3, qlt, qr, ckv_new.reshape(b, 1, r), kr_new.reshape(b, 1, MLA_ROPE),
      *([cache_ckv] * pp), *([cache_kr_t] * pp))


def _oproj_kernel(o_ref, w_ref, out_ref):
    out_ref[...] = (_dot(o_ref[:, :MLA_KV_RANK], w_ref[:, :LANE])
                    + _dot(o_ref[:, MLA_KV_RANK:], w_ref[:, LANE:])).astype(out_ref.dtype)


def _mla_out_up(o_lat, wuv):
    b = o_lat.shape[0]
    return pl.pallas_call(
        _oproj_kernel,
        grid=(MLA_HEADS // 2,),
        in_specs=[pl.BlockSpec((b, 2 * MLA_KV_RANK), lambda j: (0, j)),
                  pl.BlockSpec((MLA_KV_RANK, 2 * LANE), lambda j: (0, j))],
        out_specs=pl.BlockSpec((b, LANE), lambda j: (0, j)),
        out_shape=jax.ShapeDtypeStruct((b, 4 * LANE), BF16),
        compiler_params=_cparams(("parallel",)),
        name="mla_out_up",
    )(o_lat, wuv)


def _attend8(q8, variants, bias, ok):
    hrow = lax.broadcasted_iota(I32, (MLA_HEADS, 1), 0)
    vsel = (hrow >> GQA_SHIFT) * 2 + (hrow & 1)
    s = jnp.zeros(bias.shape, F32)
    for c, kv in enumerate(variants):
        s = s + jnp.where(vsel == c, _dot_nt(q8, kv), 0.0)
    p = _masked_softmax(s + bias, ok)
    pb = p.astype(BF16)
    o = jnp.zeros((MLA_HEADS, LANE), F32)
    for c, kv in enumerate(variants):
        o = o + jnp.where(vsel == c, _dot(pb, kv), 0.0)
    return o, p


def _slope_col():
    h = lax.broadcasted_iota(I32, (NSA_HEADS, 1), 0)
    return jnp.exp2(-(h + 1).astype(F32) * (8.0 / NSA_HEADS)) * LOG2E


def _attend_t(q8, kts, vts, bias, ok, new_row):
    hrow = lax.broadcasted_iota(I32, (NSA_HEADS, 1), 0)
    first = (hrow >> GQA_SHIFT) == 0
    s = jnp.where(first, _dot(q8, kts[0]), _dot(q8, kts[1])) + bias
    s = jnp.where(ok, s, NEG)
    new8 = jnp.broadcast_to(new_row, (NSA_HEADS, new_row.shape[1])).astype(BF16).astype(F32)
    kk, vv = new8[:, :LANE], new8[:, LANE:]
    k_new = jnp.where(first, kk, pltpu.roll(kk, HALF, 1))[:, :NSA_HD]
    v_new = jnp.where(first, vv, pltpu.roll(vv, HALF, 1))[:, :NSA_HD]
    s_new = jnp.sum(q8.astype(F32) * k_new, axis=1, keepdims=True)
    m = jnp.maximum(jnp.max(s, axis=1, keepdims=True), s_new)
    p = jnp.exp2(s - m)
    p_new = jnp.exp2(s_new - m)
    l = jnp.sum(p, axis=1, keepdims=True) + p_new
    pb = p.astype(BF16)
    o = jnp.where(first, _dot_nt(pb, vts[0]), _dot_nt(pb, vts[1])) + p_new * v_new
    return o / l


def _sample_cmp_win_kernel(qx_ref, q_ref, kc_ref, wmap_ref, st_ref, kvw_ref, newt_ref,
                           ocmp_ref, imp_ref, owin_ref, wout_ref, *, past_len):
    q8 = qx_ref[...]
    slope = _slope_col()
    nk = kc_ref.shape[0]
    kend = lax.broadcasted_iota(I32, (NSA_HEADS, nk), 1) * CMP_STRIDE + (CMP_BLOCK - 1)
    ok = kend <= past_len
    bias = -slope * (past_len - kend).astype(F32)
    kc = kc_ref[...]
    o, p = _attend8(q8, [kc[:, i * LANE:(i + 1) * LANE] for i in range(4)], bias, ok)
    ocmp_ref[...] = o.astype(ocmp_ref.dtype)
    hrow = lax.broadcasted_iota(I32, (NSA_HEADS, 1), 0)
    for kh in range(NSA_KV_HEADS):
        psum = jnp.sum(jnp.where((hrow >> GQA_SHIFT) == kh, p, 0.0), axis=0, keepdims=True)
        imp_ref[kh:kh + 1, :] = _split_dot(psum, wmap_ref[...])
    st = st_ref[...]
    wbuf = st.shape[1]
    stb = st.astype(BF16)
    j = lax.broadcasted_iota(I32, (NSA_HEADS, wbuf), 1)
    dist = wbuf - j
    okw = (dist <= WINDOW) & (past_len - dist >= 0)
    owin_ref[...] = _attend_t(q_ref[...], [stb[:NSA_HD], stb[NSA_HD:2 * NSA_HD]],
                              [stb[2 * NSA_HD:3 * NSA_HD], stb[3 * NSA_HD:]],
                              -slope * dist.astype(F32), okw, kvw_ref[...])
    newt = newt_ref[...]
    pick = lax.broadcasted_iota(I32, newt.shape, 1) == pl.program_id(0)
    new_col = jnp.sum(jnp.where(pick, newt, 0.0), axis=1, keepdims=True)
    nblk = wbuf // LANE
    rolled = [pltpu.roll(st[:, k * LANE:(k + 1) * LANE], LANE - 1, 1) for k in range(nblk)]
    keep = lax.broadcasted_iota(I32, (st.shape[0], LANE), 1) < LANE - 1
    for k in range(nblk):
        nxt = rolled[k + 1] if k + 1 < nblk else jnp.broadcast_to(new_col, rolled[k].shape)
        wout_ref[:, k * LANE:(k + 1) * LANE] = jnp.where(keep, rolled[k], nxt)


def _sample_cmp_win(qx, q8, kc_var, wmap, state_t, kvw, past_len):
    b = qx.shape[0]
    nk = kc_var.shape[0] // b
    wbuf = state_t.shape[2]
    wcols = wmap.shape[1]
    per_b3 = lambda bi: (bi, 0, 0)
    return pl.pallas_call(
        functools.partial(_sample_cmp_win_kernel, past_len=past_len),
        grid=(b,),
        in_specs=[pl.BlockSpec((None, NSA_HEADS, LANE), per_b3),
                  pl.BlockSpec((None, NSA_HEADS, NSA_HD), per_b3),
                  pl.BlockSpec((nk, 4 * LANE), lambda bi: (bi, 0)),
                  pl.BlockSpec(wmap.shape, lambda bi: (0, 0)),
                  pl.BlockSpec((None, NSA_KV_COLS, wbuf), per_b3),
                  pl.BlockSpec((None, 1, NSA_KV_COLS), per_b3),
                  pl.BlockSpec((NSA_KV_COLS, b), lambda bi: (0, 0))],
        out_specs=[pl.BlockSpec((None, NSA_HEADS, LANE), per_b3),
                   pl.BlockSpec((None, NSA_KV_HEADS, wcols), per_b3),
                   pl.BlockSpec((None, NSA_HEADS, NSA_HD), per_b3),
                   pl.BlockSpec((None, NSA_KV_COLS, wbuf), per_b3)],
        out_shape=[jax.ShapeDtypeStruct((b, NSA_HEADS, LANE), F32),
                   jax.ShapeDtypeStruct((b, NSA_KV_HEADS, wcols), F32),
                   jax.ShapeDtypeStruct((b, NSA_HEADS, NSA_HD), F32),
                   jax.ShapeDtypeStruct((b, NSA_KV_COLS, wbuf), F32)],
        compiler_params=_cparams(("parallel",)),
        name="sample_cmp_win",
    )(qx.reshape(b, NSA_HEADS, LANE), q8, kc_var, wmap, state_t, kvw.reshape(b, 1, NSA_KV_COLS),
      jnp.transpose(kvw))


def _sample_select_kernel(imp_ref, idx_ref, *, cur, n_slc):
    imp = imp_ref[...]
    rank, valid = _select_rank(imp, cur, n_slc)
    r, w = imp.shape
    blk = lax.broadcasted_iota(I32, (r, w), 1).astype(F32)
    lane = lax.broadcasted_iota(I32, (r, LANE), 1)
    out = jnp.zeros((r, LANE), F32)
    for t in range(SLC_TOP_N):
        pick = jnp.sum(jnp.where((rank == t) & valid, blk, 0.0), axis=1, keepdims=True)
        out = jnp.where(lane == t, pick, out)
    idx_ref[...] = out.astype(I32)


def _sample_select(imp, cur, n_slc):
    r = imp.shape[0]
    return pl.pallas_call(
        functools.partial(_sample_select_kernel, cur=cur, n_slc=n_slc),
        grid=(1,),
        in_specs=[pl.BlockSpec(imp.shape, lambda i: (0, 0))],
        out_specs=pl.BlockSpec((r, LANE), lambda i: (0, 0)),
        out_shape=jax.ShapeDtypeStruct((r, LANE), I32),
        compiler_params=_cparams(("arbitrary",)),
        name="sample_select",
    )(imp)


def _sample_slc_kernel(*refs, n_sel, past_len):
    blk_ref, page_ref = refs[0], refs[1]
    del page_ref
    q_ref, new_ref = refs[2], refs[3]
    c_refs = refs[4:4 + NSA_KV_HEADS * n_sel]
    o_ref = refs[4 + NSA_KV_HEADS * n_sel]
    bi = pl.program_id(0)
    hrow = lax.broadcasted_iota(I32, (NSA_HEADS, 1), 0)
    first = (hrow >> GQA_SHIFT) == 0
    nk = n_sel * PAGE
    col = lax.broadcasted_iota(I32, (NSA_HEADS, nk), 1)
    tile = col >> PAGE_SHIFT
    row = col & (PAGE - 1)
    kts, vts, blks = [], [], []
    for kh in range(NSA_KV_HEADS):
        mine = c_refs[kh * n_sel:(kh + 1) * n_sel]
        kts.append(jnp.concatenate([c[kh * NSA_HD:(kh + 1) * NSA_HD, :] for c in mine], axis=1).astype(BF16))
        vts.append(jnp.concatenate([c[(2 + kh) * NSA_HD:(3 + kh) * NSA_HD, :] for c in mine],
                                   axis=1).astype(BF16))
        blkv = jnp.zeros(col.shape, I32)
        for t in range(n_sel):
            blkv = jnp.where(tile == t, blk_ref[bi, kh * n_sel + t], blkv)
        blks.append(blkv)
    blkv = jnp.where(first, blks[0], blks[1])
    kpos = (blkv >> 1) * PAGE + row
    ok = ((row >> SLC_SHIFT) == (blkv & 1)) & (kpos < past_len)
    dist = past_len - kpos
    o_ref[...] = _attend_t(q_ref[...], kts, vts, -_slope_col() * dist.astype(F32), ok, new_ref[...])


def _sample_slc(q8, kvs_new, cache_slc_t, blk_idx, page_idx, past_len):
    b = q8.shape[0]
    n_sel = blk_idx.shape[1] // NSA_KV_HEADS
    per_b3 = lambda bi, blk, pg: (bi, 0, 0)
    in_specs = [pl.BlockSpec((None, NSA_HEADS, NSA_HD), per_b3),
                pl.BlockSpec((None, 1, NSA_KV_COLS), per_b3)]
    in_specs += [pl.BlockSpec((None, NSA_KV_COLS, PAGE),
                              functools.partial(lambda bi, blk, pg, j: (pg[bi, j], 0, 0), j=j))
                 for j in range(NSA_KV_HEADS * n_sel)]
    return pl.pallas_call(
        functools.partial(_sample_slc_kernel, n_sel=n_sel, past_len=past_len),
        grid_spec=pltpu.PrefetchScalarGridSpec(
            num_scalar_prefetch=2, grid=(b,), in_specs=in_specs,
            out_specs=pl.BlockSpec((None, NSA_HEADS, NSA_HD), per_b3)),
        out_shape=jax.ShapeDtypeStruct((b, NSA_HEADS, NSA_HD), F32),
        compiler_params=_cparams(("parallel",)),
        name="sample_slc",
    )(blk_idx, page_idx, q8, kvs_new.reshape(b, 1, NSA_KV_COLS), *([cache_slc_t] * (NSA_KV_HEADS * n_sel)))


def _heads_to_swapped(o8):
    b = o8.shape[0]
    return o8.reshape(b, NSA_HEADS // 2, 2, NSA_HD)[:, :, ::-1].reshape(b, NSA_HEADS * NSA_HD).astype(BF16)


def _pair_swap(o8):
    b = o8.shape[0]
    o = o8.reshape(b, NSA_HEADS // 2, 2, LANE)
    lo = jnp.arange(LANE) < HALF
    return jnp.where(lo, o[:, :, 1], o[:, :, 0]).reshape(b, 4 * LANE).astype(BF16)


def _rope_table(pos):
    half = MLA_ROPE // 2
    freqs = ROPE_THETA ** (-jnp.arange(half, dtype=F32) / half)
    ang = pos.astype(F32)[:, None] * freqs
    cos, sin = jnp.cos(ang), jnp.sin(ang)
    n = pos.shape[0]
    one, zero = jnp.ones((n, MLA_NOPE), F32), jnp.zeros((n, MLA_NOPE), F32)
    pad = jnp.zeros((n, LANE - MLA_NOPE - MLA_ROPE), F32)
    return jnp.concatenate([one, cos, cos, pad, zero, -sin, sin, pad], axis=1)


def _cmp_to_slc(n_cmp, n_slc, rows, cols):
    cs = np.arange(n_cmp)[:, None] * CMP_STRIDE
    ss = np.arange(n_slc)[None, :] * SLC_BLOCK
    inter = np.clip(np.minimum(cs + CMP_BLOCK, ss + SLC_BLOCK) - np.maximum(cs, ss), 0, None)
    w = np.zeros((rows, cols), np.float32)
    w[:n_cmp, :n_slc] = inter.astype(np.float32) / CMP_STRIDE
    return jnp.asarray(w, BF16)


def _prep_weights(w_in, mla_g_q, mla_g_kv, mla_w_uq, mla_w_uk, mla_w_uv, mla_w_o, nsa_pe_k, nsa_w1_k,
                  nsa_w2_k, nsa_pe_v, nsa_w1_v, nsa_w2_v, nsa_w_o, w_out, ln1_g, ln1_b, router_g_w,
                  router_g_b, router_e_w, router_e_b, ln2_g, ln2_b):
    d = w_in.shape[0]
    splits = (MLA_Q_RANK, MLA_KV_RANK, MLA_ROPE, NSA_HEADS * NSA_HD, NSA_KV_COLS, NSA_KV_COLS, NSA_KV_COLS,
              3 * NSA_HEADS, d, d)
    offs = np.cumsum(splits)[:-1].tolist()
    cq, ckv, kr, q, kvc, kvs, kvw, gn, ga, gb = jnp.split(w_in, offs, axis=1)
    r2 = MLA_ROPE // 2
    zc = lambda n: jnp.zeros((d, n), F32)
    tail = LANE - MLA_NOPE - MLA_ROPE
    kr_pad = jnp.concatenate([zc(MLA_NOPE), kr, zc(tail)], axis=1)
    kr_sw = jnp.concatenate([zc(MLA_NOPE), kr[:, r2:], kr[:, :r2], zc(tail)], axis=1)
    gn_pad = jnp.concatenate([gn, zc(LANE - gn.shape[1])], axis=1)
    w_all = jnp.concatenate([cq, ckv, q, kvc, kvs, kvw, kr_pad, kr_sw, gn_pad, ga, gb], axis=1).astype(BF16)

    def pad_heads(w, lo_cols):
        r, hh, c = w.shape
        out = jnp.zeros((r, hh, LANE), F32).at[:, :, lo_cols:lo_cols + c].set(w)
        return out

    nope, rope = mla_w_uq[:, :, :MLA_NOPE], mla_w_uq[:, :, MLA_NOPE:]
    wuq = jnp.concatenate([nope, rope, jnp.zeros(nope.shape[:2] + (tail,), F32)], axis=2)
    wuqs = jnp.concatenate([jnp.zeros_like(nope), rope[:, :, r2:], rope[:, :, :r2],
                            jnp.zeros(nope.shape[:2] + (tail,), F32)], axis=2)
    wuk = pad_heads(mla_w_uk, 0)
    odd = (jnp.arange(MLA_HEADS) % 2 == 1)[None, :, None]
    wuv = jnp.where(odd, pad_heads(mla_w_uv, HALF), pad_heads(mla_w_uv, 0))
    flat = lambda w: w.reshape(w.shape[0], -1).astype(BF16)
    wabs = jnp.zeros((MLA_HEADS, LANE, MLA_KV_RANK), F32).at[:, :MLA_NOPE, :].set(
        jnp.transpose(mla_w_uk, (1, 2, 0))).astype(BF16)
    psel = jnp.zeros((LANE, MLA_ROPE), F32).at[MLA_NOPE:MLA_NOPE + MLA_ROPE].set(jnp.eye(MLA_ROPE)).astype(BF16)

    def w1_cols(w1):
        return jnp.transpose(w1.reshape(2, CMP_STRIDE, NSA_HD, CMP_HIDDEN), (1, 2, 0, 3))

    wt = jnp.stack([w1_cols(nsa_w1_k), w1_cols(nsa_w1_v)])
    eye2 = jnp.eye(2, dtype=F32)
    w_big = jnp.einsum('tpdfh,tu,kv->ptkdfuvh', wt, eye2, eye2).reshape(
        CMP_STRIDE, NSA_KV_COLS, 2 * 2 * NSA_KV_HEADS * CMP_HIDDEN).astype(BF16)
    zpe = jnp.zeros((CMP_BLOCK * NSA_HD, CMP_HIDDEN), F32)
    w1_pe = jnp.concatenate([
        jnp.concatenate([nsa_w1_k, nsa_w1_k, zpe, zpe], axis=1),
        jnp.concatenate([zpe, zpe, nsa_w1_v, nsa_w1_v], axis=1)], axis=0)
    pe_flat = jnp.concatenate([nsa_pe_k.reshape(1, -1), nsa_pe_v.reshape(1, -1)], axis=1)
    w1_hi = w1_pe.astype(BF16)
    w1_lo = (w1_pe - w1_hi.astype(F32)).astype(BF16)
    z64 = jnp.zeros((CMP_HIDDEN, NSA_HD), F32)
    rows = [[nsa_w2_k, z64, z64, z64], [z64, nsa_w2_k, z64, z64], [z64, z64, nsa_w2_v, z64],
            [z64, z64, z64, nsa_w2_v]]
    w2_cmp = jnp.concatenate([jnp.concatenate(r, axis=1) for r in rows], axis=0).astype(BF16)

    perm = np.arange(NSA_HEADS * NSA_HD).reshape(NSA_HEADS // 2, 2, NSA_HD)[:, ::-1].reshape(-1)
    w_on = nsa_w_o[perm].astype(BF16)
    e3 = np.zeros((LANE, 3, NSA_HEADS * NSA_HD), np.float32)
    for h in range(NSA_HEADS):
        pos = (h // 2) * LANE + (0 if h % 2 else HALF)
        for j in range(3):
            e3[3 * h + j, j, pos:pos + NSA_HD] = 1.0
    e3 = jnp.asarray(e3.reshape(LANE, -1), BF16)
    w_r = jnp.concatenate([router_g_w, zc(HALF - N_GROUPS), router_e_w], axis=1)
    w_r_hi = w_r.astype(BF16)
    w_r_lo = (w_r - w_r_hi.astype(F32)).astype(BF16)
    b_r = jnp.concatenate([router_g_b, jnp.zeros((HALF - N_GROUPS,), F32), router_e_b])[None, :]
    return dict(
        w_all=w_all, wuq=flat(wuq), wuqs=flat(wuqs), wuk=flat(wuk), wuv=flat(wuv),
        gq=mla_g_q[None, :], gkv=mla_g_kv[None, :], wabs=wabs, psel=psel, w_big=w_big,
        pe_flat=pe_flat, w1_pe=jnp.stack([w1_hi, w1_lo]), w2_cmp=w2_cmp,
        w_o=mla_w_o.astype(BF16), w_on=w_on, w_out=w_out.astype(BF16), e3=e3,
        ln1_g=ln1_g[None, :], ln1_b=ln1_b[None, :], w_router=jnp.stack([w_r_hi, w_r_lo]), b_router=b_r,
        ln2_g=ln2_g[None, :], ln2_b=ln2_b[None, :])


def _prompt_layer(x, pw, slopes, moe_w):
    b, s, d = x.shape
    n = b * s
    tm = min(256, s)
    z = _in_proj(x.reshape(n, d), _rope_table(jnp.arange(s)), pw, tm)
    o_mla = _flash(z['qm'], z['km'], z['vm'], slopes, b, s, mode='causal', kv_shared=False, alibi=False,
                   swap=False)
    nb = s // CMP_STRIDE
    fs = _cmp_fs_dense(z['kvc'], pw['w_big'])
    kc_var = _cmp_finish(fs, nb, pw)
    n_slc = -(-s // SLC_BLOCK)
    wmap = _cmp_to_slc(nb - 1, n_slc, nb, LANE)
    o_cmp, sel = _cmp_select(z['qn'], kc_var, slopes, wmap, b, s, n_slc)
    emat_t = jnp.asarray((np.arange(s)[:, None] // SLC_BLOCK) == np.arange(LANE)[None, :], BF16)
    o_slc = _flash(z['qn'], z['kvsv'], z['kvsvt'], slopes, b, s, mode='select', kv_shared=True, alibi=True,
                   swap=True, selt=sel, emat_t=emat_t)
    o_win = _flash(z['qn'], z['kvwv'], z['kvwvt'], slopes, b, s, mode='window', kv_shared=True, alibi=True,
                   swap=True)
    h, rt = _merge(x.reshape(n, d), o_mla, o_cmp, o_slc, o_win, z, pw, tm)
    y = _moe(h, rt, pw, *moe_w)
    return y.reshape(b, s, d), z


def _sample_layer(x, pw, slopes, moe_w, cache_ckv, cache_kr_t, cache_cmp_t, cache_slc_t, state_t, page_table):
    b, t, d = x.shape
    n_pages = page_table.shape[1]
    past_len = n_pages * PAGE
    z = _in_proj(x.reshape(b, d), _rope_table(jnp.full((b,), past_len)), pw, b)
    qlat = _q_latent(z['qm'], pw['wabs'])
    qr = z['qm'].reshape(b, MLA_HEADS, LANE)[:, :, MLA_NOPE:MLA_NOPE + MLA_ROPE]
    o_lat = _mla_decode(qlat, qr, z['ckv'], z['kr'], cache_ckv, cache_kr_t, page_table)
    o_mla = _mla_out_up(o_lat.reshape(b, MLA_HEADS * MLA_KV_RANK), pw['wuv'])
    nb = past_len // CMP_STRIDE
    fs = _cmp_fs_paged(cache_cmp_t, page_table, pw['w_big'])
    kc_var = _cmp_finish(fs, nb, pw)
    n_slc = -(-(past_len + t) // SLC_BLOCK)
    assert n_slc >= SLC_TOP_N and past_len % SLC_BLOCK == 0
    wcols = -(-n_slc // LANE) * LANE
    wmap = _cmp_to_slc(nb - 1, n_slc, nb, wcols)
    q8 = z['qnat'].reshape(b, NSA_HEADS, NSA_HD)
    o_cmp8, imp, o_win8, win_out = _sample_cmp_win(z['qn'], q8, kc_var, wmap, state_t, z['kvw'], past_len)
    cur = past_len // SLC_BLOCK
    idx = _sample_select(imp.reshape(b * NSA_KV_HEADS, wcols), cur, n_slc)[:, :SLC_TOP_N]
    blk = idx.reshape(b, NSA_KV_HEADS * SLC_TOP_N)
    safe = jnp.minimum(blk, past_len // SLC_BLOCK - 1)
    pages = jnp.take_along_axis(page_table, safe // (PAGE // SLC_BLOCK), axis=1).astype(I32)
    o_slc8 = _sample_slc(q8, z['kvs'], cache_slc_t, blk, pages, past_len)
    h, rt = _merge(x.reshape(b, d), o_mla, _pair_swap(o_cmp8), _heads_to_swapped(o_slc8),
                   _heads_to_swapped(o_win8), z, pw, b)
    y = _moe(h, rt, pw, *moe_w)
    return y.reshape(b, t, d), z, win_out


def kernel(x_prompt, x_sample, cache_mla_ckv, cache_mla_krope, cache_nsa_cmp_kv, cache_nsa_slc_kv,
           state_nsa_win_kv, page_table, w_in, mla_g_q, mla_g_kv, mla_w_uq, mla_w_uk, mla_w_uv, mla_w_o,
           nsa_pe_k, nsa_w1_k, nsa_w2_k, nsa_pe_v, nsa_w1_v, nsa_w2_v, nsa_w_o, w_out, ln1_g, ln1_b,
           router_g_w, router_g_b, router_e_w, router_e_b, moe_w1, moe_w3, moe_w2, ln2_g, ln2_b):
    assert w_in.shape[0] == DEPTH and x_sample.shape[1] == 1
    b, s, d = x_prompt.shape
    bs = x_sample.shape[0]
    kv_shape = (2, NSA_KV_HEADS, NSA_HD)
    pw = _prep_weights(w_in[0], mla_g_q[0], mla_g_kv[0], mla_w_uq[0], mla_w_uk[0], mla_w_uv[0], mla_w_o[0],
                       nsa_pe_k[0], nsa_w1_k[0], nsa_w2_k[0], nsa_pe_v[0], nsa_w1_v[0], nsa_w2_v[0],
                       nsa_w_o[0], w_out[0], ln1_g[0], ln1_b[0], router_g_w[0], router_g_b[0],
                       router_e_w[0], router_e_b[0], ln2_g[0], ln2_b[0])
    slopes = 2.0 ** (-8.0 * jnp.arange(1, NSA_HEADS + 1, dtype=F32) / NSA_HEADS)
    moe_w = (moe_w1, moe_w3, moe_w2)
    y_p, zp = _prompt_layer(x_prompt, pw, slopes, moe_w)
    def rows_last(c):
        return jnp.transpose(c, (0, 2, 3, 4, 1)).reshape(c.shape[0], NSA_KV_COLS, c.shape[1])

    y_s, zs, win_t = _sample_layer(
        x_sample, pw, slopes, moe_w, cache_mla_ckv[0], jnp.transpose(cache_mla_krope[0], (0, 2, 1)),
        rows_last(cache_nsa_cmp_kv[0]), rows_last(cache_nsa_slc_kv[0]), rows_last(state_nsa_win_kv[0]),
        page_table)
    win_s = jnp.transpose(win_t.reshape((bs,) + kv_shape + (-1,)), (0, 4, 1, 2, 3))
    wp = min(WINDOW, s)
    return (y_p, y_s,
            zp['ckv'].reshape(1, b, s, MLA_KV_RANK), zs['ckv'].reshape(1, bs, 1, MLA_KV_RANK),
            zp['kr'].reshape(1, b, s, MLA_ROPE), zs['kr'].reshape(1, bs, 1, MLA_ROPE),
            zp['kvc'].reshape((1, b, s) + kv_shape), zs['kvc'].reshape((1, bs, 1) + kv_shape),
            zp['kvs'].reshape((1, b, s) + kv_shape), zs['kvs'].reshape((1, bs, 1) + kv_shape),
            zp['kvw'].reshape(b, s, NSA_KV_COLS)[:, s - wp:].reshape((1, b, wp) + kv_shape),
            win_s.reshape((1, bs, -1) + kv_shape))
```

```python
import functools

import numpy as np
import jax
import jax.numpy as jnp
from jax import lax
from jax.experimental import pallas as pl
from jax.experimental.pallas import tpu as pltpu

F32 = jnp.float32
BF16 = jnp.bfloat16
I32 = jnp.int32

PAGE = 128
MLA_HEADS = 8
MLA_NOPE = 64
MLA_ROPE = 32
MLA_V = 64
MLA_Q_RANK = 256
MLA_KV_RANK = 256
ROPE_THETA = 10000.0
NSA_HEADS = 8
NSA_KV_HEADS = 2
NSA_GQA = NSA_HEADS // NSA_KV_HEADS
NSA_HD = 64
CMP_BLOCK = 32
CMP_STRIDE = 16
CMP_HIDDEN = 64
SLC_BLOCK = 64
SLC_TOP_N = 16
N_LOCAL_SLC = 2
WINDOW = 512
FORCE_SCORE = 1.0e4
N_GROUPS = 8
EXPERTS_PER_GROUP = 8
N_EXPERTS = N_GROUPS * EXPERTS_PER_GROUP
TOP_K = 2
D_EXPERT = 256
MOE_BLOCK = 128
LN_EPS = 1e-5
RMS_EPS = 1e-6
DEPTH = 1
DEEPNORM_ALPHA = (2.0 * DEPTH) ** 0.25
NSA_KV_COLS = 2 * NSA_KV_HEADS * NSA_HD

PAGE_SHIFT = PAGE.bit_length() - 1
SLC_SHIFT = SLC_BLOCK.bit_length() - 1
GQA_SHIFT = NSA_GQA.bit_length() - 1
LANE = 128
HALF = LANE // 2
VMEM_LIMIT = 56 * 1024 * 1024
NEG = -1e30
DMA_UNROLL = 4
LOG2E = 1.4426950408889634

_C_CQ, _C_CKV, _C_Q, _C_KVC, _C_KVS, _C_KVW, _C_KRP, _C_KRS, _C_GN, _C_GA = (
    0, 256, 512, 1024, 1280, 1536, 1792, 1920, 2048, 2176)


def _cparams(sem):
    return pltpu.CompilerParams(dimension_semantics=sem, vmem_limit_bytes=VMEM_LIMIT)


def _dot(a, b):
    return jnp.dot(a, b, preferred_element_type=F32)


def _dot_nt(a, b):
    return lax.dot_general(a, b, (((1,), (1,)), ((), ())), preferred_element_type=F32)


def _split_dot(a, w_hi, w_lo=None):
    a_hi = a.astype(BF16)
    a_lo = (a - a_hi.astype(F32)).astype(BF16)
    r = _dot(a_hi, w_hi) + _dot(a_lo, w_hi)
    if w_lo is not None:
        r = r + _dot(a_hi, w_lo)
    return r


def _sigmoid(x):
    return 1.0 / (1.0 + jnp.exp(-x))


def _kv_variants(x0, x1):
    lo = lax.broadcasted_iota(I32, x0.shape, 1) < HALF
    r0 = pltpu.roll(x0, HALF, 1)
    r1 = pltpu.roll(x1, HALF, 1)
    return (jnp.where(lo, x0, r1), jnp.where(lo, x1, r0), jnp.where(lo, r0, x1), jnp.where(lo, r1, x0))


def _inproj_kernel(x_ref, cs_ref, w_ref, wuq_ref, wuqs_ref, wuk_ref, wuv_ref, gq_ref, gkv_ref,
                   qm_ref, km_ref, vm_ref, ckv_ref, kr_ref, qn_ref, kvc_ref, kvs_ref, kvw_ref,
                   kvsv_ref, kvwv_ref, gn_ref, ga_ref, gb_ref, qnat_ref, kvsvt_ref, kvwvt_ref,
                   kvct_ref, kvst_ref, kvwt_ref, *, mla_scale):
    xb = x_ref[...].astype(BF16)

    def proj(a, b):
        return _dot(xb, w_ref[:, a:b])

    cos = cs_ref[:, :LANE]
    sin = cs_ref[:, LANE:]

    def rms(z, g):
        return z * lax.rsqrt(jnp.mean(z * z, axis=-1, keepdims=True) + RMS_EPS) * g

    cq = rms(proj(_C_CQ, _C_CKV), gq_ref[...]).astype(BF16)
    q = _dot(cq, wuq_ref[...])
    qs = _dot(cq, wuqs_ref[...])
    ckv = rms(proj(_C_CKV, _C_Q), gkv_ref[...])
    ckv_ref[...] = ckv
    ckvb = ckv.astype(BF16)
    kr = proj(_C_KRP, _C_KRS) * cos + proj(_C_KRS, _C_GN) * sin
    kr_ref[...] = kr[:, MLA_NOPE:MLA_NOPE + MLA_ROPE]
    kn = _dot(ckvb, wuk_ref[...])
    v = _dot(ckvb, wuv_ref[...])
    for h in range(MLA_HEADS):
        sl = slice(h * LANE, (h + 1) * LANE)
        qm_ref[:, sl] = ((q[:, sl] * cos + qs[:, sl] * sin) * mla_scale).astype(BF16)
        km_ref[:, sl] = (kn[:, sl] + kr).astype(BF16)
        vm_ref[sl, :] = v[:, sl].T.astype(BF16)
    qn = proj(_C_Q, _C_KVC) * (NSA_HD ** -0.5 * LOG2E)
    qnat_ref[...] = qn.astype(BF16)
    lo = lax.broadcasted_iota(I32, (qn.shape[0], LANE), 1) < HALF
    for j in range(NSA_HEADS // 2):
        blk = qn[:, j * LANE:(j + 1) * LANE]
        qn_ref[:, (2 * j) * LANE:(2 * j + 1) * LANE] = jnp.where(lo, blk, 0.0).astype(BF16)
        qn_ref[:, (2 * j + 1) * LANE:(2 * j + 2) * LANE] = jnp.where(lo, 0.0, blk).astype(BF16)
    zc = proj(_C_KVC, _C_KVS)
    kvc_ref[...] = zc
    kvct_ref[...] = zc.T
    for src, dst, dst_t, var, vart in ((_C_KVS, kvs_ref, kvst_ref, kvsv_ref, kvsvt_ref),
                                       (_C_KVW, kvw_ref, kvwt_ref, kvwv_ref, kvwvt_ref)):
        z = proj(src, src + NSA_KV_COLS)
        dst[...] = z
        dst_t[...] = z.T
        for i, t in enumerate(_kv_variants(z[:, :LANE], z[:, LANE:])):
            var[:, i * LANE:(i + 1) * LANE] = t.astype(BF16)
            vart[i * LANE:(i + 1) * LANE, :] = t.T.astype(BF16)
    gn_ref[...] = _sigmoid(proj(_C_GN, _C_GA))
    d = ga_ref.shape[1]
    ga_ref[...] = _sigmoid(proj(_C_GA, _C_GA + d)).astype(BF16)
    gb_ref[...] = _sigmoid(proj(_C_GA + d, _C_GA + 2 * d)).astype(BF16)


def _in_proj(x, cs, pw, tm, seq):
    n, d = x.shape
    per_seq = seq // tm
    n_cs = cs.shape[0] // tm
    row = lambda i: (i, 0)
    full = lambda i: (0, 0)
    wcols = pw['w_all'].shape[1]
    outs = [
        ('qm', 8 * LANE, BF16), ('km', 8 * LANE, BF16), ('vm', 8 * LANE, BF16),
        ('ckv', MLA_KV_RANK, F32), ('kr', MLA_ROPE, F32), ('qn', 8 * LANE, BF16),
        ('kvc', NSA_KV_COLS, F32), ('kvs', NSA_KV_COLS, F32), ('kvw', NSA_KV_COLS, F32),
        ('kvsv', 4 * LANE, BF16), ('kvwv', 4 * LANE, BF16), ('gn', LANE, F32),
        ('ga', d, BF16), ('gb', d, BF16), ('qnat', NSA_HEADS * NSA_HD, BF16),
        ('kvsvt', 4 * LANE, BF16), ('kvwvt', 4 * LANE, BF16),
        ('kvct', NSA_KV_COLS, F32), ('kvst', NSA_KV_COLS, F32), ('kvwt', NSA_KV_COLS, F32)]
    transposed = ('vm', 'kvsvt', 'kvwvt')
    by_seq = ('kvct', 'kvst', 'kvwt')

    def out_spec(k, c):
        if k in transposed:
            return pl.BlockSpec((c, tm), lambda i: (0, i))
        if k in by_seq:
            return pl.BlockSpec((None, c, tm), lambda i: (i // per_seq, 0, i % per_seq))
        return pl.BlockSpec((tm, c), row)

    def out_shape(k, c, t):
        shape = (c, n) if k in transposed else (n // seq, c, seq) if k in by_seq else (n, c)
        return jax.ShapeDtypeStruct(shape, t)

    res = pl.pallas_call(
        functools.partial(_inproj_kernel, mla_scale=(MLA_NOPE + MLA_ROPE) ** -0.5 * LOG2E),
        grid=(n // tm,),
        in_specs=[pl.BlockSpec((tm, d), row),
                  pl.BlockSpec((tm, 2 * LANE), lambda i: (i % n_cs, 0)),
                  pl.BlockSpec((d, wcols), full),
                  pl.BlockSpec((MLA_Q_RANK, 8 * LANE), full),
                  pl.BlockSpec((MLA_Q_RANK, 8 * LANE), full),
                  pl.BlockSpec((MLA_KV_RANK, 8 * LANE), full),
                  pl.BlockSpec((MLA_KV_RANK, 8 * LANE), full),
                  pl.BlockSpec((1, MLA_Q_RANK), full),
                  pl.BlockSpec((1, MLA_KV_RANK), full)],
        out_specs=[out_spec(k, c) for k, c, _ in outs],
        out_shape=[out_shape(k, c, t) for k, c, t in outs],
        compiler_params=_cparams(("parallel",)),
        name="in_proj",
    )(x, cs, pw['w_all'], pw['wuq'], pw['wuqs'], pw['wuk'], pw['wuv'], pw['gq'], pw['gkv'])
    return {k: v for (k, _, _), v in zip(outs, res)}


def _flash_kernel(*refs, mode, t, alibi, swap):
    if mode == 'select':
        slopes_ref, q_ref, k_ref, vt_ref, selt_ref, et_ref, o_ref, m_scr, l_scr, acc_scr = refs
    else:
        slopes_ref, q_ref, k_ref, vt_ref, o_ref, m_scr, l_scr, acc_scr = refs
    hp = pl.program_id(1)
    qi = pl.program_id(2)
    q0 = qi * t
    krow = lax.broadcasted_iota(I32, (t, t), 0)
    dmat = lax.broadcasted_iota(I32, (t, t), 1) - krow
    krow_f = krow.astype(F32)
    qs = [q_ref[:, e * LANE:(e + 1) * LANE] for e in range(2)]
    for e in range(2):
        m_scr[e] = jnp.full((1, t), NEG, F32)
        l_scr[e] = jnp.zeros((1, t), F32)
        acc_scr[e] = jnp.zeros((LANE, t), F32)

    def step(kt, edge):
        k0 = pl.multiple_of(kt * t, t)
        ok = None
        if edge:
            dist = dmat + (q0 - k0)
            ok = dist >= 0
            if mode == 'window':
                ok = ok & (dist <= WINDOW)
        if mode == 'select':
            picked = _dot(et_ref[pl.ds(k0, t), :], selt_ref[...]) > 0.5
            ok = picked if ok is None else ok & picked
        for e in range(2):
            k = k_ref[pl.ds(k0, t), e * LANE:(e + 1) * LANE]
            vt = vt_ref[e * LANE:(e + 1) * LANE, pl.ds(k0, t)]
            s = _dot_nt(k, qs[e])
            shift = 0.0
            if alibi:
                slope = slopes_ref[2 * hp + e] * LOG2E
                s = s + slope * krow_f
                shift = slope * (k0 - q0).astype(F32)
            if ok is not None:
                s = jnp.where(ok, s, NEG)
            m_prev = m_scr[e]
            m_new = jnp.maximum(m_prev, jnp.max(s, axis=0, keepdims=True) + shift)
            alpha = jnp.exp2(m_prev - m_new)
            p = jnp.exp2(s - (m_new - shift))
            l_scr[e] = alpha * l_scr[e] + jnp.sum(p, axis=0, keepdims=True)
            acc_scr[e] = alpha * acc_scr[e] + _dot(vt, p.astype(BF16))
            m_scr[e] = m_new

    def loop(lo, hi, edge):
        def body(kt, c):
            step(kt, edge)
            return c
        lax.fori_loop(lo, hi, body, 0)

    if mode == 'window':
        loop(jnp.maximum(q0 - WINDOW, 0) // t, qi + 1, True)
    else:
        loop(0, qi, False)
        step(qi, True)
    o0 = acc_scr[0] / l_scr[0]
    o1 = acc_scr[1] / l_scr[1]
    lo = lax.broadcasted_iota(I32, (LANE, t), 0) < HALF
    o_ref[...] = (jnp.where(lo, o1, o0) if swap else jnp.where(lo, o0, o1)).T.astype(o_ref.dtype)


def _flash(q, k, vt, slopes, b, s, *, mode, kv_shared, alibi, swap, selt=None, emat_t=None):
    tq = min(512, s)
    nq = s // tq
    k_idx = (lambda bi, hp, qi: (bi, hp // 2)) if kv_shared else (lambda bi, hp, qi: (bi, hp))
    vt_idx = (lambda bi, hp, qi: (hp // 2, bi)) if kv_shared else (lambda bi, hp, qi: (hp, bi))
    in_specs = [pl.BlockSpec(memory_space=pltpu.SMEM),
                pl.BlockSpec((tq, 2 * LANE), lambda bi, hp, qi: (bi * nq + qi, hp)),
                pl.BlockSpec((s, 2 * LANE), k_idx),
                pl.BlockSpec((2 * LANE, s), vt_idx)]
    args = [slopes, q, k, vt]
    if mode == 'select':
        in_specs += [pl.BlockSpec((None, None, LANE, tq), lambda bi, hp, qi: (bi, hp // 2, 0, qi)),
                     pl.BlockSpec((s, LANE), lambda bi, hp, qi: (0, 0))]
        args += [selt, emat_t]
    return pl.pallas_call(
        functools.partial(_flash_kernel, mode=mode, t=tq, alibi=alibi, swap=swap),
        grid=(b, 4, nq),
        in_specs=in_specs,
        out_specs=pl.BlockSpec((tq, LANE), lambda bi, hp, qi: (bi * nq + qi, hp)),
        out_shape=jax.ShapeDtypeStruct((b * s, 4 * LANE), BF16),
        scratch_shapes=[pltpu.VMEM((2, 1, tq), F32), pltpu.VMEM((2, 1, tq), F32),
                        pltpu.VMEM((2, LANE, tq), F32)],
        compiler_params=_cparams(("parallel", "parallel", "arbitrary")),
        name="flash_" + mode,
    )(*args)


def _cmp_first_linear(lo, hi, w_ref, o_ref):
    r = o_ref.shape[0]
    acc = jnp.zeros(o_ref.shape, F32)
    for p in range(CMP_STRIDE):
        rows = pl.ds(p, r, stride=CMP_STRIDE)
        x = jnp.concatenate([lo[rows, :], hi[rows, :]], axis=1)
        acc = acc + _dot(x.astype(BF16), w_ref[p])
    o_ref[...] = acc


def _cmp_fs_kernel(lo, hi, w_ref, o_ref):
    _cmp_first_linear(lo, hi, w_ref, o_ref)


def _cmp_fs_dense(x, w_big):
    n, c = x.shape
    tr = min(256, n // CMP_STRIDE)
    return pl.pallas_call(
        _cmp_fs_kernel,
        grid=(n // CMP_STRIDE // tr,),
        in_specs=[pl.BlockSpec((tr * CMP_STRIDE, LANE), lambda i: (i, 0)),
                  pl.BlockSpec((tr * CMP_STRIDE, LANE), lambda i: (i, 1)),
                  pl.BlockSpec(w_big.shape, lambda i: (0, 0, 0))],
        out_specs=pl.BlockSpec((tr, 4 * LANE), lambda i: (i, 0)),
        out_shape=jax.ShapeDtypeStruct((n // CMP_STRIDE, 4 * LANE), F32),
        compiler_params=_cparams(("parallel",)),
        name="cmp_fs",
    )(x, x, w_big)


def _prefetched_pages(pt_ref, srcs, bufs, sems, n_pages):
    bi = pl.program_id(0)
    slot = bi % 2

    def copies(seq, s, j):
        pg = pt_ref[seq, j]
        return [pltpu.make_async_copy(src.at[pg], buf.at[s, j], sems.at[s]) for src, buf in zip(srcs, bufs)]

    def start_all(seq, s):
        def body(j, c):
            for cp in copies(seq, s, j):
                cp.start()
            return c
        lax.fori_loop(0, n_pages, body, 0, unroll=DMA_UNROLL)

    @pl.when(bi == 0)
    def _():
        start_all(0, 0)

    @pl.when(bi + 1 < pl.num_programs(0))
    def _():
        start_all(bi + 1, 1 - slot)

    def wait(j, c):
        for cp in copies(bi, slot, j):
            cp.wait()
        return c

    lax.fori_loop(0, n_pages, wait, 0, unroll=DMA_UNROLL)
    return slot


def _cmp_fs_paged_kernel(pt_ref, cache_hbm, w_ref, o_ref, pbuf, sems, lo, hi, *, n_pages):
    slot = _prefetched_pages(pt_ref, (cache_hbm,), (pbuf,), sems, n_pages)

    def to_rows(j, c):
        r0 = pl.multiple_of(j * PAGE, PAGE)
        lo[pl.ds(r0, PAGE), :] = pbuf[slot, j, :LANE, :].T
        hi[pl.ds(r0, PAGE), :] = pbuf[slot, j, LANE:, :].T
        return c

    lax.fori_loop(0, n_pages, to_rows, 0, unroll=8)
    _cmp_first_linear(lo, hi, w_ref, o_ref)


def _cmp_fs_paged(cache_t, page_table, w_big):
    b, n_pages = page_table.shape
    rows = n_pages * (PAGE // CMP_STRIDE)
    return pl.pallas_call(
        functools.partial(_cmp_fs_paged_kernel, n_pages=n_pages),
        grid_spec=pltpu.PrefetchScalarGridSpec(
            num_scalar_prefetch=1, grid=(b,),
            in_specs=[pl.BlockSpec(memory_space=pl.ANY),
                      pl.BlockSpec(w_big.shape, lambda bi, pt: (0, 0, 0))],
            out_specs=pl.BlockSpec((rows, 4 * LANE), lambda bi, pt: (bi, 0)),
            scratch_shapes=[pltpu.VMEM((2, n_pages, NSA_KV_COLS, PAGE), F32), pltpu.SemaphoreType.DMA((2,)),
                            pltpu.VMEM((n_pages * PAGE, LANE), F32), pltpu.VMEM((n_pages * PAGE, LANE), F32)]),
        out_shape=jax.ShapeDtypeStruct((b * rows, 4 * LANE), F32),
        compiler_params=_cparams(("arbitrary",)),
        name="cmp_fs_paged",
    )(page_table, cache_t, w_big)


def _gelu_tanh(x):
    return 0.5 * x * (1.0 + jnp.tanh(0.7978845608028654 * (x + 0.044715 * x * x * x)))


def _cmp_pe_kernel(pe_ref, w1_ref, o_ref):
    o_ref[...] = _split_dot(pe_ref[...], w1_ref[0], w1_ref[1])


def _cmp_pe(pw):
    return pl.pallas_call(
        _cmp_pe_kernel,
        grid=(1,),
        in_specs=[pl.BlockSpec(pw['pe_flat'].shape, lambda i: (0, 0)),
                  pl.BlockSpec(pw['w1_pe'].shape, lambda i: (0, 0, 0))],
        out_specs=pl.BlockSpec((1, 2 * LANE), lambda i: (0, 0)),
        out_shape=jax.ShapeDtypeStruct((1, 2 * LANE), F32),
        compiler_params=_cparams(("arbitrary",)),
        name="cmp_pe",
    )(pw['pe_flat'], pw['w1_pe'])


def _cmp_finish_kernel(fs_ref, pe_ref, w2_ref, o_ref, ot_ref):
    fs = fs_ref[...]
    n = fs.shape[0]
    h = _gelu_tanh(fs[:, :2 * LANE] + pltpu.roll(fs[:, 2 * LANE:], n - 1, 0) + pe_ref[...])
    kc = _dot(h.astype(BF16), w2_ref[...])
    for i, t in enumerate(_kv_variants(kc[:, :LANE], kc[:, LANE:])):
        o_ref[:, i * LANE:(i + 1) * LANE] = t.astype(BF16)
        ot_ref[i * LANE:(i + 1) * LANE, :] = t.T.astype(BF16)


def _cmp_finish(fs, nb, pe, pw):
    r = fs.shape[0]
    full2 = lambda i: (0, 0)
    return pl.pallas_call(
        _cmp_finish_kernel,
        grid=(r // nb,),
        in_specs=[pl.BlockSpec((nb, 4 * LANE), lambda i: (i, 0)),
                  pl.BlockSpec(pe.shape, full2),
                  pl.BlockSpec(pw['w2_cmp'].shape, full2)],
        out_specs=[pl.BlockSpec((nb, 4 * LANE), lambda i: (i, 0)),
                   pl.BlockSpec((None, 4 * LANE, nb), lambda i: (i, 0, 0))],
        out_shape=[jax.ShapeDtypeStruct((r, 4 * LANE), BF16),
                   jax.ShapeDtypeStruct((r // nb, 4 * LANE, nb), BF16)],
        compiler_params=_cparams(("parallel",)),
        name="cmp_finish",
    )(fs, pe, pw['w2_cmp'])


def _masked_softmax(s, ok, axis=-1):
    s = jnp.where(ok, s, -jnp.inf)
    m = jnp.max(s, axis=axis, keepdims=True)
    m = jnp.where(m > -jnp.inf, m, 0.0)
    e = jnp.exp2(s - m)
    d = jnp.sum(e, axis=axis, keepdims=True)
    return e / jnp.where(d > 0.0, d, 1.0)


def _select_rank(imp, cur, n_slc):
    r, w = imp.shape
    blk = lax.broadcasted_iota(I32, (r, w), 1)
    valid = (blk <= cur) & (blk < n_slc)
    forced = (blk == 0) | (valid & (blk > cur - N_LOCAL_SLC))
    score = jnp.where(forced, FORCE_SCORE, jnp.where(valid, imp, -1.0))
    score = jnp.where(blk < n_slc, score, -2.0)

    def body(i, rank):
        col = jnp.sum(jnp.where(blk == i, score, 0.0), axis=1, keepdims=True)
        beats = (col > score) | ((col == score) & (i < blk))
        return rank + jnp.where(beats, 1.0, 0.0)

    rank = lax.fori_loop(0, n_slc, body, jnp.zeros((r, w), F32), unroll=True if n_slc <= 32 else 4)
    return rank, valid


def _cmp_select_kernel(slopes_ref, q_ref, kc_ref, kct_ref, wmapt_ref, o_ref, sel_ref, *, tq, n_slc):
    kvh = pl.program_id(1)
    q0 = pl.program_id(2) * tq
    nk = kc_ref.shape[0]
    qpos = q0 + lax.broadcasted_iota(I32, (nk, tq), 1)
    kend = lax.broadcasted_iota(I32, (nk, tq), 0) * CMP_STRIDE + (CMP_BLOCK - 1)
    ok = kend <= qpos
    dist = (qpos - kend).astype(F32)
    psum = jnp.zeros((nk, tq), F32)
    outs = []
    for g in range(NSA_GQA):
        sl = slice((g % 2) * LANE, (g % 2 + 1) * LANE)
        s = _dot_nt(kc_ref[:, sl], q_ref[:, g * LANE:(g + 1) * LANE]) - (slopes_ref[kvh * NSA_GQA + g] * LOG2E) * dist
        p = _masked_softmax(s, ok, axis=0)
        psum = psum + p
        outs.append(_dot(kct_ref[sl, :], p.astype(BF16)))
    lo = lax.broadcasted_iota(I32, (LANE, tq), 0) < HALF
    o_ref[:, :LANE] = jnp.where(lo, outs[1], outs[0]).T.astype(o_ref.dtype)
    o_ref[:, LANE:] = jnp.where(lo, outs[3], outs[2]).T.astype(o_ref.dtype)
    ps_hi = psum.astype(BF16)
    ps_lo = (psum - ps_hi.astype(F32)).astype(BF16)
    imp = _dot(wmapt_ref[...], ps_hi) + _dot(wmapt_ref[...], ps_lo)
    rows = -(-n_slc // 8) * 8
    imp = imp[:rows]
    blk = lax.broadcasted_iota(I32, (rows, tq), 0)
    cur = (q0 + lax.broadcasted_iota(I32, (1, tq), 1)) >> SLC_SHIFT
    valid = (blk <= cur) & (blk < n_slc)
    forced = (blk == 0) | (valid & (blk > cur - N_LOCAL_SLC))
    score = jnp.where(forced, FORCE_SCORE, jnp.where(valid, imp, -1.0))
    rank = jnp.zeros((rows, tq), F32)
    for i in range(n_slc):
        other = score[i:i + 1, :]
        rank = rank + jnp.where((other > score) | ((other == score) & (i < blk)), 1.0, 0.0)
    sel = jnp.where((rank < SLC_TOP_N) & valid, 1.0, 0.0)
    sel_ref[...] = jnp.concatenate([sel, jnp.zeros((LANE - rows, tq), F32)], axis=0).astype(sel_ref.dtype)


def _cmp_select(qn, kc_var, kc_var_t, slopes, wmap_t, b, s, n_slc):
    tq = min(256, s)
    nq = s // tq
    nk = kc_var.shape[0] // b
    assert n_slc <= LANE
    return pl.pallas_call(
        functools.partial(_cmp_select_kernel, tq=tq, n_slc=n_slc),
        grid=(b, NSA_KV_HEADS, nq),
        in_specs=[pl.BlockSpec(memory_space=pltpu.SMEM),
                  pl.BlockSpec((tq, 4 * LANE), lambda bi, kh, qi: (bi * nq + qi, kh)),
                  pl.BlockSpec((nk, 2 * LANE), lambda bi, kh, qi: (bi, kh)),
                  pl.BlockSpec((None, 2 * LANE, nk), lambda bi, kh, qi: (bi, kh, 0)),
                  pl.BlockSpec(wmap_t.shape, lambda bi, kh, qi: (0, 0))],
        out_specs=[pl.BlockSpec((tq, 2 * LANE), lambda bi, kh, qi: (bi * nq + qi, kh)),
                   pl.BlockSpec((None, None, LANE, tq), lambda bi, kh, qi: (bi, kh, 0, qi))],
        out_shape=[jax.ShapeDtypeStruct((b * s, 4 * LANE), BF16),
                   jax.ShapeDtypeStruct((b, NSA_KV_HEADS, LANE, s), BF16)],
        compiler_params=_cparams(("parallel", "parallel", "parallel")),
        name="cmp_select",
    )(slopes, qn, kc_var, kc_var_t, wmap_t)


def _layer_norm(x, g, b):
    mu = jnp.mean(x, axis=-1, keepdims=True)
    xc = x - mu
    var = jnp.mean(xc * xc, axis=-1, keepdims=True)
    return xc * lax.rsqrt(var + LN_EPS) * g + b


def _merge_kernel(x_ref, omla_ref, ocmp_ref, oslc_ref, owin_ref, gn_ref, ga_ref, gb_ref,
                  wo_ref, wn_ref, wout_ref, e3_ref, g1_ref, b1_ref, wr_ref, br_ref,
                  h_ref, rt_ref):
    gn = gn_ref[...]
    w = 4 * LANE
    ge = _split_dot(gn, e3_ref[...])
    o_nsa = (ge[:, :w] * ocmp_ref[...].astype(F32) + ge[:, w:2 * w] * oslc_ref[...].astype(F32)
             + ge[:, 2 * w:] * owin_ref[...].astype(F32))
    u = (ga_ref[...].astype(F32) * _dot(omla_ref[...], wo_ref[...])
         + gb_ref[...].astype(F32) * _dot(o_nsa.astype(BF16), wn_ref[...]))
    h = _layer_norm(DEEPNORM_ALPHA * x_ref[...] + _dot(u.astype(BF16), wout_ref[...]),
                    g1_ref[...], b1_ref[...])
    h_ref[...] = h
    lg = _split_dot(h, wr_ref[0], wr_ref[1]) + br_ref[...]
    tm = lg.shape[0]
    lane = lax.broadcasted_iota(I32, (tm, LANE), 1)
    is_g = lane < N_GROUPS
    gm = jnp.max(jnp.where(is_g, lg, -jnp.inf), axis=1, keepdims=True)
    gidx = jnp.min(jnp.where(is_g & (lg == gm), lane, LANE), axis=1, keepdims=True)
    p_group = 1.0 / jnp.sum(jnp.where(is_g, jnp.exp(lg - gm), 0.0), axis=1, keepdims=True)
    in_g = (lane >= HALF) & (((lane - HALF) >> 3) == gidx)
    m1 = jnp.max(jnp.where(in_g, lg, -jnp.inf), axis=1, keepdims=True)
    i1 = jnp.min(jnp.where(in_g & (lg == m1), lane, 2 * LANE), axis=1, keepdims=True)
    rest = in_g & (lane != i1)
    m2 = jnp.max(jnp.where(rest, lg, -jnp.inf), axis=1, keepdims=True)
    i2 = jnp.min(jnp.where(rest & (lg == m2), lane, 2 * LANE), axis=1, keepdims=True)
    t = jnp.exp(m2 - m1)
    g1 = p_group / (1.0 + t)
    g2 = p_group * t / (1.0 + t)
    rt = jnp.where(lane == 0, (i1 - HALF).astype(F32),
                   jnp.where(lane == 1, (i2 - HALF).astype(F32),
                             jnp.where(lane == 2, g1, jnp.where(lane == 3, g2, 0.0))))
    rt_ref[...] = rt


def _merge(x, omla, ocmp, oslc, owin, z, pw, tm):
    n, d = x.shape
    row = lambda i: (i, 0)
    full = lambda i: (0, 0)
    w = 4 * LANE
    return pl.pallas_call(
        _merge_kernel,
        grid=(n // tm,),
        in_specs=[pl.BlockSpec((tm, d), row), pl.BlockSpec((tm, w), row), pl.BlockSpec((tm, w), row),
                  pl.BlockSpec((tm, w), row), pl.BlockSpec((tm, w), row), pl.BlockSpec((tm, LANE), row),
                  pl.BlockSpec((tm, d), row), pl.BlockSpec((tm, d), row),
                  pl.BlockSpec((w, d), full), pl.BlockSpec((w, d), full), pl.BlockSpec((d, d), full),
                  pl.BlockSpec((LANE, 3 * w), full), pl.BlockSpec((1, d), full), pl.BlockSpec((1, d), full),
                  pl.BlockSpec((2, d, LANE), lambda i: (0, 0, 0)), pl.BlockSpec((1, LANE), full)],
        out_specs=[pl.BlockSpec((tm, d), row), pl.BlockSpec((tm, LANE), row)],
        out_shape=[jax.ShapeDtypeStruct((n, d), F32), jax.ShapeDtypeStruct((n, LANE), F32)],
        compiler_params=_cparams(("parallel",)),
        name="merge_router",
    )(x, omla, ocmp, oslc, owin, z['gn'], z['ga'], z['gb'], pw['w_o'], pw['w_on'], pw['w_out'],
      pw['e3'], pw['ln1_g'], pw['ln1_b'], pw['w_router'], pw['b_router'])


def _row_copy(src, dst, sem, src_row, dst_row):
    return pltpu.make_async_copy(src.at[pl.ds(src_row, 1)], dst.at[pl.ds(dst_row, 1)], sem)


def _moe_rank_kernel(rt_ref, tri_ref, rank_ref, cnt_ref, run_scr):
    @pl.when(pl.program_id(0) == 0)
    def _():
        run_scr[...] = jnp.zeros_like(run_scr)

    rt = rt_ref[...]
    lane = lax.broadcasted_iota(I32, rt.shape, 1)
    lanef = lane.astype(F32)
    e0 = jnp.sum(jnp.where(lane == 0, rt, 0.0), axis=1, keepdims=True)
    e1 = jnp.sum(jnp.where(lane == 1, rt, 0.0), axis=1, keepdims=True)
    hit0 = lanef == e0
    hit1 = lanef == e1
    onehot = jnp.where(hit0 | hit1, 1.0, 0.0)
    before = _dot(tri_ref[...], onehot.astype(BF16)) + run_scr[...]
    r0 = jnp.sum(jnp.where(hit0, before, 0.0), axis=1, keepdims=True)
    r1 = jnp.sum(jnp.where(hit1, before, 0.0), axis=1, keepdims=True)
    rank_ref[...] = jnp.where(lane == 0, r0, jnp.where(lane == 1, r1, 0.0))
    run = run_scr[...] + jnp.sum(onehot, axis=0, keepdims=True)
    run_scr[...] = run
    cnt_ref[...] = run


def _moe_rank(rt):
    n = rt.shape[0]
    tm = min(512, n)
    tri = jnp.asarray(np.tril(np.ones((tm, tm), np.float32), -1), BF16)
    return pl.pallas_call(
        _moe_rank_kernel,
        grid=(n // tm,),
        in_specs=[pl.BlockSpec((tm, LANE), lambda i: (i, 0)), pl.BlockSpec((tm, tm), lambda i: (0, 0))],
        out_specs=[pl.BlockSpec((tm, LANE), lambda i: (i, 0)), pl.BlockSpec((1, LANE), lambda i: (0, 0))],
        out_shape=[jax.ShapeDtypeStruct((n, LANE), F32), jax.ShapeDtypeStruct((1, LANE), F32)],
        scratch_shapes=[pltpu.VMEM((1, LANE), F32)],
        compiler_params=_cparams(("arbitrary",)),
        name="moe_rank",
    )(rt, tri)


def _moe_dispatch_kernel(dest_ref, h_ref, xs_in, xs_out, sem, *, tm):
    del xs_in

    def copies(t):
        return [_row_copy(h_ref, xs_out, sem, t, dest_ref[0, 0, TOP_K * t + k]) for k in range(TOP_K)]

    def start(t, c):
        for cp in copies(t):
            cp.start()
        return c

    def wait(t, c):
        for cp in copies(t):
            cp.wait()
        return c

    lax.fori_loop(0, tm, start, 0, unroll=DMA_UNROLL)
    lax.fori_loop(0, tm, wait, 0, unroll=DMA_UNROLL)


def _moe_dispatch(h, dest, n_slots):
    n, d = h.shape
    tm = min(256, n)
    return pl.pallas_call(
        functools.partial(_moe_dispatch_kernel, tm=tm),
        grid=(n // tm,),
        in_specs=[pl.BlockSpec((1, 1, TOP_K * tm), lambda i: (i, 0, 0), memory_space=pltpu.SMEM),
                  pl.BlockSpec((tm, d), lambda i: (i, 0)),
                  pl.BlockSpec(memory_space=pl.ANY)],
        out_specs=pl.BlockSpec(memory_space=pl.ANY),
        out_shape=jax.ShapeDtypeStruct((n_slots, d), F32),
        scratch_shapes=[pltpu.SemaphoreType.DMA(())],
        input_output_aliases={2: 0},
        compiler_params=_cparams(("arbitrary",)),
        name="moe_dispatch",
    )(dest.reshape(n // tm, 1, TOP_K * tm), h, jnp.zeros((n_slots, d), F32))


def _moe_expert_kernel(be_ref, x_ref, w1_ref, w3_ref, w2_ref, y_ref):
    del be_ref
    xb = x_ref[...].astype(BF16)
    a = _dot(xb, w1_ref[...].astype(BF16))
    hmid = a * _sigmoid(a) * _dot(xb, w3_ref[...].astype(BF16))
    y_ref[...] = _dot(hmid.astype(BF16), w2_ref[...].astype(BF16))


def _moe_experts(xs, blk_expert, w1, w3, w2):
    d = xs.shape[1]
    n_blocks = blk_expert.shape[0]
    de = w1.shape[-1]
    return pl.pallas_call(
        _moe_expert_kernel,
        grid_spec=pltpu.PrefetchScalarGridSpec(
            num_scalar_prefetch=1, grid=(n_blocks,),
            in_specs=[pl.BlockSpec((MOE_BLOCK, d), lambda i, be: (i, 0)),
                      pl.BlockSpec((None, None, d, de), lambda i, be: (0, be[i], 0, 0)),
                      pl.BlockSpec((None, None, d, de), lambda i, be: (0, be[i], 0, 0)),
                      pl.BlockSpec((None, None, de, d), lambda i, be: (0, be[i], 0, 0))],
            out_specs=pl.BlockSpec((MOE_BLOCK, d), lambda i, be: (i, 0))),
        out_shape=jax.ShapeDtypeStruct((n_blocks * MOE_BLOCK, d), F32),
        compiler_params=_cparams(("parallel",)),
        name="moe_experts",
    )(blk_expert, xs, w1, w3, w2)


def _moe_combine_kernel(dest_ref, nxt_ref, y_hbm, h_ref, rt_ref, g2_ref, b2_ref, o_ref, ybuf, sems, *, tm):
    i = pl.program_id(0)
    slot = i % 2

    def copies(ref, s, t):
        return [_row_copy(y_hbm, ybuf.at[s, k], sems.at[s], ref[0, 0, TOP_K * t + k], t) for k in range(TOP_K)]

    def start_all(ref, s):
        def body(t, c):
            for cp in copies(ref, s, t):
                cp.start()
            return c
        lax.fori_loop(0, tm, body, 0, unroll=DMA_UNROLL)

    @pl.when(i == 0)
    def _():
        start_all(dest_ref, 0)

    @pl.when(i + 1 < pl.num_programs(0))
    def _():
        start_all(nxt_ref, 1 - slot)

    def wait(t, c):
        for cp in copies(dest_ref, slot, t):
            cp.wait()
        return c

    lax.fori_loop(0, tm, wait, 0, unroll=DMA_UNROLL)
    y0 = ybuf[slot, 0]
    y1 = ybuf[slot, 1]
    rt = rt_ref[...]
    lane = lax.broadcasted_iota(I32, rt.shape, 1)
    g1 = jnp.sum(jnp.where(lane == 2, rt, 0.0), axis=1, keepdims=True)
    g2 = jnp.sum(jnp.where(lane == 3, rt, 0.0), axis=1, keepdims=True)
    moe = g1 * y0 + g2 * y1
    o_ref[...] = _layer_norm(DEEPNORM_ALPHA * h_ref[...] + moe, g2_ref[...], b2_ref[...])


def _moe_combine(y, dest, h, rt, ln_g, ln_b):
    n, d = h.shape
    tm = min(128, n)
    nt = n // tm
    row = lambda i: (i, 0)
    full = lambda i: (0, 0)
    dest3 = dest.reshape(nt, 1, TOP_K * tm)
    return pl.pallas_call(
        functools.partial(_moe_combine_kernel, tm=tm),
        grid=(nt,),
        in_specs=[pl.BlockSpec((1, 1, TOP_K * tm), lambda i: (i, 0, 0), memory_space=pltpu.SMEM),
                  pl.BlockSpec((1, 1, TOP_K * tm), lambda i: (jnp.minimum(i + 1, nt - 1), 0, 0),
                               memory_space=pltpu.SMEM),
                  pl.BlockSpec(memory_space=pl.ANY),
                  pl.BlockSpec((tm, d), row), pl.BlockSpec((tm, LANE), row),
                  pl.BlockSpec((1, d), full), pl.BlockSpec((1, d), full)],
        out_specs=pl.BlockSpec((tm, d), row),
        out_shape=jax.ShapeDtypeStruct((n, d), F32),
        scratch_shapes=[pltpu.VMEM((2, TOP_K, tm, d), F32), pltpu.SemaphoreType.DMA((2,))],
        compiler_params=_cparams(("arbitrary",)),
        name="moe_combine",
    )(dest3, dest3, y, h, rt, ln_g, ln_b)


def _moe(h, rt, pw, w1, w3, w2):
    n = h.shape[0]
    a = n * TOP_K
    rank, cnt = _moe_rank(rt)
    counts = cnt[0, :N_EXPERTS].astype(I32)
    padded = (counts + MOE_BLOCK - 1) // MOE_BLOCK * MOE_BLOCK
    pad_end = jnp.cumsum(padded)
    pad_start = pad_end - padded
    expert = rt[:, :TOP_K].astype(I32)
    dest = (pad_start[expert] + rank[:, :TOP_K].astype(I32)).reshape(-1)
    n_blocks = -(-a // MOE_BLOCK) + N_EXPERTS
    blk_start = jnp.arange(n_blocks) * MOE_BLOCK
    blk_expert = jnp.minimum(jnp.sum(pad_end[None, :] <= blk_start[:, None], axis=1), N_EXPERTS - 1).astype(I32)
    xs = _moe_dispatch(h, dest, n_blocks * MOE_BLOCK)
    y = _moe_experts(xs, blk_expert, w1, w3, w2)
    return _moe_combine(y, dest, h, rt, pw['ln2_g'], pw['ln2_b'])


def _qlat_kernel(q_ref, w_ref, o_ref):
    o_ref[...] = _dot(q_ref[...], w_ref[...]).astype(o_ref.dtype)


def _q_latent(qm, wabs):
    b = qm.shape[0]
    return pl.pallas_call(
        _qlat_kernel,
        grid=(MLA_HEADS,),
        in_specs=[pl.BlockSpec((b, LANE), lambda h: (0, h)),
                  pl.BlockSpec((None, LANE, MLA_KV_RANK), lambda h: (h, 0, 0))],
        out_specs=pl.BlockSpec((b, MLA_KV_RANK), lambda h: (0, h)),
        out_shape=jax.ShapeDtypeStruct((b, MLA_HEADS * MLA_KV_RANK), BF16),
        compiler_params=_cparams(("parallel",)),
        name="q_latent",
    )(qm, wabs)


def _mla_decode_kernel(pt_ref, ql_ref, qlt_ref, qr_ref, cn_ref, kn_ref, c_hbm, k_hbm, o_ref,
                       cbuf, kbuf, sems, m_scr, l_scr, acc_scr, *, n_pages, pp):
    slot = _prefetched_pages(pt_ref, (c_hbm, k_hbm), (cbuf, kbuf), sems, n_pages)
    qlt = qlt_ref[...]
    qr = qr_ref[...]
    cn = cn_ref[...].astype(BF16).astype(F32)
    kn = kn_ref[...].astype(BF16).astype(F32)
    m_scr[...] = (jnp.sum(ql_ref[...].astype(F32) * cn, axis=1, keepdims=True)
                  + jnp.sum(qr.astype(F32) * kn, axis=1, keepdims=True))
    l_scr[...] = jnp.ones_like(l_scr)
    acc_scr[...] = jnp.broadcast_to(cn, acc_scr.shape)

    def chunk(ci, carry):
        cs = [cbuf[slot, ci * pp + j].astype(BF16) for j in range(pp)]
        s = jnp.concatenate([_dot(c, qlt).T[:MLA_HEADS] + _dot(qr, kbuf[slot, ci * pp + j].astype(BF16))
                             for j, c in enumerate(cs)], axis=1)
        m_prev = m_scr[...]
        m_new = jnp.maximum(m_prev, jnp.max(s, axis=1, keepdims=True))
        alpha = jnp.exp2(m_prev - m_new)
        p = jnp.exp2(s - m_new)
        l_scr[...] = alpha * l_scr[...] + jnp.sum(p, axis=1, keepdims=True)
        pb = p.astype(BF16)
        acc = alpha * acc_scr[...]
        for j, c in enumerate(cs):
            acc = acc + _dot(pb[:, j * PAGE:(j + 1) * PAGE], c)
        acc_scr[...] = acc
        m_scr[...] = m_new
        return carry

    lax.fori_loop(0, n_pages // pp, chunk, 0)
    o_ref[...] = (acc_scr[...] / l_scr[...]).astype(o_ref.dtype)


def _mla_decode(qlat, qr, ckv_new, kr_new, cache_ckv, cache_kr_t, page_table):
    b, n_pages = page_table.shape
    pp = min(32, n_pages)
    r = MLA_KV_RANK
    ql3 = qlat.reshape(b, MLA_HEADS, r)
    qlt = jnp.pad(jnp.transpose(ql3, (0, 2, 1)), ((0, 0), (0, 0), (0, LANE - MLA_HEADS)))
    per_b = lambda bi, pt: (bi, 0, 0)
    return pl.pallas_call(
        functools.partial(_mla_decode_kernel, n_pages=n_pages, pp=pp),
        grid_spec=pltpu.PrefetchScalarGridSpec(
            num_scalar_prefetch=1, grid=(b,),
            in_specs=[pl.BlockSpec((None, MLA_HEADS, r), per_b),
                      pl.BlockSpec((None, r, LANE), per_b),
                      pl.BlockSpec((None, MLA_HEADS, MLA_ROPE), per_b),
                      pl.BlockSpec((None, 1, r), per_b),
                      pl.BlockSpec((None, 1, MLA_ROPE), per_b),
                      pl.BlockSpec(memory_space=pl.ANY),
                      pl.BlockSpec(memory_space=pl.ANY)],
            out_specs=pl.BlockSpec((None, MLA_HEADS, r), per_b),
            scratch_shapes=[pltpu.VMEM((2, n_pages, PAGE, r), F32), pltpu.VMEM((2, n_pages, MLA_ROPE, PAGE), F32),
                            pltpu.SemaphoreType.DMA((2,)),
                            pltpu.VMEM((MLA_HEADS, 1), F32), pltpu.VMEM((MLA_HEADS, 1), F32),
                            pltpu.VMEM((MLA_HEADS, r), F32)]),
        out_shape=jax.ShapeDtypeStruct((b, MLA_HEADS, r), BF16),
        compiler_params=_cparams(("arbitrary",)),
        name="mla_decode",
    )(page_table, ql3, qlt, qr, ckv_new.reshape(b, 1, r), kr_new.reshape(b, 1, MLA_ROPE),
      cache_ckv, cache_kr_t)


def _oproj_kernel(o_ref, w_ref, out_ref):
    out_ref[...] = (_dot(o_ref[:, :MLA_KV_RANK], w_ref[:, :LANE])
                    + _dot(o_ref[:, MLA_KV_RANK:], w_ref[:, LANE:])).astype(out_ref.dtype)


def _mla_out_up(o_lat, wuv):
    b = o_lat.shape[0]
    return pl.pallas_call(
        _oproj_kernel,
        grid=(MLA_HEADS // 2,),
        in_specs=[pl.BlockSpec((b, 2 * MLA_KV_RANK), lambda j: (0, j)),
                  pl.BlockSpec((MLA_KV_RANK, 2 * LANE), lambda j: (0, j))],
        out_specs=pl.BlockSpec((b, LANE), lambda j: (0, j)),
        out_shape=jax.ShapeDtypeStruct((b, 4 * LANE), BF16),
        compiler_params=_cparams(("parallel",)),
        name="mla_out_up",
    )(o_lat, wuv)


def _attend8(q8, variants, bias, ok):
    hrow = lax.broadcasted_iota(I32, (MLA_HEADS, 1), 0)
    vsel = (hrow >> GQA_SHIFT) * 2 + (hrow & 1)
    s = jnp.zeros(bias.shape, F32)
    for c, kv in enumerate(variants):
        s = s + jnp.where(vsel == c, _dot_nt(q8, kv), 0.0)
    p = _masked_softmax(s + bias, ok)
    pb = p.astype(BF16)
    o = jnp.zeros((MLA_HEADS, LANE), F32)
    for c, kv in enumerate(variants):
        o = o + jnp.where(vsel == c, _dot(pb, kv), 0.0)
    return o, p


def _slope_col():
    h = lax.broadcasted_iota(I32, (NSA_HEADS, 1), 0)
    return jnp.exp2(-(h + 1).astype(F32) * (8.0 / NSA_HEADS)) * LOG2E


def _attend_t(q8, kts, vts, bias, ok, new_row):
    hrow = lax.broadcasted_iota(I32, (NSA_HEADS, 1), 0)
    first = (hrow >> GQA_SHIFT) == 0
    s = jnp.where(first, _dot(q8, kts[0]), _dot(q8, kts[1])) + bias
    s = jnp.where(ok, s, NEG)
    new8 = jnp.broadcast_to(new_row, (NSA_HEADS, new_row.shape[1])).astype(BF16).astype(F32)
    kk, vv = new8[:, :LANE], new8[:, LANE:]
    k_new = jnp.where(first, kk, pltpu.roll(kk, HALF, 1))[:, :NSA_HD]
    v_new = jnp.where(first, vv, pltpu.roll(vv, HALF, 1))[:, :NSA_HD]
    s_new = jnp.sum(q8.astype(F32) * k_new, axis=1, keepdims=True)
    m = jnp.maximum(jnp.max(s, axis=1, keepdims=True), s_new)
    p = jnp.exp2(s - m)
    p_new = jnp.exp2(s_new - m)
    l = jnp.sum(p, axis=1, keepdims=True) + p_new
    pb = p.astype(BF16)
    o = jnp.where(first, _dot_nt(pb, vts[0]), _dot_nt(pb, vts[1])) + p_new * v_new
    return o / l


def _sample_cmp_win_kernel(qx_ref, q_ref, kc_ref, wmap_ref, st_ref, kvw_ref, newt_ref,
                           ocmp_ref, imp_ref, owin_ref, wout_ref, *, past_len):
    q8 = qx_ref[...]
    slope = _slope_col()
    nk = kc_ref.shape[0]
    kend = lax.broadcasted_iota(I32, (NSA_HEADS, nk), 1) * CMP_STRIDE + (CMP_BLOCK - 1)
    ok = kend <= past_len
    bias = -slope * (past_len - kend).astype(F32)
    kc = kc_ref[...]
    o, p = _attend8(q8, [kc[:, i * LANE:(i + 1) * LANE] for i in range(4)], bias, ok)
    ocmp_ref[...] = o.astype(ocmp_ref.dtype)
    hrow = lax.broadcasted_iota(I32, (NSA_HEADS, 1), 0)
    for kh in range(NSA_KV_HEADS):
        psum = jnp.sum(jnp.where((hrow >> GQA_SHIFT) == kh, p, 0.0), axis=0, keepdims=True)
        imp_ref[kh:kh + 1, :] = _split_dot(psum, wmap_ref[...])
    st = st_ref[...]
    wbuf = st.shape[1]
    stb = st.astype(BF16)
    j = lax.broadcasted_iota(I32, (NSA_HEADS, wbuf), 1)
    dist = wbuf - j
    okw = (dist <= WINDOW) & (past_len - dist >= 0)
    owin_ref[...] = _attend_t(q_ref[...], [stb[:NSA_HD], stb[NSA_HD:2 * NSA_HD]],
                              [stb[2 * NSA_HD:3 * NSA_HD], stb[3 * NSA_HD:]],
                              -slope * dist.astype(F32), okw, kvw_ref[...])
    newt = newt_ref[...]
    pick = lax.broadcasted_iota(I32, newt.shape, 1) == pl.program_id(0)
    new_col = jnp.sum(jnp.where(pick, newt, 0.0), axis=1, keepdims=True)
    nblk = wbuf // LANE
    rolled = [pltpu.roll(st[:, k * LANE:(k + 1) * LANE], LANE - 1, 1) for k in range(nblk)]
    keep = lax.broadcasted_iota(I32, (st.shape[0], LANE), 1) < LANE - 1
    for k in range(nblk):
        nxt = rolled[k + 1] if k + 1 < nblk else jnp.broadcast_to(new_col, rolled[k].shape)
        wout_ref[:, k * LANE:(k + 1) * LANE] = jnp.where(keep, rolled[k], nxt)


def _sample_cmp_win(qx, q8, kc_var, wmap, state_t, kvw, past_len):
    b = qx.shape[0]
    nk = kc_var.shape[0] // b
    wbuf = state_t.shape[2]
    wcols = wmap.shape[1]
    per_b3 = lambda bi: (bi, 0, 0)
    return pl.pallas_call(
        functools.partial(_sample_cmp_win_kernel, past_len=past_len),
        grid=(b,),
        in_specs=[pl.BlockSpec((None, NSA_HEADS, LANE), per_b3),
                  pl.BlockSpec((None, NSA_HEADS, NSA_HD), per_b3),
                  pl.BlockSpec((nk, 4 * LANE), lambda bi: (bi, 0)),
                  pl.BlockSpec(wmap.shape, lambda bi: (0, 0)),
                  pl.BlockSpec((None, NSA_KV_COLS, wbuf), per_b3),
                  pl.BlockSpec((None, 1, NSA_KV_COLS), per_b3),
                  pl.BlockSpec((NSA_KV_COLS, b), lambda bi: (0, 0))],
        out_specs=[pl.BlockSpec((None, NSA_HEADS, LANE), per_b3),
                   pl.BlockSpec((None, NSA_KV_HEADS, wcols), per_b3),
                   pl.BlockSpec((None, NSA_HEADS, NSA_HD), per_b3),
                   pl.BlockSpec((None, NSA_KV_COLS, wbuf), per_b3)],
        out_shape=[jax.ShapeDtypeStruct((b, NSA_HEADS, LANE), F32),
                   jax.ShapeDtypeStruct((b, NSA_KV_HEADS, wcols), F32),
                   jax.ShapeDtypeStruct((b, NSA_HEADS, NSA_HD), F32),
                   jax.ShapeDtypeStruct((b, NSA_KV_COLS, wbuf), F32)],
        compiler_params=_cparams(("parallel",)),
        name="sample_cmp_win",
    )(qx.reshape(b, NSA_HEADS, LANE), q8, kc_var, wmap, state_t, kvw.reshape(b, 1, NSA_KV_COLS),
      jnp.transpose(kvw))


def _sample_select_kernel(imp_ref, idx_ref, *, cur, n_slc):
    imp = imp_ref[...]
    rank, valid = _select_rank(imp, cur, n_slc)
    r, w = imp.shape
    blk = lax.broadcasted_iota(I32, (r, w), 1).astype(F32)
    lane = lax.broadcasted_iota(I32, (r, LANE), 1)
    out = jnp.zeros((r, LANE), F32)
    for t in range(SLC_TOP_N):
        pick = jnp.sum(jnp.where((rank == t) & valid, blk, 0.0), axis=1, keepdims=True)
        out = jnp.where(lane == t, pick, out)
    idx_ref[...] = out.astype(I32)


def _sample_select(imp, cur, n_slc):
    r = imp.shape[0]
    return pl.pallas_call(
        functools.partial(_sample_select_kernel, cur=cur, n_slc=n_slc),
        grid=(1,),
        in_specs=[pl.BlockSpec(imp.shape, lambda i: (0, 0))],
        out_specs=pl.BlockSpec((r, LANE), lambda i: (0, 0)),
        out_shape=jax.ShapeDtypeStruct((r, LANE), I32),
        compiler_params=_cparams(("arbitrary",)),
        name="sample_select",
    )(imp)


def _sample_slc_kernel(*refs, n_sel, past_len):
    blk_ref, page_ref = refs[0], refs[1]
    del page_ref
    q_ref, new_ref = refs[2], refs[3]
    c_refs = refs[4:4 + NSA_KV_HEADS * n_sel]
    o_ref = refs[4 + NSA_KV_HEADS * n_sel]
    bi = pl.program_id(0)
    hrow = lax.broadcasted_iota(I32, (NSA_HEADS, 1), 0)
    first = (hrow >> GQA_SHIFT) == 0
    nk = n_sel * PAGE
    col = lax.broadcasted_iota(I32, (NSA_HEADS, nk), 1)
    tile = col >> PAGE_SHIFT
    row = col & (PAGE - 1)
    kts, vts, blks = [], [], []
    for kh in range(NSA_KV_HEADS):
        mine = c_refs[kh * n_sel:(kh + 1) * n_sel]
        kts.append(jnp.concatenate([c[kh * NSA_HD:(kh + 1) * NSA_HD, :] for c in mine], axis=1).astype(BF16))
        vts.append(jnp.concatenate([c[(2 + kh) * NSA_HD:(3 + kh) * NSA_HD, :] for c in mine],
                                   axis=1).astype(BF16))
        blkv = jnp.zeros(col.shape, I32)
        for t in range(n_sel):
            blkv = jnp.where(tile == t, blk_ref[bi, kh * n_sel + t], blkv)
        blks.append(blkv)
    blkv = jnp.where(first, blks[0], blks[1])
    kpos = (blkv >> 1) * PAGE + row
    ok = ((row >> SLC_SHIFT) == (blkv & 1)) & (kpos < past_len)
    dist = past_len - kpos
    o_ref[...] = _attend_t(q_ref[...], kts, vts, -_slope_col() * dist.astype(F32), ok, new_ref[...])


def _sample_slc(q8, kvs_new, cache_slc_t, blk_idx, page_idx, past_len):
    b = q8.shape[0]
    n_sel = blk_idx.shape[1] // NSA_KV_HEADS
    per_b3 = lambda bi, blk, pg: (bi, 0, 0)
    in_specs = [pl.BlockSpec((None, NSA_HEADS, NSA_HD), per_b3),
                pl.BlockSpec((None, 1, NSA_KV_COLS), per_b3)]
    in_specs += [pl.BlockSpec((None, NSA_KV_COLS, PAGE),
                              functools.partial(lambda bi, blk, pg, j: (pg[bi, j], 0, 0), j=j))
                 for j in range(NSA_KV_HEADS * n_sel)]
    return pl.pallas_call(
        functools.partial(_sample_slc_kernel, n_sel=n_sel, past_len=past_len),
        grid_spec=pltpu.PrefetchScalarGridSpec(
            num_scalar_prefetch=2, grid=(b,), in_specs=in_specs,
            out_specs=pl.BlockSpec((None, NSA_HEADS, NSA_HD), per_b3)),
        out_shape=jax.ShapeDtypeStruct((b, NSA_HEADS, NSA_HD), F32),
        compiler_params=_cparams(("parallel",)),
        name="sample_slc",
    )(blk_idx, page_idx, q8, kvs_new.reshape(b, 1, NSA_KV_COLS), *([cache_slc_t] * (NSA_KV_HEADS * n_sel)))


def _heads_to_swapped(o8):
    b = o8.shape[0]
    return o8.reshape(b, NSA_HEADS // 2, 2, NSA_HD)[:, :, ::-1].reshape(b, NSA_HEADS * NSA_HD).astype(BF16)


def _pair_swap(o8):
    b = o8.shape[0]
    o = o8.reshape(b, NSA_HEADS // 2, 2, LANE)
    lo = jnp.arange(LANE) < HALF
    return jnp.where(lo, o[:, :, 1], o[:, :, 0]).reshape(b, 4 * LANE).astype(BF16)


def _rope_table(pos):
    half = MLA_ROPE // 2
    freqs = ROPE_THETA ** (-jnp.arange(half, dtype=F32) / half)
    ang = pos.astype(F32)[:, None] * freqs
    cos, sin = jnp.cos(ang), jnp.sin(ang)
    n = pos.shape[0]
    one, zero = jnp.ones((n, MLA_NOPE), F32), jnp.zeros((n, MLA_NOPE), F32)
    pad = jnp.zeros((n, LANE - MLA_NOPE - MLA_ROPE), F32)
    return jnp.concatenate([one, cos, cos, pad, zero, -sin, sin, pad], axis=1)


def _cmp_to_slc(n_cmp, n_slc, rows, cols):
    cs = np.arange(n_cmp)[:, None] * CMP_STRIDE
    ss = np.arange(n_slc)[None, :] * SLC_BLOCK
    inter = np.clip(np.minimum(cs + CMP_BLOCK, ss + SLC_BLOCK) - np.maximum(cs, ss), 0, None)
    w = np.zeros((rows, cols), np.float32)
    w[:n_cmp, :n_slc] = inter.astype(np.float32) / CMP_STRIDE
    return jnp.asarray(w, BF16)


def _prep_weights(w_in, mla_g_q, mla_g_kv, mla_w_uq, mla_w_uk, mla_w_uv, mla_w_o, nsa_pe_k, nsa_w1_k,
                  nsa_w2_k, nsa_pe_v, nsa_w1_v, nsa_w2_v, nsa_w_o, w_out, ln1_g, ln1_b, router_g_w,
                  router_g_b, router_e_w, router_e_b, ln2_g, ln2_b):
    d = w_in.shape[0]
    splits = (MLA_Q_RANK, MLA_KV_RANK, MLA_ROPE, NSA_HEADS * NSA_HD, NSA_KV_COLS, NSA_KV_COLS, NSA_KV_COLS,
              3 * NSA_HEADS, d, d)
    offs = np.cumsum(splits)[:-1].tolist()
    cq, ckv, kr, q, kvc, kvs, kvw, gn, ga, gb = jnp.split(w_in, offs, axis=1)
    r2 = MLA_ROPE // 2
    zc = lambda n: jnp.zeros((d, n), F32)
    tail = LANE - MLA_NOPE - MLA_ROPE
    kr_pad = jnp.concatenate([zc(MLA_NOPE), kr, zc(tail)], axis=1)
    kr_sw = jnp.concatenate([zc(MLA_NOPE), kr[:, r2:], kr[:, :r2], zc(tail)], axis=1)
    gn_pad = jnp.concatenate([gn, zc(LANE - gn.shape[1])], axis=1)
    w_all = jnp.concatenate([cq, ckv, q, kvc, kvs, kvw, kr_pad, kr_sw, gn_pad, ga, gb], axis=1).astype(BF16)

    def pad_heads(w, lo_cols):
        r, hh, c = w.shape
        out = jnp.zeros((r, hh, LANE), F32).at[:, :, lo_cols:lo_cols + c].set(w)
        return out

    nope, rope = mla_w_uq[:, :, :MLA_NOPE], mla_w_uq[:, :, MLA_NOPE:]
    wuq = jnp.concatenate([nope, rope, jnp.zeros(nope.shape[:2] + (tail,), F32)], axis=2)
    wuqs = jnp.concatenate([jnp.zeros_like(nope), rope[:, :, r2:], rope[:, :, :r2],
                            jnp.zeros(nope.shape[:2] + (tail,), F32)], axis=2)
    wuk = pad_heads(mla_w_uk, 0)
    odd = (jnp.arange(MLA_HEADS) % 2 == 1)[None, :, None]
    wuv = jnp.where(odd, pad_heads(mla_w_uv, HALF), pad_heads(mla_w_uv, 0))
    flat = lambda w: w.reshape(w.shape[0], -1).astype(BF16)
    wabs = jnp.zeros((MLA_HEADS, LANE, MLA_KV_RANK), F32).at[:, :MLA_NOPE, :].set(
        jnp.transpose(mla_w_uk, (1, 2, 0))).astype(BF16)
    psel = jnp.zeros((LANE, MLA_ROPE), F32).at[MLA_NOPE:MLA_NOPE + MLA_ROPE].set(jnp.eye(MLA_ROPE)).astype(BF16)

    def w1_cols(w1):
        return jnp.transpose(w1.reshape(2, CMP_STRIDE, NSA_HD, CMP_HIDDEN), (1, 2, 0, 3))

    wt = jnp.stack([w1_cols(nsa_w1_k), w1_cols(nsa_w1_v)])
    eye2 = jnp.eye(2, dtype=F32)
    w_big = jnp.einsum('tpdfh,tu,kv->ptkdfuvh', wt, eye2, eye2).reshape(
        CMP_STRIDE, NSA_KV_COLS, 2 * 2 * NSA_KV_HEADS * CMP_HIDDEN).astype(BF16)
    zpe = jnp.zeros((CMP_BLOCK * NSA_HD, CMP_HIDDEN), F32)
    w1_pe = jnp.concatenate([
        jnp.concatenate([nsa_w1_k, nsa_w1_k, zpe, zpe], axis=1),
        jnp.concatenate([zpe, zpe, nsa_w1_v, nsa_w1_v], axis=1)], axis=0)
    pe_flat = jnp.concatenate([nsa_pe_k.reshape(1, -1), nsa_pe_v.reshape(1, -1)], axis=1)
    w1_hi = w1_pe.astype(BF16)
    w1_lo = (w1_pe - w1_hi.astype(F32)).astype(BF16)
    z64 = jnp.zeros((CMP_HIDDEN, NSA_HD), F32)
    rows = [[nsa_w2_k, z64, z64, z64], [z64, nsa_w2_k, z64, z64], [z64, z64, nsa_w2_v, z64],
            [z64, z64, z64, nsa_w2_v]]
    w2_cmp = jnp.concatenate([jnp.concatenate(r, axis=1) for r in rows], axis=0).astype(BF16)

    perm = np.arange(NSA_HEADS * NSA_HD).reshape(NSA_HEADS // 2, 2, NSA_HD)[:, ::-1].reshape(-1)
    w_on = nsa_w_o[perm].astype(BF16)
    e3 = np.zeros((LANE, 3, NSA_HEADS * NSA_HD), np.float32)
    for h in range(NSA_HEADS):
        pos = (h // 2) * LANE + (0 if h % 2 else HALF)
        for j in range(3):
            e3[3 * h + j, j, pos:pos + NSA_HD] = 1.0
    e3 = jnp.asarray(e3.reshape(LANE, -1), BF16)
    w_r = jnp.concatenate([router_g_w, zc(HALF - N_GROUPS), router_e_w], axis=1)
    w_r_hi = w_r.astype(BF16)
    w_r_lo = (w_r - w_r_hi.astype(F32)).astype(BF16)
    b_r = jnp.concatenate([router_g_b, jnp.zeros((HALF - N_GROUPS,), F32), router_e_b])[None, :]
    return dict(
        w_all=w_all, wuq=flat(wuq), wuqs=flat(wuqs), wuk=flat(wuk), wuv=flat(wuv),
        gq=mla_g_q[None, :], gkv=mla_g_kv[None, :], wabs=wabs, psel=psel, w_big=w_big,
        pe_flat=pe_flat, w1_pe=jnp.stack([w1_hi, w1_lo]), w2_cmp=w2_cmp,
        w_o=mla_w_o.astype(BF16), w_on=w_on, w_out=w_out.astype(BF16), e3=e3,
        ln1_g=ln1_g[None, :], ln1_b=ln1_b[None, :], w_router=jnp.stack([w_r_hi, w_r_lo]), b_router=b_r,
        ln2_g=ln2_g[None, :], ln2_b=ln2_b[None, :])


def _prompt_layer(x, pw, pe, slopes, moe_w):
    b, s, d = x.shape
    n = b * s
    tm = min(256, s)
    z = _in_proj(x.reshape(n, d), _rope_table(jnp.arange(s)), pw, tm, s)
    o_mla = _flash(z['qm'], z['km'], z['vm'], slopes, b, s, mode='causal', kv_shared=False, alibi=False,
                   swap=False)
    nb = s // CMP_STRIDE
    fs = _cmp_fs_dense(z['kvc'], pw['w_big'])
    kc_var, kc_var_t = _cmp_finish(fs, nb, pe, pw)
    n_slc = -(-s // SLC_BLOCK)
    wmap_t = jnp.transpose(_cmp_to_slc(nb - 1, n_slc, nb, LANE))
    o_cmp, sel = _cmp_select(z['qn'], kc_var, kc_var_t, slopes, wmap_t, b, s, n_slc)
    emat_t = jnp.asarray((np.arange(s)[:, None] // SLC_BLOCK) == np.arange(LANE)[None, :], BF16)
    o_slc = _flash(z['qn'], z['kvsv'], z['kvsvt'], slopes, b, s, mode='select', kv_shared=True, alibi=True,
                   swap=True, selt=sel, emat_t=emat_t)
    o_win = _flash(z['qn'], z['kvwv'], z['kvwvt'], slopes, b, s, mode='window', kv_shared=True, alibi=True,
                   swap=True)
    h, rt = _merge(x.reshape(n, d), o_mla, o_cmp, o_slc, o_win, z, pw, tm)
    y = _moe(h, rt, pw, *moe_w)
    return y.reshape(b, s, d), z


def _sample_layer(x, pw, pe, slopes, moe_w, cache_ckv, cache_kr_t, cache_cmp_t, cache_slc_t, state_t, page_table):
    b, t, d = x.shape
    n_pages = page_table.shape[1]
    past_len = n_pages * PAGE
    z = _in_proj(x.reshape(b, d), _rope_table(jnp.full((b,), past_len)), pw, b, b)
    qlat = _q_latent(z['qm'], pw['wabs'])
    qr = z['qm'].reshape(b, MLA_HEADS, LANE)[:, :, MLA_NOPE:MLA_NOPE + MLA_ROPE]
    o_lat = _mla_decode(qlat, qr, z['ckv'], z['kr'], cache_ckv, cache_kr_t, page_table)
    o_mla = _mla_out_up(o_lat.reshape(b, MLA_HEADS * MLA_KV_RANK), pw['wuv'])
    nb = past_len // CMP_STRIDE
    fs = _cmp_fs_paged(cache_cmp_t, page_table, pw['w_big'])
    kc_var, _ = _cmp_finish(fs, nb, pe, pw)
    n_slc = -(-(past_len + t) // SLC_BLOCK)
    assert n_slc >= SLC_TOP_N and past_len % SLC_BLOCK == 0
    wcols = -(-n_slc // LANE) * LANE
    wmap = _cmp_to_slc(nb - 1, n_slc, nb, wcols)
    q8 = z['qnat'].reshape(b, NSA_HEADS, NSA_HD)
    o_cmp8, imp, o_win8, win_out = _sample_cmp_win(z['qn'], q8, kc_var, wmap, state_t, z['kvw'], past_len)
    cur = past_len // SLC_BLOCK
    idx = _sample_select(imp.reshape(b * NSA_KV_HEADS, wcols), cur, n_slc)[:, :SLC_TOP_N]
    blk = idx.reshape(b, NSA_KV_HEADS * SLC_TOP_N)
    safe = jnp.minimum(blk, past_len // SLC_BLOCK - 1)
    pages = jnp.take_along_axis(page_table, safe // (PAGE // SLC_BLOCK), axis=1).astype(I32)
    o_slc8 = _sample_slc(q8, z['kvs'], cache_slc_t, blk, pages, past_len)
    h, rt = _merge(x.reshape(b, d), o_mla, _pair_swap(o_cmp8), _heads_to_swapped(o_slc8),
                   _heads_to_swapped(o_win8), z, pw, b)
    y = _moe(h, rt, pw, *moe_w)
    return y.reshape(b, t, d), z, win_out


def kernel(x_prompt, x_sample, cache_mla_ckv, cache_mla_krope, cache_nsa_cmp_kv, cache_nsa_slc_kv,
           state_nsa_win_kv, page_table, w_in, mla_g_q, mla_g_kv, mla_w_uq, mla_w_uk, mla_w_uv, mla_w_o,
           nsa_pe_k, nsa_w1_k, nsa_w2_k, nsa_pe_v, nsa_w1_v, nsa_w2_v, nsa_w_o, w_out, ln1_g, ln1_b,
           router_g_w, router_g_b, router_e_w, router_e_b, moe_w1, moe_w3, moe_w2, ln2_g, ln2_b):
    assert w_in.shape[0] == DEPTH and x_sample.shape[1] == 1
    b, s, d = x_prompt.shape
    bs = x_sample.shape[0]
    kv_shape = (2, NSA_KV_HEADS, NSA_HD)
    pw = _prep_weights(w_in[0], mla_g_q[0], mla_g_kv[0], mla_w_uq[0], mla_w_uk[0], mla_w_uv[0], mla_w_o[0],
                       nsa_pe_k[0], nsa_w1_k[0], nsa_w2_k[0], nsa_pe_v[0], nsa_w1_v[0], nsa_w2_v[0],
                       nsa_w_o[0], w_out[0], ln1_g[0], ln1_b[0], router_g_w[0], router_g_b[0],
                       router_e_w[0], router_e_b[0], ln2_g[0], ln2_b[0])
    slopes = 2.0 ** (-8.0 * jnp.arange(1, NSA_HEADS + 1, dtype=F32) / NSA_HEADS)
    moe_w = (moe_w1, moe_w3, moe_w2)
    pe = _cmp_pe(pw)
    y_p, zp = _prompt_layer(x_prompt, pw, pe, slopes, moe_w)
    def rows_last(c):
        return jnp.transpose(c, (0, 2, 3, 4, 1)).reshape(c.shape[0], NSA_KV_COLS, c.shape[1])

    y_s, zs, win_t = _sample_layer(
        x_sample, pw, pe, slopes, moe_w, cache_mla_ckv[0], jnp.transpose(cache_mla_krope[0], (0, 2, 1)),
        rows_last(cache_nsa_cmp_kv[0]), rows_last(cache_nsa_slc_kv[0]), rows_last(state_nsa_win_kv[0]),
        page_table)
    def rows_first(t):
        return jnp.transpose(t.reshape((t.shape[0],) + kv_shape + (-1,)), (0, 4, 1, 2, 3))[None]

    wp = min(WINDOW, s)
    return (y_p, y_s,
            zp['ckv'].reshape(1, b, s, MLA_KV_RANK), zs['ckv'].reshape(1, bs, 1, MLA_KV_RANK),
            zp['kr'].reshape(1, b, s, MLA_ROPE), zs['kr'].reshape(1, bs, 1, MLA_ROPE),
            rows_first(zp['kvct']), zs['kvc'].reshape((1, bs, 1) + kv_shape),
            rows_first(zp['kvst']), zs['kvs'].reshape((1, bs, 1) + kv_shape),
            rows_first(zp['kvwt'][:, :, s - wp:]), rows_first(win_t))
```

```python
import functools

import numpy as np
import jax
import jax.numpy as jnp
from jax import lax
from jax.experimental import pallas as pl
from jax.experimental.pallas import tpu as pltpu

F32 = jnp.float32
BF16 = jnp.bfloat16
I32 = jnp.int32

PAGE = 128
MLA_HEADS = 8
MLA_NOPE = 64
MLA_ROPE = 32
MLA_V = 64
MLA_Q_RANK = 256
MLA_KV_RANK = 256
ROPE_THETA = 10000.0
NSA_HEADS = 8
NSA_KV_HEADS = 2
NSA_GQA = NSA_HEADS // NSA_KV_HEADS
NSA_HD = 64
CMP_BLOCK = 32
CMP_STRIDE = 16
CMP_HIDDEN = 64
SLC_BLOCK = 64
SLC_TOP_N = 16
N_LOCAL_SLC = 2
WINDOW = 512
FORCE_SCORE = 1.0e4
N_GROUPS = 8
EXPERTS_PER_GROUP = 8
N_EXPERTS = N_GROUPS * EXPERTS_PER_GROUP
TOP_K = 2
D_EXPERT = 256
MOE_BLOCK = 128
LN_EPS = 1e-5
RMS_EPS = 1e-6
DEPTH = 1
DEEPNORM_ALPHA = (2.0 * DEPTH) ** 0.25
NSA_KV_COLS = 2 * NSA_KV_HEADS * NSA_HD

PAGE_SHIFT = PAGE.bit_length() - 1
SLC_SHIFT = SLC_BLOCK.bit_length() - 1
GQA_SHIFT = NSA_GQA.bit_length() - 1
LANE = 128
HALF = LANE // 2
VMEM_LIMIT = 56 * 1024 * 1024
NEG = -1e30
DMA_UNROLL = 8
LOG2E = 1.4426950408889634

_C_CQ, _C_CKV, _C_Q, _C_KVC, _C_KVS, _C_KVW, _C_KRP, _C_KRS, _C_GN, _C_GA = (
    0, 256, 512, 1024, 1280, 1536, 1792, 1920, 2048, 2176)


def _cparams(sem):
    return pltpu.CompilerParams(dimension_semantics=sem, vmem_limit_bytes=VMEM_LIMIT)


def _dot(a, b):
    return jnp.dot(a, b, preferred_element_type=F32)


def _dot_nt(a, b):
    return lax.dot_general(a, b, (((1,), (1,)), ((), ())), preferred_element_type=F32)


def _split_dot(a, w_hi, w_lo=None):
    a_hi = a.astype(BF16)
    a_lo = (a - a_hi.astype(F32)).astype(BF16)
    r = _dot(a_hi, w_hi) + _dot(a_lo, w_hi)
    if w_lo is not None:
        r = r + _dot(a_hi, w_lo)
    return r


def _sigmoid(x):
    return 1.0 / (1.0 + jnp.exp(-x))


def _kv_variants(x0, x1):
    lo = lax.broadcasted_iota(I32, x0.shape, 1) < HALF
    r0 = pltpu.roll(x0, HALF, 1)
    r1 = pltpu.roll(x1, HALF, 1)
    return (jnp.where(lo, x0, r1), jnp.where(lo, x1, r0), jnp.where(lo, r0, x1), jnp.where(lo, r1, x0))


def _inproj_kernel(x_ref, cs_ref, w_ref, wuq_ref, wuqs_ref, wuk_ref, wuv_ref, gq_ref, gkv_ref,
                   qm_ref, km_ref, vm_ref, ckv_ref, kr_ref, qn_ref, kvc_ref, kvs_ref, kvw_ref,
                   kvsv_ref, kvwv_ref, gn_ref, ga_ref, gb_ref, qnat_ref, kvsvt_ref, kvwvt_ref,
                   kvct_ref, kvst_ref, kvwt_ref, *, mla_scale):
    xb = x_ref[...].astype(BF16)

    def proj(a, b):
        return _dot(xb, w_ref[:, a:b])

    cos = cs_ref[:, :LANE]
    sin = cs_ref[:, LANE:]

    def rms(z, g):
        return z * lax.rsqrt(jnp.mean(z * z, axis=-1, keepdims=True) + RMS_EPS) * g

    cq = rms(proj(_C_CQ, _C_CKV), gq_ref[...]).astype(BF16)
    q = _dot(cq, wuq_ref[...])
    qs = _dot(cq, wuqs_ref[...])
    ckv = rms(proj(_C_CKV, _C_Q), gkv_ref[...])
    ckv_ref[...] = ckv
    ckvb = ckv.astype(BF16)
    kr = proj(_C_KRP, _C_KRS) * cos + proj(_C_KRS, _C_GN) * sin
    kr_ref[...] = kr[:, MLA_NOPE:MLA_NOPE + MLA_ROPE]
    kn = _dot(ckvb, wuk_ref[...])
    v = _dot(ckvb, wuv_ref[...])
    for h in range(MLA_HEADS):
        sl = slice(h * LANE, (h + 1) * LANE)
        qm_ref[:, sl] = ((q[:, sl] * cos + qs[:, sl] * sin) * mla_scale).astype(BF16)
        km_ref[:, sl] = (kn[:, sl] + kr).astype(BF16)
        vm_ref[sl, :] = v[:, sl].T.astype(BF16)
    qn = proj(_C_Q, _C_KVC) * (NSA_HD ** -0.5 * LOG2E)
    qnat_ref[...] = qn.astype(BF16)
    lo = lax.broadcasted_iota(I32, (qn.shape[0], LANE), 1) < HALF
    for j in range(NSA_HEADS // 2):
        blk = qn[:, j * LANE:(j + 1) * LANE]
        qn_ref[:, (2 * j) * LANE:(2 * j + 1) * LANE] = jnp.where(lo, blk, 0.0).astype(BF16)
        qn_ref[:, (2 * j + 1) * LANE:(2 * j + 2) * LANE] = jnp.where(lo, 0.0, blk).astype(BF16)
    zc = proj(_C_KVC, _C_KVS)
    kvc_ref[...] = zc
    kvct_ref[...] = zc.T
    for src, dst, dst_t, var, vart in ((_C_KVS, kvs_ref, kvst_ref, kvsv_ref, kvsvt_ref),
                                       (_C_KVW, kvw_ref, kvwt_ref, kvwv_ref, kvwvt_ref)):
        z = proj(src, src + NSA_KV_COLS)
        dst[...] = z
        dst_t[...] = z.T
        for i, t in enumerate(_kv_variants(z[:, :LANE], z[:, LANE:])):
            var[:, i * LANE:(i + 1) * LANE] = t.astype(BF16)
            vart[i * LANE:(i + 1) * LANE, :] = t.T.astype(BF16)
    gn_ref[...] = _sigmoid(proj(_C_GN, _C_GA))
    d = ga_ref.shape[1]
    ga_ref[...] = _sigmoid(proj(_C_GA, _C_GA + d)).astype(BF16)
    gb_ref[...] = _sigmoid(proj(_C_GA + d, _C_GA + 2 * d)).astype(BF16)


def _in_proj(x, cs, pw, tm, seq):
    n, d = x.shape
    per_seq = seq // tm
    n_cs = cs.shape[0] // tm
    row = lambda i: (i, 0)
    full = lambda i: (0, 0)
    wcols = pw['w_all'].shape[1]
    outs = [
        ('qm', 8 * LANE, BF16), ('km', 8 * LANE, BF16), ('vm', 8 * LANE, BF16),
        ('ckv', MLA_KV_RANK, F32), ('kr', MLA_ROPE, F32), ('qn', 8 * LANE, BF16),
        ('kvc', NSA_KV_COLS, F32), ('kvs', NSA_KV_COLS, F32), ('kvw', NSA_KV_COLS, F32),
        ('kvsv', 4 * LANE, BF16), ('kvwv', 4 * LANE, BF16), ('gn', LANE, F32),
        ('ga', d, BF16), ('gb', d, BF16), ('qnat', NSA_HEADS * NSA_HD, BF16),
        ('kvsvt', 4 * LANE, BF16), ('kvwvt', 4 * LANE, BF16),
        ('kvct', NSA_KV_COLS, F32), ('kvst', NSA_KV_COLS, F32), ('kvwt', NSA_KV_COLS, F32)]
    transposed = ('vm', 'kvsvt', 'kvwvt')
    by_seq = ('kvct', 'kvst', 'kvwt')

    def out_spec(k, c):
        if k in transposed:
            return pl.BlockSpec((c, tm), lambda i: (0, i))
        if k in by_seq:
            return pl.BlockSpec((None, c, tm), lambda i: (i // per_seq, 0, i % per_seq))
        return pl.BlockSpec((tm, c), row)

    def out_shape(k, c, t):
        shape = (c, n) if k in transposed else (n // seq, c, seq) if k in by_seq else (n, c)
        return jax.ShapeDtypeStruct(shape, t)

    res = pl.pallas_call(
        functools.partial(_inproj_kernel, mla_scale=(MLA_NOPE + MLA_ROPE) ** -0.5 * LOG2E),
        grid=(n // tm,),
        in_specs=[pl.BlockSpec((tm, d), row),
                  pl.BlockSpec((tm, 2 * LANE), lambda i: (i % n_cs, 0)),
                  pl.BlockSpec((d, wcols), full),
                  pl.BlockSpec((MLA_Q_RANK, 8 * LANE), full),
                  pl.BlockSpec((MLA_Q_RANK, 8 * LANE), full),
                  pl.BlockSpec((MLA_KV_RANK, 8 * LANE), full),
                  pl.BlockSpec((MLA_KV_RANK, 8 * LANE), full),
                  pl.BlockSpec((1, MLA_Q_RANK), full),
                  pl.BlockSpec((1, MLA_KV_RANK), full)],
        out_specs=[out_spec(k, c) for k, c, _ in outs],
        out_shape=[out_shape(k, c, t) for k, c, t in outs],
        compiler_params=_cparams(("parallel",)),
        name="in_proj",
    )(x, cs, pw['w_all'], pw['wuq'], pw['wuqs'], pw['wuk'], pw['wuv'], pw['gq'], pw['gkv'])
    return {k: v for (k, _, _), v in zip(outs, res)}


def _flash_kernel(*refs, mode, t, alibi, swap):
    if mode == 'select':
        slopes_ref, q_ref, k_ref, vt_ref, selt_ref, et_ref, o_ref, m_scr, l_scr, acc_scr = refs
    else:
        slopes_ref, q_ref, k_ref, vt_ref, o_ref, m_scr, l_scr, acc_scr = refs
    hp = pl.program_id(1)
    qi = pl.program_id(2)
    q0 = qi * t
    krow = lax.broadcasted_iota(I32, (t, t), 0)
    dmat = lax.broadcasted_iota(I32, (t, t), 1) - krow
    krow_f = krow.astype(F32)
    qs = [q_ref[:, e * LANE:(e + 1) * LANE] for e in range(2)]
    for e in range(2):
        m_scr[e] = jnp.full((1, t), NEG, F32)
        l_scr[e] = jnp.zeros((1, t), F32)
        acc_scr[e] = jnp.zeros((LANE, t), F32)

    def step(kt, edge):
        k0 = pl.multiple_of(kt * t, t)
        ok = None
        if edge:
            dist = dmat + (q0 - k0)
            ok = dist >= 0
            if mode == 'window':
                ok = ok & (dist <= WINDOW)
        if mode == 'select':
            picked = _dot(et_ref[pl.ds(k0, t), :], selt_ref[...]) > 0.5
            ok = picked if ok is None else ok & picked
        for e in range(2):
            k = k_ref[pl.ds(k0, t), e * LANE:(e + 1) * LANE]
            vt = vt_ref[e * LANE:(e + 1) * LANE, pl.ds(k0, t)]
            shift = 0.0
            if alibi:
                slope = slopes_ref[2 * hp + e] * LOG2E
                shift = slope * (k0 - q0).astype(F32)
            s = _dot_nt(k, qs[e])
            if alibi:
                s = s + slope * krow_f
            if ok is not None:
                s = jnp.where(ok, s, NEG)
            m_prev = m_scr[e]
            m_new = jnp.maximum(m_prev, jnp.max(s, axis=0, keepdims=True) + shift)
            alpha = jnp.exp2(m_prev - m_new)
            p = jnp.exp2(s - (m_new - shift))
            l_scr[e] = alpha * l_scr[e] + jnp.sum(p, axis=0, keepdims=True)
            acc_scr[e] = alpha * acc_scr[e] + _dot(vt, p.astype(BF16))
            m_scr[e] = m_new

    def loop(lo, hi, edge):
        def body(kt, c):
            step(kt, edge)
            return c
        lax.fori_loop(lo, hi, body, 0)

    if mode == 'window':
        loop(jnp.maximum(q0 - WINDOW, 0) // t, qi + 1, True)
    else:
        loop(0, qi, False)
        step(qi, True)
    o0 = acc_scr[0] / l_scr[0]
    o1 = acc_scr[1] / l_scr[1]
    lo = lax.broadcasted_iota(I32, (LANE, t), 0) < HALF
    o_ref[...] = (jnp.where(lo, o1, o0) if swap else jnp.where(lo, o0, o1)).T.astype(o_ref.dtype)


def _flash(q, k, vt, slopes, b, s, *, mode, kv_shared, alibi, swap, selt=None, emat_t=None):
    tq = min(512, s)
    nq = s // tq
    k_idx = (lambda bi, hp, qi: (bi, hp // 2)) if kv_shared else (lambda bi, hp, qi: (bi, hp))
    vt_idx = (lambda bi, hp, qi: (hp // 2, bi)) if kv_shared else (lambda bi, hp, qi: (hp, bi))
    in_specs = [pl.BlockSpec(memory_space=pltpu.SMEM),
                pl.BlockSpec((tq, 2 * LANE), lambda bi, hp, qi: (bi * nq + qi, hp)),
                pl.BlockSpec((s, 2 * LANE), k_idx),
                pl.BlockSpec((2 * LANE, s), vt_idx)]
    args = [slopes, q, k, vt]
    if mode == 'select':
        in_specs += [pl.BlockSpec((None, None, LANE, tq), lambda bi, hp, qi: (bi, hp // 2, 0, qi)),
                     pl.BlockSpec((s, LANE), lambda bi, hp, qi: (0, 0))]
        args += [selt, emat_t]
    return pl.pallas_call(
        functools.partial(_flash_kernel, mode=mode, t=tq, alibi=alibi, swap=swap),
        grid=(b, 4, nq),
        in_specs=in_specs,
        out_specs=pl.BlockSpec((tq, LANE), lambda bi, hp, qi: (bi * nq + qi, hp)),
        out_shape=jax.ShapeDtypeStruct((b * s, 4 * LANE), BF16),
        scratch_shapes=[pltpu.VMEM((2, 1, tq), F32), pltpu.VMEM((2, 1, tq), F32),
                        pltpu.VMEM((2, LANE, tq), F32)],
        compiler_params=_cparams(("parallel", "parallel", "arbitrary")),
        name="flash_" + mode,
    )(*args)


def _cmp_first_linear(lo, hi, w_ref, o_ref, r0=0, r=None):
    r = o_ref.shape[0] if r is None else r
    acc = jnp.zeros((r, o_ref.shape[1]), F32)
    for p in range(CMP_STRIDE):
        rows = pl.ds(r0 * CMP_STRIDE + p, r, stride=CMP_STRIDE)
        x = jnp.concatenate([lo[rows, :], hi[rows, :]], axis=1)
        acc = acc + _dot(x.astype(BF16), w_ref[p])
    o_ref[r0:r0 + r, :] = acc


def _cmp_fs_kernel(lo, hi, w_ref, o_ref):
    _cmp_first_linear(lo, hi, w_ref, o_ref)


def _cmp_fs_dense(x, w_big):
    n, c = x.shape
    tr = min(256, n // CMP_STRIDE)
    return pl.pallas_call(
        _cmp_fs_kernel,
        grid=(n // CMP_STRIDE // tr,),
        in_specs=[pl.BlockSpec((tr * CMP_STRIDE, LANE), lambda i: (i, 0)),
                  pl.BlockSpec((tr * CMP_STRIDE, LANE), lambda i: (i, 1)),
                  pl.BlockSpec(w_big.shape, lambda i: (0, 0, 0))],
        out_specs=pl.BlockSpec((tr, 4 * LANE), lambda i: (i, 0)),
        out_shape=jax.ShapeDtypeStruct((n // CMP_STRIDE, 4 * LANE), F32),
        compiler_params=_cparams(("parallel",)),
        name="cmp_fs",
    )(x, x, w_big)


def _prefetched_pages(pt_ref, srcs, bufs, sems, n_pages):
    bi = pl.program_id(0)
    slot = bi % 2

    def copies(seq, s, j):
        pg = pt_ref[seq, j]
        return [pltpu.make_async_copy(src.at[pg], buf.at[s, j], sems.at[s]) for src, buf in zip(srcs, bufs)]

    def start_all(seq, s):
        def body(j, c):
            for cp in copies(seq, s, j):
                cp.start()
            return c
        lax.fori_loop(0, n_pages, body, 0, unroll=DMA_UNROLL)

    @pl.when(bi == 0)
    def _():
        start_all(0, 0)

    @pl.when(bi + 1 < pl.num_programs(0))
    def _():
        start_all(bi + 1, 1 - slot)

    def wait(j, c):
        for cp in copies(bi, slot, j):
            cp.wait()
        return c

    lax.fori_loop(0, n_pages, wait, 0, unroll=DMA_UNROLL)
    return slot


def _cmp_fs_paged_kernel(pt_ref, cache_hbm, w_ref, o_ref, pbuf, sems, lo, hi, *, n_pages):
    slot = _prefetched_pages(pt_ref, (cache_hbm,), (pbuf,), sems, n_pages)

    groups = 4 if n_pages % 4 == 0 else 1
    per = n_pages // groups
    for g in range(groups):
        for j in range(g * per, (g + 1) * per):
            lo[j * PAGE:(j + 1) * PAGE, :] = pbuf[slot, j, :LANE, :].T
            hi[j * PAGE:(j + 1) * PAGE, :] = pbuf[slot, j, LANE:, :].T
        blocks = per * (PAGE // CMP_STRIDE)
        _cmp_first_linear(lo, hi, w_ref, o_ref, g * blocks, blocks)


def _cmp_fs_paged(cache_t, page_table, w_big):
    b, n_pages = page_table.shape
    rows = n_pages * (PAGE // CMP_STRIDE)
    return pl.pallas_call(
        functools.partial(_cmp_fs_paged_kernel, n_pages=n_pages),
        grid_spec=pltpu.PrefetchScalarGridSpec(
            num_scalar_prefetch=1, grid=(b,),
            in_specs=[pl.BlockSpec(memory_space=pl.ANY),
                      pl.BlockSpec(w_big.shape, lambda bi, pt: (0, 0, 0))],
            out_specs=pl.BlockSpec((rows, 4 * LANE), lambda bi, pt: (bi, 0)),
            scratch_shapes=[pltpu.VMEM((2, n_pages, NSA_KV_COLS, PAGE), F32), pltpu.SemaphoreType.DMA((2,)),
                            pltpu.VMEM((n_pages * PAGE, LANE), F32), pltpu.VMEM((n_pages * PAGE, LANE), F32)]),
        out_shape=jax.ShapeDtypeStruct((b * rows, 4 * LANE), F32),
        compiler_params=_cparams(("arbitrary",)),
        name="cmp_fs_paged",
    )(page_table, cache_t, w_big)


def _gelu_tanh(x):
    return 0.5 * x * (1.0 + jnp.tanh(0.7978845608028654 * (x + 0.044715 * x * x * x)))


def _cmp_pe_kernel(pe_ref, w1_ref, o_ref):
    o_ref[...] = _split_dot(pe_ref[...], w1_ref[0], w1_ref[1])


def _cmp_pe(pw):
    return pl.pallas_call(
        _cmp_pe_kernel,
        grid=(1,),
        in_specs=[pl.BlockSpec(pw['pe_flat'].shape, lambda i: (0, 0)),
                  pl.BlockSpec(pw['w1_pe'].shape, lambda i: (0, 0, 0))],
        out_specs=pl.BlockSpec((1, 2 * LANE), lambda i: (0, 0)),
        out_shape=jax.ShapeDtypeStruct((1, 2 * LANE), F32),
        compiler_params=_cparams(("arbitrary",)),
        name="cmp_pe",
    )(pw['pe_flat'], pw['w1_pe'])


def _cmp_finish_kernel(fs_ref, pe_ref, w2_ref, o_ref, ot_ref):
    fs = fs_ref[...]
    n = fs.shape[0]
    h = _gelu_tanh(fs[:, :2 * LANE] + pltpu.roll(fs[:, 2 * LANE:], n - 1, 0) + pe_ref[...])
    kc = _dot(h.astype(BF16), w2_ref[...])
    for i, t in enumerate(_kv_variants(kc[:, :LANE], kc[:, LANE:])):
        o_ref[:, i * LANE:(i + 1) * LANE] = t.astype(BF16)
        ot_ref[i * LANE:(i + 1) * LANE, :] = t.T.astype(BF16)


def _cmp_finish(fs, nb, pe, pw):
    r = fs.shape[0]
    full2 = lambda i: (0, 0)
    return pl.pallas_call(
        _cmp_finish_kernel,
        grid=(r // nb,),
        in_specs=[pl.BlockSpec((nb, 4 * LANE), lambda i: (i, 0)),
                  pl.BlockSpec(pe.shape, full2),
                  pl.BlockSpec(pw['w2_cmp'].shape, full2)],
        out_specs=[pl.BlockSpec((nb, 4 * LANE), lambda i: (i, 0)),
                   pl.BlockSpec((None, 4 * LANE, nb), lambda i: (i, 0, 0))],
        out_shape=[jax.ShapeDtypeStruct((r, 4 * LANE), BF16),
                   jax.ShapeDtypeStruct((r // nb, 4 * LANE, nb), BF16)],
        compiler_params=_cparams(("parallel",)),
        name="cmp_finish",
    )(fs, pe, pw['w2_cmp'])


def _masked_softmax(s, ok, axis=-1):
    s = jnp.where(ok, s, -jnp.inf)
    m = jnp.max(s, axis=axis, keepdims=True)
    m = jnp.where(m > -jnp.inf, m, 0.0)
    e = jnp.exp2(s - m)
    d = jnp.sum(e, axis=axis, keepdims=True)
    return e / jnp.where(d > 0.0, d, 1.0)


def _select_rank(imp, cur, n_slc):
    r, w = imp.shape
    blk = lax.broadcasted_iota(I32, (r, w), 1)
    valid = (blk <= cur) & (blk < n_slc)
    forced = (blk == 0) | (valid & (blk > cur - N_LOCAL_SLC))
    score = jnp.where(forced, FORCE_SCORE, jnp.where(valid, imp, -1.0))
    score = jnp.where(blk < n_slc, score, -2.0)

    def body(i, rank):
        col = jnp.sum(jnp.where(blk == i, score, 0.0), axis=1, keepdims=True)
        beats = (col > score) | ((col == score) & (i < blk))
        return rank + jnp.where(beats, 1.0, 0.0)

    rank = lax.fori_loop(0, n_slc, body, jnp.zeros((r, w), F32), unroll=True if n_slc <= 32 else 4)
    return rank, valid


def _cmp_select_kernel(slopes_ref, q_ref, kc_ref, kct_ref, wmapt_ref, o_ref, sel_ref, *, tq, n_slc):
    kvh = pl.program_id(1)
    q0 = pl.program_id(2) * tq
    nk = kc_ref.shape[0]
    qpos = q0 + lax.broadcasted_iota(I32, (nk, tq), 1)
    kend = lax.broadcasted_iota(I32, (nk, tq), 0) * CMP_STRIDE + (CMP_BLOCK - 1)
    ok = kend <= qpos
    dist = (qpos - kend).astype(F32)
    psum = jnp.zeros((nk, tq), F32)
    outs = []
    for g in range(NSA_GQA):
        sl = slice((g % 2) * LANE, (g % 2 + 1) * LANE)
        s = _dot_nt(kc_ref[:, sl], q_ref[:, g * LANE:(g + 1) * LANE]) - (slopes_ref[kvh * NSA_GQA + g] * LOG2E) * dist
        p = _masked_softmax(s, ok, axis=0)
        psum = psum + p
        outs.append(_dot(kct_ref[sl, :], p.astype(BF16)))
    lo = lax.broadcasted_iota(I32, (LANE, tq), 0) < HALF
    o_ref[:, :LANE] = jnp.where(lo, outs[1], outs[0]).T.astype(o_ref.dtype)
    o_ref[:, LANE:] = jnp.where(lo, outs[3], outs[2]).T.astype(o_ref.dtype)
    ps_hi = psum.astype(BF16)
    ps_lo = (psum - ps_hi.astype(F32)).astype(BF16)
    imp = _dot(wmapt_ref[...], ps_hi) + _dot(wmapt_ref[...], ps_lo)
    rows = -(-n_slc // 8) * 8
    imp = imp[:rows]
    blk = lax.broadcasted_iota(I32, (rows, tq), 0)
    cur = (q0 + lax.broadcasted_iota(I32, (1, tq), 1)) >> SLC_SHIFT
    valid = (blk <= cur) & (blk < n_slc)
    forced = (blk == 0) | (valid & (blk > cur - N_LOCAL_SLC))
    score = jnp.where(forced, FORCE_SCORE, jnp.where(valid, imp, -1.0))
    rank = jnp.zeros((rows, tq), F32)
    for i in range(n_slc):
        other = score[i:i + 1, :]
        rank = rank + jnp.where((other > score) | ((other == score) & (i < blk)), 1.0, 0.0)
    sel = jnp.where((rank < SLC_TOP_N) & valid, 1.0, 0.0)
    sel_ref[...] = jnp.concatenate([sel, jnp.zeros((LANE - rows, tq), F32)], axis=0).astype(sel_ref.dtype)


def _cmp_select(qn, kc_var, kc_var_t, slopes, wmap_t, b, s, n_slc):
    tq = min(256, s)
    nq = s // tq
    nk = kc_var.shape[0] // b
    assert n_slc <= LANE
    return pl.pallas_call(
        functools.partial(_cmp_select_kernel, tq=tq, n_slc=n_slc),
        grid=(b, NSA_KV_HEADS, nq),
        in_specs=[pl.BlockSpec(memory_space=pltpu.SMEM),
                  pl.BlockSpec((tq, 4 * LANE), lambda bi, kh, qi: (bi * nq + qi, kh)),
                  pl.BlockSpec((nk, 2 * LANE), lambda bi, kh, qi: (bi, kh)),
                  pl.BlockSpec((None, 2 * LANE, nk), lambda bi, kh, qi: (bi, kh, 0)),
                  pl.BlockSpec(wmap_t.shape, lambda bi, kh, qi: (0, 0))],
        out_specs=[pl.BlockSpec((tq, 2 * LANE), lambda bi, kh, qi: (bi * nq + qi, kh)),
                   pl.BlockSpec((None, None, LANE, tq), lambda bi, kh, qi: (bi, kh, 0, qi))],
        out_shape=[jax.ShapeDtypeStruct((b * s, 4 * LANE), BF16),
                   jax.ShapeDtypeStruct((b, NSA_KV_HEADS, LANE, s), BF16)],
        compiler_params=_cparams(("parallel", "parallel", "parallel")),
        name="cmp_select",
    )(slopes, qn, kc_var, kc_var_t, wmap_t)


def _layer_norm(x, g, b):
    mu = jnp.mean(x, axis=-1, keepdims=True)
    xc = x - mu
    var = jnp.mean(xc * xc, axis=-1, keepdims=True)
    return xc * lax.rsqrt(var + LN_EPS) * g + b


def _merge_kernel(x_ref, omla_ref, ocmp_ref, oslc_ref, owin_ref, gn_ref, ga_ref, gb_ref,
                  wo_ref, wn_ref, wout_ref, e3_ref, g1_ref, b1_ref, wr_ref, br_ref,
                  h_ref, rt_ref):
    gn = gn_ref[...]
    w = 4 * LANE
    ge = _split_dot(gn, e3_ref[...])
    o_nsa = (ge[:, :w] * ocmp_ref[...].astype(F32) + ge[:, w:2 * w] * oslc_ref[...].astype(F32)
             + ge[:, 2 * w:] * owin_ref[...].astype(F32))
    u = (ga_ref[...].astype(F32) * _dot(omla_ref[...], wo_ref[...])
         + gb_ref[...].astype(F32) * _dot(o_nsa.astype(BF16), wn_ref[...]))
    h = _layer_norm(DEEPNORM_ALPHA * x_ref[...] + _dot(u.astype(BF16), wout_ref[...]),
                    g1_ref[...], b1_ref[...])
    h_ref[...] = h
    lg = _split_dot(h, wr_ref[0], wr_ref[1]) + br_ref[...]
    tm = lg.shape[0]
    lane = lax.broadcasted_iota(I32, (tm, LANE), 1)
    is_g = lane < N_GROUPS
    gm = jnp.max(jnp.where(is_g, lg, -jnp.inf), axis=1, keepdims=True)
    gidx = jnp.min(jnp.where(is_g & (lg == gm), lane, LANE), axis=1, keepdims=True)
    p_group = 1.0 / jnp.sum(jnp.where(is_g, jnp.exp(lg - gm), 0.0), axis=1, keepdims=True)
    in_g = (lane >= HALF) & (((lane - HALF) >> 3) == gidx)
    m1 = jnp.max(jnp.where(in_g, lg, -jnp.inf), axis=1, keepdims=True)
    i1 = jnp.min(jnp.where(in_g & (lg == m1), lane, 2 * LANE), axis=1, keepdims=True)
    rest = in_g & (lane != i1)
    m2 = jnp.max(jnp.where(rest, lg, -jnp.inf), axis=1, keepdims=True)
    i2 = jnp.min(jnp.where(rest & (lg == m2), lane, 2 * LANE), axis=1, keepdims=True)
    t = jnp.exp(m2 - m1)
    g1 = p_group / (1.0 + t)
    g2 = p_group * t / (1.0 + t)
    rt = jnp.where(lane == 0, (i1 - HALF).astype(F32),
                   jnp.where(lane == 1, (i2 - HALF).astype(F32),
                             jnp.where(lane == 2, g1, jnp.where(lane == 3, g2, 0.0))))
    rt_ref[...] = rt


def _merge(x, omla, ocmp, oslc, owin, z, pw, tm):
    n, d = x.shape
    row = lambda i: (i, 0)
    full = lambda i: (0, 0)
    w = 4 * LANE
    return pl.pallas_call(
        _merge_kernel,
        grid=(n // tm,),
        in_specs=[pl.BlockSpec((tm, d), row), pl.BlockSpec((tm, w), row), pl.BlockSpec((tm, w), row),
                  pl.BlockSpec((tm, w), row), pl.BlockSpec((tm, w), row), pl.BlockSpec((tm, LANE), row),
                  pl.BlockSpec((tm, d), row), pl.BlockSpec((tm, d), row),
                  pl.BlockSpec((w, d), full), pl.BlockSpec((w, d), full), pl.BlockSpec((d, d), full),
                  pl.BlockSpec((LANE, 3 * w), full), pl.BlockSpec((1, d), full), pl.BlockSpec((1, d), full),
                  pl.BlockSpec((2, d, LANE), lambda i: (0, 0, 0)), pl.BlockSpec((1, LANE), full)],
        out_specs=[pl.BlockSpec((tm, d), row), pl.BlockSpec((tm, LANE), row)],
        out_shape=[jax.ShapeDtypeStruct((n, d), F32), jax.ShapeDtypeStruct((n, LANE), F32)],
        compiler_params=_cparams(("parallel",)),
        name="merge_router",
    )(x, omla, ocmp, oslc, owin, z['gn'], z['ga'], z['gb'], pw['w_o'], pw['w_on'], pw['w_out'],
      pw['e3'], pw['ln1_g'], pw['ln1_b'], pw['w_router'], pw['b_router'])


def _row_copy(src, dst, sem, src_row, dst_row):
    return pltpu.make_async_copy(src.at[pl.ds(src_row, 1)], dst.at[pl.ds(dst_row, 1)], sem)


def _moe_rank_kernel(rt_ref, tri_ref, rank_ref, cnt_ref, run_scr):
    @pl.when(pl.program_id(0) == 0)
    def _():
        run_scr[...] = jnp.zeros_like(run_scr)

    rt = rt_ref[...]
    lane = lax.broadcasted_iota(I32, rt.shape, 1)
    lanef = lane.astype(F32)
    e0 = jnp.sum(jnp.where(lane == 0, rt, 0.0), axis=1, keepdims=True)
    e1 = jnp.sum(jnp.where(lane == 1, rt, 0.0), axis=1, keepdims=True)
    hit0 = lanef == e0
    hit1 = lanef == e1
    onehot = jnp.where(hit0 | hit1, 1.0, 0.0)
    before = _dot(tri_ref[...], onehot.astype(BF16)) + run_scr[...]
    r0 = jnp.sum(jnp.where(hit0, before, 0.0), axis=1, keepdims=True)
    r1 = jnp.sum(jnp.where(hit1, before, 0.0), axis=1, keepdims=True)
    rank_ref[...] = jnp.where(lane == 0, r0, jnp.where(lane == 1, r1, 0.0))
    run = run_scr[...] + jnp.sum(onehot, axis=0, keepdims=True)
    run_scr[...] = run
    cnt_ref[...] = run


def _moe_rank(rt):
    n = rt.shape[0]
    tm = min(512, n)
    tri = jnp.asarray(np.tril(np.ones((tm, tm), np.float32), -1), BF16)
    return pl.pallas_call(
        _moe_rank_kernel,
        grid=(n // tm,),
        in_specs=[pl.BlockSpec((tm, LANE), lambda i: (i, 0)), pl.BlockSpec((tm, tm), lambda i: (0, 0))],
        out_specs=[pl.BlockSpec((tm, LANE), lambda i: (i, 0)), pl.BlockSpec((1, LANE), lambda i: (0, 0))],
        out_shape=[jax.ShapeDtypeStruct((n, LANE), F32), jax.ShapeDtypeStruct((1, LANE), F32)],
        scratch_shapes=[pltpu.VMEM((1, LANE), F32)],
        compiler_params=_cparams(("arbitrary",)),
        name="moe_rank",
    )(rt, tri)


def _moe_dispatch_kernel(dest_ref, h_ref, xs_in, xs_out, sem, *, tm):
    del xs_in

    def copies(t):
        return [_row_copy(h_ref, xs_out, sem, t, dest_ref[0, 0, TOP_K * t + k]) for k in range(TOP_K)]

    def start(t, c):
        for cp in copies(t):
            cp.start()
        return c

    def wait(t, c):
        for cp in copies(t):
            cp.wait()
        return c

    lax.fori_loop(0, tm, start, 0, unroll=DMA_UNROLL)
    lax.fori_loop(0, tm, wait, 0, unroll=DMA_UNROLL)


def _moe_dispatch(h, dest, n_slots):
    n, d = h.shape
    tm = min(256, n)
    return pl.pallas_call(
        functools.partial(_moe_dispatch_kernel, tm=tm),
        grid=(n // tm,),
        in_specs=[pl.BlockSpec((1, 1, TOP_K * tm), lambda i: (i, 0, 0), memory_space=pltpu.SMEM),
                  pl.BlockSpec((tm, d), lambda i: (i, 0)),
                  pl.BlockSpec(memory_space=pl.ANY)],
        out_specs=pl.BlockSpec(memory_space=pl.ANY),
        out_shape=jax.ShapeDtypeStruct((n_slots, d), F32),
        scratch_shapes=[pltpu.SemaphoreType.DMA(())],
        input_output_aliases={2: 0},
        compiler_params=_cparams(("arbitrary",)),
        name="moe_dispatch",
    )(dest.reshape(n // tm, 1, TOP_K * tm), h, jnp.zeros((n_slots, d), F32))


def _moe_expert_kernel(be_ref, x_ref, w1_ref, w3_ref, w2_ref, y_ref):
    del be_ref
    xb = x_ref[...].astype(BF16)
    a = _dot(xb, w1_ref[...].astype(BF16))
    hmid = a * _sigmoid(a) * _dot(xb, w3_ref[...].astype(BF16))
    y_ref[...] = _dot(hmid.astype(BF16), w2_ref[...].astype(BF16))


def _moe_experts(xs, blk_expert, w1, w3, w2):
    d = xs.shape[1]
    n_blocks = blk_expert.shape[0]
    de = w1.shape[-1]
    return pl.pallas_call(
        _moe_expert_kernel,
        grid_spec=pltpu.PrefetchScalarGridSpec(
            num_scalar_prefetch=1, grid=(n_blocks,),
            in_specs=[pl.BlockSpec((MOE_BLOCK, d), lambda i, be: (i, 0)),
                      pl.BlockSpec((None, None, d, de), lambda i, be: (0, be[i], 0, 0)),
                      pl.BlockSpec((None, None, d, de), lambda i, be: (0, be[i], 0, 0)),
                      pl.BlockSpec((None, None, de, d), lambda i, be: (0, be[i], 0, 0))],
            out_specs=pl.BlockSpec((MOE_BLOCK, d), lambda i, be: (i, 0))),
        out_shape=jax.ShapeDtypeStruct((n_blocks * MOE_BLOCK, d), F32),
        compiler_params=_cparams(("parallel",)),
        name="moe_experts",
    )(blk_expert, xs, w1, w3, w2)


def _moe_combine_kernel(dest_ref, nxt_ref, y_hbm, h_ref, rt_ref, g2_ref, b2_ref, o_ref, ybuf, sems, *, tm):
    i = pl.program_id(0)
    slot = i % 2

    def copies(ref, s, t):
        return [_row_copy(y_hbm, ybuf.at[s, k], sems.at[s], ref[0, 0, TOP_K * t + k], t) for k in range(TOP_K)]

    def start_all(ref, s):
        def body(t, c):
            for cp in copies(ref, s, t):
                cp.start()
            return c
        lax.fori_loop(0, tm, body, 0, unroll=DMA_UNROLL)

    @pl.when(i == 0)
    def _():
        start_all(dest_ref, 0)

    @pl.when(i + 1 < pl.num_programs(0))
    def _():
        start_all(nxt_ref, 1 - slot)

    def wait(t, c):
        for cp in copies(dest_ref, slot, t):
            cp.wait()
        return c

    lax.fori_loop(0, tm, wait, 0, unroll=DMA_UNROLL)
    y0 = ybuf[slot, 0]
    y1 = ybuf[slot, 1]
    rt = rt_ref[...]
    lane = lax.broadcasted_iota(I32, rt.shape, 1)
    g1 = jnp.sum(jnp.where(lane == 2, rt, 0.0), axis=1, keepdims=True)
    g2 = jnp.sum(jnp.where(lane == 3, rt, 0.0), axis=1, keepdims=True)
    moe = g1 * y0 + g2 * y1
    o_ref[...] = _layer_norm(DEEPNORM_ALPHA * h_ref[...] + moe, g2_ref[...], b2_ref[...])


def _moe_combine(y, dest, h, rt, ln_g, ln_b):
    n, d = h.shape
    tm = min(128, n)
    nt = n // tm
    row = lambda i: (i, 0)
    full = lambda i: (0, 0)
    dest3 = dest.reshape(nt, 1, TOP_K * tm)
    return pl.pallas_call(
        functools.partial(_moe_combine_kernel, tm=tm),
        grid=(nt,),
        in_specs=[pl.BlockSpec((1, 1, TOP_K * tm), lambda i: (i, 0, 0), memory_space=pltpu.SMEM),
                  pl.BlockSpec((1, 1, TOP_K * tm), lambda i: (jnp.minimum(i + 1, nt - 1), 0, 0),
                               memory_space=pltpu.SMEM),
                  pl.BlockSpec(memory_space=pl.ANY),
                  pl.BlockSpec((tm, d), row), pl.BlockSpec((tm, LANE), row),
                  pl.BlockSpec((1, d), full), pl.BlockSpec((1, d), full)],
        out_specs=pl.BlockSpec((tm, d), row),
        out_shape=jax.ShapeDtypeStruct((n, d), F32),
        scratch_shapes=[pltpu.VMEM((2, TOP_K, tm, d), F32), pltpu.SemaphoreType.DMA((2,))],
        compiler_params=_cparams(("arbitrary",)),
        name="moe_combine",
    )(dest3, dest3, y, h, rt, ln_g, ln_b)


def _moe(h, rt, pw, w1, w3, w2):
    n = h.shape[0]
    a = n * TOP_K
    rank, cnt = _moe_rank(rt)
    counts = cnt[0, :N_EXPERTS].astype(I32)
    padded = (counts + MOE_BLOCK - 1) // MOE_BLOCK * MOE_BLOCK
    pad_end = jnp.cumsum(padded)
    pad_start = pad_end - padded
    expert = rt[:, :TOP_K].astype(I32)
    first_slot = jnp.sum(jnp.where(expert[..., None] == jnp.arange(N_EXPERTS), pad_start, 0), axis=-1)
    dest = (first_slot + rank[:, :TOP_K].astype(I32)).reshape(-1)
    n_blocks = -(-a // MOE_BLOCK) + N_EXPERTS
    blk_start = jnp.arange(n_blocks) * MOE_BLOCK
    blk_expert = jnp.minimum(jnp.sum(pad_end[None, :] <= blk_start[:, None], axis=1), N_EXPERTS - 1).astype(I32)
    xs = _moe_dispatch(h, dest, n_blocks * MOE_BLOCK)
    y = _moe_experts(xs, blk_expert, w1, w3, w2)
    return _moe_combine(y, dest, h, rt, pw['ln2_g'], pw['ln2_b'])


def _qlat_kernel(q_ref, w_ref, o_ref):
    o_ref[...] = _dot(q_ref[...], w_ref[...]).astype(o_ref.dtype)


def _q_latent(qm, wabs):
    b = qm.shape[0]
    return pl.pallas_call(
        _qlat_kernel,
        grid=(MLA_HEADS,),
        in_specs=[pl.BlockSpec((b, LANE), lambda h: (0, h)),
                  pl.BlockSpec((None, LANE, MLA_KV_RANK), lambda h: (h, 0, 0))],
        out_specs=pl.BlockSpec((b, MLA_KV_RANK), lambda h: (0, h)),
        out_shape=jax.ShapeDtypeStruct((b, MLA_HEADS * MLA_KV_RANK), BF16),
        compiler_params=_cparams(("parallel",)),
        name="q_latent",
    )(qm, wabs)


def _mla_decode_kernel(pt_ref, ql_ref, qlt_ref, qr_ref, cn_ref, kn_ref, c_hbm, k_hbm, o_ref,
                       cbuf, kbuf, sems, m_scr, l_scr, acc_scr, *, n_pages, pp):
    slot = _prefetched_pages(pt_ref, (c_hbm, k_hbm), (cbuf, kbuf), sems, n_pages)
    qlt = qlt_ref[...]
    qr = qr_ref[...]
    cn = cn_ref[...].astype(BF16).astype(F32)
    kn = kn_ref[...].astype(BF16).astype(F32)
    m_scr[...] = (jnp.sum(ql_ref[...].astype(F32) * cn, axis=1, keepdims=True)
                  + jnp.sum(qr.astype(F32) * kn, axis=1, keepdims=True))
    l_scr[...] = jnp.ones_like(l_scr)
    acc_scr[...] = jnp.broadcast_to(cn, acc_scr.shape)

    def chunk(ci, carry):
        cs = [cbuf[slot, ci * pp + j].astype(BF16) for j in range(pp)]
        s = jnp.concatenate([_dot(c, qlt).T[:MLA_HEADS] + _dot(qr, kbuf[slot, ci * pp + j].astype(BF16))
                             for j, c in enumerate(cs)], axis=1)
        m_prev = m_scr[...]
        m_new = jnp.maximum(m_prev, jnp.max(s, axis=1, keepdims=True))
        alpha = jnp.exp2(m_prev - m_new)
        p = jnp.exp2(s - m_new)
        l_scr[...] = alpha * l_scr[...] + jnp.sum(p, axis=1, keepdims=True)
        pb = p.astype(BF16)
        acc = alpha * acc_scr[...]
        for j, c in enumerate(cs):
            acc = acc + _dot(pb[:, j * PAGE:(j + 1) * PAGE], c)
        acc_scr[...] = acc
        m_scr[...] = m_new
        return carry

    lax.fori_loop(0, n_pages // pp, chunk, 0)
    o_ref[...] = (acc_scr[...] / l_scr[...]).astype(o_ref.dtype)


def _mla_decode(qlat, qr, ckv_new, kr_new, cache_ckv, cache_kr_t, page_table):
    b, n_pages = page_table.shape
    pp = min(32, n_pages)
    r = MLA_KV_RANK
    ql3 = qlat.reshape(b, MLA_HEADS, r)
    qlt = jnp.pad(jnp.transpose(ql3, (0, 2, 1)), ((0, 0), (0, 0), (0, LANE - MLA_HEADS)))
    per_b = lambda bi, pt: (bi, 0, 0)
    return pl.pallas_call(
        functools.partial(_mla_decode_kernel, n_pages=n_pages, pp=pp),
        grid_spec=pltpu.PrefetchScalarGridSpec(
            num_scalar_prefetch=1, grid=(b,),
            in_specs=[pl.BlockSpec((None, MLA_HEADS, r), per_b),
                      pl.BlockSpec((None, r, LANE), per_b),
                      pl.BlockSpec((None, MLA_HEADS, MLA_ROPE), per_b),
                      pl.BlockSpec((None, 1, r), per_b),
                      pl.BlockSpec((None, 1, MLA_ROPE), per_b),
                      pl.BlockSpec(memory_space=pl.ANY),
                      pl.BlockSpec(memory_space=pl.ANY)],
            out_specs=pl.BlockSpec((None, MLA_HEADS, r), per_b),
            scratch_shapes=[pltpu.VMEM((2, n_pages, PAGE, r), F32), pltpu.VMEM((2, n_pages, MLA_ROPE, PAGE), F32),
                            pltpu.SemaphoreType.DMA((2,)),
                            pltpu.VMEM((MLA_HEADS, 1), F32), pltpu.VMEM((MLA_HEADS, 1), F32),
                            pltpu.VMEM((MLA_HEADS, r), F32)]),
        out_shape=jax.ShapeDtypeStruct((b, MLA_HEADS, r), BF16),
        compiler_params=_cparams(("arbitrary",)),
        name="mla_decode",
    )(page_table, ql3, qlt, qr, ckv_new.reshape(b, 1, r), kr_new.reshape(b, 1, MLA_ROPE),
      cache_ckv, cache_kr_t)


def _oproj_kernel(o_ref, w_ref, out_ref):
    out_ref[...] = (_dot(o_ref[:, :MLA_KV_RANK], w_ref[:, :LANE])
                    + _dot(o_ref[:, MLA_KV_RANK:], w_ref[:, LANE:])).astype(out_ref.dtype)


def _mla_out_up(o_lat, wuv):
    b = o_lat.shape[0]
    return pl.pallas_call(
        _oproj_kernel,
        grid=(MLA_HEADS // 2,),
        in_specs=[pl.BlockSpec((b, 2 * MLA_KV_RANK), lambda j: (0, j)),
                  pl.BlockSpec((MLA_KV_RANK, 2 * LANE), lambda j: (0, j))],
        out_specs=pl.BlockSpec((b, LANE), lambda j: (0, j)),
        out_shape=jax.ShapeDtypeStruct((b, 4 * LANE), BF16),
        compiler_params=_cparams(("parallel",)),
        name="mla_out_up",
    )(o_lat, wuv)


def _attend8(q8, variants, bias, ok):
    hrow = lax.broadcasted_iota(I32, (MLA_HEADS, 1), 0)
    vsel = (hrow >> GQA_SHIFT) * 2 + (hrow & 1)
    s = jnp.zeros(bias.shape, F32)
    for c, kv in enumerate(variants):
        s = s + jnp.where(vsel == c, _dot_nt(q8, kv), 0.0)
    p = _masked_softmax(s + bias, ok)
    pb = p.astype(BF16)
    o = jnp.zeros((MLA_HEADS, LANE), F32)
    for c, kv in enumerate(variants):
        o = o + jnp.where(vsel == c, _dot(pb, kv), 0.0)
    return o, p


def _slope_col():
    h = lax.broadcasted_iota(I32, (NSA_HEADS, 1), 0)
    return jnp.exp2(-(h + 1).astype(F32) * (8.0 / NSA_HEADS)) * LOG2E


def _attend_t(q8, kts, vts, bias, ok, new_row):
    hrow = lax.broadcasted_iota(I32, (NSA_HEADS, 1), 0)
    first = (hrow >> GQA_SHIFT) == 0
    s = jnp.where(first, _dot(q8, kts[0]), _dot(q8, kts[1])) + bias
    s = jnp.where(ok, s, NEG)
    new8 = jnp.broadcast_to(new_row, (NSA_HEADS, new_row.shape[1])).astype(BF16).astype(F32)
    kk, vv = new8[:, :LANE], new8[:, LANE:]
    k_new = jnp.where(first, kk, pltpu.roll(kk, HALF, 1))[:, :NSA_HD]
    v_new = jnp.where(first, vv, pltpu.roll(vv, HALF, 1))[:, :NSA_HD]
    s_new = jnp.sum(q8.astype(F32) * k_new, axis=1, keepdims=True)
    m = jnp.maximum(jnp.max(s, axis=1, keepdims=True), s_new)
    p = jnp.exp2(s - m)
    p_new = jnp.exp2(s_new - m)
    l = jnp.sum(p, axis=1, keepdims=True) + p_new
    pb = p.astype(BF16)
    o = jnp.where(first, _dot_nt(pb, vts[0]), _dot_nt(pb, vts[1])) + p_new * v_new
    return o / l


def _sample_cmp_win_kernel(qx_ref, q_ref, kc_ref, wmap_ref, st_ref, kvw_ref, newt_ref,
                           ocmp_ref, imp_ref, owin_ref, wout_ref, *, past_len):
    q8 = qx_ref[...]
    slope = _slope_col()
    nk = kc_ref.shape[0]
    kend = lax.broadcasted_iota(I32, (NSA_HEADS, nk), 1) * CMP_STRIDE + (CMP_BLOCK - 1)
    ok = kend <= past_len
    bias = -slope * (past_len - kend).astype(F32)
    kc = kc_ref[...]
    o, p = _attend8(q8, [kc[:, i * LANE:(i + 1) * LANE] for i in range(4)], bias, ok)
    ocmp_ref[...] = o.astype(ocmp_ref.dtype)
    hrow = lax.broadcasted_iota(I32, (NSA_HEADS, 1), 0)
    for kh in range(NSA_KV_HEADS):
        psum = jnp.sum(jnp.where((hrow >> GQA_SHIFT) == kh, p, 0.0), axis=0, keepdims=True)
        imp_ref[kh:kh + 1, :] = _split_dot(psum, wmap_ref[...])
    st = st_ref[...]
    wbuf = st.shape[1]
    stb = st.astype(BF16)
    j = lax.broadcasted_iota(I32, (NSA_HEADS, wbuf), 1)
    dist = wbuf - j
    okw = (dist <= WINDOW) & (past_len - dist >= 0)
    owin_ref[...] = _attend_t(q_ref[...], [stb[:NSA_HD], stb[NSA_HD:2 * NSA_HD]],
                              [stb[2 * NSA_HD:3 * NSA_HD], stb[3 * NSA_HD:]],
                              -slope * dist.astype(F32), okw, kvw_ref[...])
    newt = newt_ref[...]
    pick = lax.broadcasted_iota(I32, newt.shape, 1) == pl.program_id(0)
    new_col = jnp.sum(jnp.where(pick, newt, 0.0), axis=1, keepdims=True)
    nblk = wbuf // LANE
    rolled = [pltpu.roll(st[:, k * LANE:(k + 1) * LANE], LANE - 1, 1) for k in range(nblk)]
    keep = lax.broadcasted_iota(I32, (st.shape[0], LANE), 1) < LANE - 1
    for k in range(nblk):
        nxt = rolled[k + 1] if k + 1 < nblk else jnp.broadcast_to(new_col, rolled[k].shape)
        wout_ref[:, k * LANE:(k + 1) * LANE] = jnp.where(keep, rolled[k], nxt)


def _sample_cmp_win(qx, q8, kc_var, wmap, state_t, kvw, past_len):
    b = qx.shape[0]
    nk = kc_var.shape[0] // b
    wbuf = state_t.shape[2]
    wcols = wmap.shape[1]
    per_b3 = lambda bi: (bi, 0, 0)
    return pl.pallas_call(
        functools.partial(_sample_cmp_win_kernel, past_len=past_len),
        grid=(b,),
        in_specs=[pl.BlockSpec((None, NSA_HEADS, LANE), per_b3),
                  pl.BlockSpec((None, NSA_HEADS, NSA_HD), per_b3),
                  pl.BlockSpec((nk, 4 * LANE), lambda bi: (bi, 0)),
                  pl.BlockSpec(wmap.shape, lambda bi: (0, 0)),
                  pl.BlockSpec((None, NSA_KV_COLS, wbuf), per_b3),
                  pl.BlockSpec((None, 1, NSA_KV_COLS), per_b3),
                  pl.BlockSpec((NSA_KV_COLS, b), lambda bi: (0, 0))],
        out_specs=[pl.BlockSpec((None, NSA_HEADS, LANE), per_b3),
                   pl.BlockSpec((None, NSA_KV_HEADS, wcols), per_b3),
                   pl.BlockSpec((None, NSA_HEADS, NSA_HD), per_b3),
                   pl.BlockSpec((None, NSA_KV_COLS, wbuf), per_b3)],
        out_shape=[jax.ShapeDtypeStruct((b, NSA_HEADS, LANE), F32),
                   jax.ShapeDtypeStruct((b, NSA_KV_HEADS, wcols), F32),
                   jax.ShapeDtypeStruct((b, NSA_HEADS, NSA_HD), F32),
                   jax.ShapeDtypeStruct((b, NSA_KV_COLS, wbuf), F32)],
        compiler_params=_cparams(("parallel",)),
        name="sample_cmp_win",
    )(qx.reshape(b, NSA_HEADS, LANE), q8, kc_var, wmap, state_t, kvw.reshape(b, 1, NSA_KV_COLS),
      jnp.transpose(kvw))


def _sample_select_kernel(imp_ref, idx_ref, *, cur, n_slc):
    imp = imp_ref[...]
    rank, valid = _select_rank(imp, cur, n_slc)
    r, w = imp.shape
    blk = lax.broadcasted_iota(I32, (r, w), 1).astype(F32)
    lane = lax.broadcasted_iota(I32, (r, LANE), 1)
    out = jnp.zeros((r, LANE), F32)
    for t in range(SLC_TOP_N):
        pick = jnp.sum(jnp.where((rank == t) & valid, blk, 0.0), axis=1, keepdims=True)
        out = jnp.where(lane == t, pick, out)
    idx_ref[...] = out.astype(I32)


def _sample_select(imp, cur, n_slc):
    r = imp.shape[0]
    return pl.pallas_call(
        functools.partial(_sample_select_kernel, cur=cur, n_slc=n_slc),
        grid=(1,),
        in_specs=[pl.BlockSpec(imp.shape, lambda i: (0, 0))],
        out_specs=pl.BlockSpec((r, LANE), lambda i: (0, 0)),
        out_shape=jax.ShapeDtypeStruct((r, LANE), I32),
        compiler_params=_cparams(("arbitrary",)),
        name="sample_select",
    )(imp)


def _sample_slc_kernel(*refs, n_sel, past_len):
    blk_ref, page_ref = refs[0], refs[1]
    del page_ref
    q_ref, new_ref = refs[2], refs[3]
    c_refs = refs[4:4 + NSA_KV_HEADS * n_sel]
    o_ref = refs[4 + NSA_KV_HEADS * n_sel]
    bi = pl.program_id(0)
    hrow = lax.broadcasted_iota(I32, (NSA_HEADS, 1), 0)
    first = (hrow >> GQA_SHIFT) == 0
    nk = n_sel * PAGE
    col = lax.broadcasted_iota(I32, (NSA_HEADS, nk), 1)
    tile = col >> PAGE_SHIFT
    row = col & (PAGE - 1)
    kts, vts, blks = [], [], []
    for kh in range(NSA_KV_HEADS):
        mine = c_refs[kh * n_sel:(kh + 1) * n_sel]
        kts.append(jnp.concatenate([c[kh * NSA_HD:(kh + 1) * NSA_HD, :] for c in mine], axis=1).astype(BF16))
        vts.append(jnp.concatenate([c[(2 + kh) * NSA_HD:(3 + kh) * NSA_HD, :] for c in mine],
                                   axis=1).astype(BF16))
        blkv = jnp.zeros(col.shape, I32)
        for t in range(n_sel):
            blkv = jnp.where(tile == t, blk_ref[bi, kh * n_sel + t], blkv)
        blks.append(blkv)
    blkv = jnp.where(first, blks[0], blks[1])
    kpos = (blkv >> 1) * PAGE + row
    ok = ((row >> SLC_SHIFT) == (blkv & 1)) & (kpos < past_len)
    dist = past_len - kpos
    o_ref[...] = _attend_t(q_ref[...], kts, vts, -_slope_col() * dist.astype(F32), ok, new_ref[...])


def _sample_slc(q8, kvs_new, cache_slc_t, blk_idx, page_idx, past_len):
    b = q8.shape[0]
    n_sel = blk_idx.shape[1] // NSA_KV_HEADS
    per_b3 = lambda bi, blk, pg: (bi, 0, 0)
    in_specs = [pl.BlockSpec((None, NSA_HEADS, NSA_HD), per_b3),
                pl.BlockSpec((None, 1, NSA_KV_COLS), per_b3)]
    in_specs += [pl.BlockSpec((None, NSA_KV_COLS, PAGE),
                              functools.partial(lambda bi, blk, pg, j: (pg[bi, j], 0, 0), j=j))
                 for j in range(NSA_KV_HEADS * n_sel)]
    return pl.pallas_call(
        functools.partial(_sample_slc_kernel, n_sel=n_sel, past_len=past_len),
        grid_spec=pltpu.PrefetchScalarGridSpec(
            num_scalar_prefetch=2, grid=(b,), in_specs=in_specs,
            out_specs=pl.BlockSpec((None, NSA_HEADS, NSA_HD), per_b3)),
        out_shape=jax.ShapeDtypeStruct((b, NSA_HEADS, NSA_HD), F32),
        compiler_params=_cparams(("parallel",)),
        name="sample_slc",
    )(blk_idx, page_idx, q8, kvs_new.reshape(b, 1, NSA_KV_COLS), *([cache_slc_t] * (NSA_KV_HEADS * n_sel)))


def _heads_to_swapped(o8):
    b = o8.shape[0]
    return o8.reshape(b, NSA_HEADS // 2, 2, NSA_HD)[:, :, ::-1].reshape(b, NSA_HEADS * NSA_HD).astype(BF16)


def _pair_swap(o8):
    b = o8.shape[0]
    o = o8.reshape(b, NSA_HEADS // 2, 2, LANE)
    lo = jnp.arange(LANE) < HALF
    return jnp.where(lo, o[:, :, 1], o[:, :, 0]).reshape(b, 4 * LANE).astype(BF16)


def _rope_table(pos):
    half = MLA_ROPE // 2
    freqs = ROPE_THETA ** (-jnp.arange(half, dtype=F32) / half)
    ang = pos.astype(F32)[:, None] * freqs
    cos, sin = jnp.cos(ang), jnp.sin(ang)
    n = pos.shape[0]
    one, zero = jnp.ones((n, MLA_NOPE), F32), jnp.zeros((n, MLA_NOPE), F32)
    pad = jnp.zeros((n, LANE - MLA_NOPE - MLA_ROPE), F32)
    return jnp.concatenate([one, cos, cos, pad, zero, -sin, sin, pad], axis=1)


def _cmp_to_slc(n_cmp, n_slc, rows, cols):
    cs = np.arange(n_cmp)[:, None] * CMP_STRIDE
    ss = np.arange(n_slc)[None, :] * SLC_BLOCK
    inter = np.clip(np.minimum(cs + CMP_BLOCK, ss + SLC_BLOCK) - np.maximum(cs, ss), 0, None)
    w = np.zeros((rows, cols), np.float32)
    w[:n_cmp, :n_slc] = inter.astype(np.float32) / CMP_STRIDE
    return jnp.asarray(w, BF16)


def _prep_weights(w_in, mla_g_q, mla_g_kv, mla_w_uq, mla_w_uk, mla_w_uv, mla_w_o, nsa_pe_k, nsa_w1_k,
                  nsa_w2_k, nsa_pe_v, nsa_w1_v, nsa_w2_v, nsa_w_o, w_out, ln1_g, ln1_b, router_g_w,
                  router_g_b, router_e_w, router_e_b, ln2_g, ln2_b):
    d = w_in.shape[0]
    splits = (MLA_Q_RANK, MLA_KV_RANK, MLA_ROPE, NSA_HEADS * NSA_HD, NSA_KV_COLS, NSA_KV_COLS, NSA_KV_COLS,
              3 * NSA_HEADS, d, d)
    offs = np.cumsum(splits)[:-1].tolist()
    cq, ckv, kr, q, kvc, kvs, kvw, gn, ga, gb = jnp.split(w_in, offs, axis=1)
    r2 = MLA_ROPE // 2
    zc = lambda n: jnp.zeros((d, n), F32)
    tail = LANE - MLA_NOPE - MLA_ROPE
    kr_pad = jnp.concatenate([zc(MLA_NOPE), kr, zc(tail)], axis=1)
    kr_sw = jnp.concatenate([zc(MLA_NOPE), kr[:, r2:], kr[:, :r2], zc(tail)], axis=1)
    gn_pad = jnp.concatenate([gn, zc(LANE - gn.shape[1])], axis=1)
    w_all = jnp.concatenate([cq, ckv, q, kvc, kvs, kvw, kr_pad, kr_sw, gn_pad, ga, gb], axis=1).astype(BF16)

    def pad_heads(w, lo_cols):
        r, hh, c = w.shape
        out = jnp.zeros((r, hh, LANE), F32).at[:, :, lo_cols:lo_cols + c].set(w)
        return out

    nope, rope = mla_w_uq[:, :, :MLA_NOPE], mla_w_uq[:, :, MLA_NOPE:]
    wuq = jnp.concatenate([nope, rope, jnp.zeros(nope.shape[:2] + (tail,), F32)], axis=2)
    wuqs = jnp.concatenate([jnp.zeros_like(nope), rope[:, :, r2:], rope[:, :, :r2],
                            jnp.zeros(nope.shape[:2] + (tail,), F32)], axis=2)
    wuk = pad_heads(mla_w_uk, 0)
    odd = (jnp.arange(MLA_HEADS) % 2 == 1)[None, :, None]
    wuv = jnp.where(odd, pad_heads(mla_w_uv, HALF), pad_heads(mla_w_uv, 0))
    flat = lambda w: w.reshape(w.shape[0], -1).astype(BF16)
    wabs = jnp.zeros((MLA_HEADS, LANE, MLA_KV_RANK), F32).at[:, :MLA_NOPE, :].set(
        jnp.transpose(mla_w_uk, (1, 2, 0))).astype(BF16)

    def w1_cols(w1):
        return jnp.transpose(w1.reshape(2, CMP_STRIDE, NSA_HD, CMP_HIDDEN), (1, 2, 0, 3))

    wt = jnp.stack([w1_cols(nsa_w1_k), w1_cols(nsa_w1_v)])
    eye2 = jnp.eye(2, dtype=F32)
    w_big = jnp.einsum('tpdfh,tu,kv->ptkdfuvh', wt, eye2, eye2).reshape(
        CMP_STRIDE, NSA_KV_COLS, 2 * 2 * NSA_KV_HEADS * CMP_HIDDEN).astype(BF16)
    zpe = jnp.zeros((CMP_BLOCK * NSA_HD, CMP_HIDDEN), F32)
    w1_pe = jnp.concatenate([
        jnp.concatenate([nsa_w1_k, nsa_w1_k, zpe, zpe], axis=1),
        jnp.concatenate([zpe, zpe, nsa_w1_v, nsa_w1_v], axis=1)], axis=0)
    pe_flat = jnp.concatenate([nsa_pe_k.reshape(1, -1), nsa_pe_v.reshape(1, -1)], axis=1)
    w1_hi = w1_pe.astype(BF16)
    w1_lo = (w1_pe - w1_hi.astype(F32)).astype(BF16)
    z64 = jnp.zeros((CMP_HIDDEN, NSA_HD), F32)
    rows = [[nsa_w2_k, z64, z64, z64], [z64, nsa_w2_k, z64, z64], [z64, z64, nsa_w2_v, z64],
            [z64, z64, z64, nsa_w2_v]]
    w2_cmp = jnp.concatenate([jnp.concatenate(r, axis=1) for r in rows], axis=0).astype(BF16)

    perm = np.arange(NSA_HEADS * NSA_HD).reshape(NSA_HEADS // 2, 2, NSA_HD)[:, ::-1].reshape(-1)
    w_on = nsa_w_o[perm].astype(BF16)
    e3 = np.zeros((LANE, 3, NSA_HEADS * NSA_HD), np.float32)
    for h in range(NSA_HEADS):
        pos = (h // 2) * LANE + (0 if h % 2 else HALF)
        for j in range(3):
            e3[3 * h + j, j, pos:pos + NSA_HD] = 1.0
    e3 = jnp.asarray(e3.reshape(LANE, -1), BF16)
    w_r = jnp.concatenate([router_g_w, zc(HALF - N_GROUPS), router_e_w], axis=1)
    w_r_hi = w_r.astype(BF16)
    w_r_lo = (w_r - w_r_hi.astype(F32)).astype(BF16)
    b_r = jnp.concatenate([router_g_b, jnp.zeros((HALF - N_GROUPS,), F32), router_e_b])[None, :]
    return dict(
        w_all=w_all, wuq=flat(wuq), wuqs=flat(wuqs), wuk=flat(wuk), wuv=flat(wuv),
        gq=mla_g_q[None, :], gkv=mla_g_kv[None, :], wabs=wabs, w_big=w_big,
        pe_flat=pe_flat, w1_pe=jnp.stack([w1_hi, w1_lo]), w2_cmp=w2_cmp,
        w_o=mla_w_o.astype(BF16), w_on=w_on, w_out=w_out.astype(BF16), e3=e3,
        ln1_g=ln1_g[None, :], ln1_b=ln1_b[None, :], w_router=jnp.stack([w_r_hi, w_r_lo]), b_router=b_r,
        ln2_g=ln2_g[None, :], ln2_b=ln2_b[None, :])


def _prompt_layer(x, pw, pe, slopes, moe_w):
    b, s, d = x.shape
    n = b * s
    tm = min(256, s)
    z = _in_proj(x.reshape(n, d), _rope_table(jnp.arange(s)), pw, tm, s)
    o_mla = _flash(z['qm'], z['km'], z['vm'], slopes, b, s, mode='causal', kv_shared=False, alibi=False,
                   swap=False)
    nb = s // CMP_STRIDE
    fs = _cmp_fs_dense(z['kvc'], pw['w_big'])
    kc_var, kc_var_t = _cmp_finish(fs, nb, pe, pw)
    n_slc = -(-s // SLC_BLOCK)
    wmap_t = jnp.transpose(_cmp_to_slc(nb - 1, n_slc, nb, LANE))
    o_cmp, sel = _cmp_select(z['qn'], kc_var, kc_var_t, slopes, wmap_t, b, s, n_slc)
    emat_t = jnp.asarray((np.arange(s)[:, None] // SLC_BLOCK) == np.arange(LANE)[None, :], BF16)
    o_slc = _flash(z['qn'], z['kvsv'], z['kvsvt'], slopes, b, s, mode='select', kv_shared=True, alibi=True,
                   swap=True, selt=sel, emat_t=emat_t)
    o_win = _flash(z['qn'], z['kvwv'], z['kvwvt'], slopes, b, s, mode='window', kv_shared=True, alibi=True,
                   swap=True)
    h, rt = _merge(x.reshape(n, d), o_mla, o_cmp, o_slc, o_win, z, pw, min(512, s))
    y = _moe(h, rt, pw, *moe_w)
    return y.reshape(b, s, d), z


def _sample_layer(x, pw, pe, slopes, moe_w, cache_ckv, cache_kr_t, cache_cmp_t, cache_slc_t, state_t, page_table):
    b, t, d = x.shape
    n_pages = page_table.shape[1]
    past_len = n_pages * PAGE
    z = _in_proj(x.reshape(b, d), _rope_table(jnp.full((b,), past_len)), pw, b, b)
    qlat = _q_latent(z['qm'], pw['wabs'])
    qr = z['qm'].reshape(b, MLA_HEADS, LANE)[:, :, MLA_NOPE:MLA_NOPE + MLA_ROPE]
    o_lat = _mla_decode(qlat, qr, z['ckv'], z['kr'], cache_ckv, cache_kr_t, page_table)
    o_mla = _mla_out_up(o_lat.reshape(b, MLA_HEADS * MLA_KV_RANK), pw['wuv'])
    nb = past_len // CMP_STRIDE
    fs = _cmp_fs_paged(cache_cmp_t, page_table, pw['w_big'])
    kc_var, _ = _cmp_finish(fs, nb, pe, pw)
    n_slc = -(-(past_len + t) // SLC_BLOCK)
    assert n_slc >= SLC_TOP_N and past_len % SLC_BLOCK == 0
    wcols = -(-n_slc // LANE) * LANE
    wmap = _cmp_to_slc(nb - 1, n_slc, nb, wcols)
    q8 = z['qnat'].reshape(b, NSA_HEADS, NSA_HD)
    o_cmp8, imp, o_win8, win_out = _sample_cmp_win(z['qn'], q8, kc_var, wmap, state_t, z['kvw'], past_len)
    cur = past_len // SLC_BLOCK
    idx = _sample_select(imp.reshape(b * NSA_KV_HEADS, wcols), cur, n_slc)[:, :SLC_TOP_N]
    blk = idx.reshape(b, NSA_KV_HEADS * SLC_TOP_N)
    safe = jnp.minimum(blk, past_len // SLC_BLOCK - 1)
    pages = jnp.take_along_axis(page_table, safe // (PAGE // SLC_BLOCK), axis=1).astype(I32)
    o_slc8 = _sample_slc(q8, z['kvs'], cache_slc_t, blk, pages, past_len)
    h, rt = _merge(x.reshape(b, d), o_mla, _pair_swap(o_cmp8), _heads_to_swapped(o_slc8),
                   _heads_to_swapped(o_win8), z, pw, b)
    y = _moe(h, rt, pw, *moe_w)
    return y.reshape(b, t, d), z, win_out


def kernel(x_prompt, x_sample, cache_mla_ckv, cache_mla_krope, cache_nsa_cmp_kv, cache_nsa_slc_kv,
           state_nsa_win_kv, page_table, w_in, mla_g_q, mla_g_kv, mla_w_uq, mla_w_uk, mla_w_uv, mla_w_o,
           nsa_pe_k, nsa_w1_k, nsa_w2_k, nsa_pe_v, nsa_w1_v, nsa_w2_v, nsa_w_o, w_out, ln1_g, ln1_b,
           router_g_w, router_g_b, router_e_w, router_e_b, moe_w1, moe_w3, moe_w2, ln2_g, ln2_b):
    assert w_in.shape[0] == DEPTH and x_sample.shape[1] == 1
    b, s, d = x_prompt.shape
    bs = x_sample.shape[0]
    kv_shape = (2, NSA_KV_HEADS, NSA_HD)
    pw = _prep_weights(w_in[0], mla_g_q[0], mla_g_kv[0], mla_w_uq[0], mla_w_uk[0], mla_w_uv[0], mla_w_o[0],
                       nsa_pe_k[0], nsa_w1_k[0], nsa_w2_k[0], nsa_pe_v[0], nsa_w1_v[0], nsa_w2_v[0],
                       nsa_w_o[0], w_out[0], ln1_g[0], ln1_b[0], router_g_w[0], router_g_b[0],
                       router_e_w[0], router_e_b[0], ln2_g[0], ln2_b[0])
    slopes = 2.0 ** (-8.0 * jnp.arange(1, NSA_HEADS + 1, dtype=F32) / NSA_HEADS)
    moe_w = (moe_w1, moe_w3, moe_w2)
    pe = _cmp_pe(pw)
    y_p, zp = _prompt_layer(x_prompt, pw, pe, slopes, moe_w)
    def rows_last(c):
        return jnp.transpose(c, (0, 2, 3, 4, 1)).reshape(c.shape[0], NSA_KV_COLS, c.shape[1])

    y_s, zs, win_t = _sample_layer(
        x_sample, pw, pe, slopes, moe_w, cache_mla_ckv[0], jnp.transpose(cache_mla_krope[0], (0, 2, 1)),
        rows_last(cache_nsa_cmp_kv[0]), rows_last(cache_nsa_slc_kv[0]), rows_last(state_nsa_win_kv[0]),
        page_table)
    def rows_first(t):
        return jnp.transpose(t.reshape((t.shape[0],) + kv_shape + (-1,)), (0, 4, 1, 2, 3))[None]

    wp = min(WINDOW, s)
    return (y_p, y_s,
            zp['ckv'].reshape(1, b, s, MLA_KV_RANK), zs['ckv'].reshape(1, bs, 1, MLA_KV_RANK),
            zp['kr'].reshape(1, b, s, MLA_ROPE), zs['kr'].reshape(1, bs, 1, MLA_ROPE),
            rows_first(zp['kvct']), zs['kvc'].reshape((1, bs, 1) + kv_shape),
            rows_first(zp['kvst']), zs['kvs'].reshape((1, bs, 1) + kv_shape),
            rows_first(zp['kvwt'][:, :, s - wp:]), rows_first(win_t))
```

```python
import functools

import numpy as np
import jax
import jax.numpy as jnp
from jax import lax
from jax.experimental import pallas as pl
from jax.experimental.pallas import tpu as pltpu

F32 = jnp.float32
BF16 = jnp.bfloat16
I32 = jnp.int32

PAGE = 128
MLA_HEADS = 8
MLA_NOPE = 64
MLA_ROPE = 32
MLA_V = 64
MLA_Q_RANK = 256
MLA_KV_RANK = 256
ROPE_THETA = 10000.0
NSA_HEADS = 8
NSA_KV_HEADS = 2
NSA_GQA = NSA_HEADS // NSA_KV_HEADS
NSA_HD = 64
CMP_BLOCK = 32
CMP_STRIDE = 16
CMP_HIDDEN = 64
SLC_BLOCK = 64
SLC_TOP_N = 16
N_LOCAL_SLC = 2
WINDOW = 512
FORCE_SCORE = 1.0e4
N_GROUPS = 8
EXPERTS_PER_GROUP = 8
N_EXPERTS = N_GROUPS * EXPERTS_PER_GROUP
TOP_K = 2
D_EXPERT = 256
MOE_BLOCK = 256
LN_EPS = 1e-5
RMS_EPS = 1e-6
DEPTH = 1
DEEPNORM_ALPHA = (2.0 * DEPTH) ** 0.25
NSA_KV_COLS = 2 * NSA_KV_HEADS * NSA_HD

PAGE_SHIFT = PAGE.bit_length() - 1
SLC_SHIFT = SLC_BLOCK.bit_length() - 1
GQA_SHIFT = NSA_GQA.bit_length() - 1
LANE = 128
HALF = LANE // 2
VMEM_LIMIT = 56 * 1024 * 1024
NEG = -1e30
DMA_UNROLL = 8
LOG2E = 1.4426950408889634

_C_CQ, _C_CKV, _C_Q, _C_KVC, _C_KVS, _C_KVW, _C_KRP, _C_KRS, _C_GN, _C_GA = (
    0, 256, 512, 1024, 1280, 1536, 1792, 1920, 2048, 2176)


def _cparams(sem):
    return pltpu.CompilerParams(dimension_semantics=sem, vmem_limit_bytes=VMEM_LIMIT)


def _dot(a, b):
    return jnp.dot(a, b, preferred_element_type=F32)


def _dot_nt(a, b):
    return lax.dot_general(a, b, (((1,), (1,)), ((), ())), preferred_element_type=F32)


def _split_dot(a, w_hi, w_lo=None):
    a_hi = a.astype(BF16)
    a_lo = (a - a_hi.astype(F32)).astype(BF16)
    r = _dot(a_hi, w_hi) + _dot(a_lo, w_hi)
    if w_lo is not None:
        r = r + _dot(a_hi, w_lo)
    return r


def _sigmoid(x):
    return 1.0 / (1.0 + jnp.exp(-x))


def _kv_variants(x0, x1):
    lo = lax.broadcasted_iota(I32, x0.shape, 1) < HALF
    r0 = pltpu.roll(x0, HALF, 1)
    r1 = pltpu.roll(x1, HALF, 1)
    return (jnp.where(lo, x0, r1), jnp.where(lo, x1, r0), jnp.where(lo, r0, x1), jnp.where(lo, r1, x0))


def _inproj_kernel(x_ref, cs_ref, w_ref, wuq_ref, wuqs_ref, wuk_ref, wuv_ref, gq_ref, gkv_ref,
                   qm_ref, km_ref, vm_ref, ckv_ref, kr_ref, qn_ref, kvc_ref, kvs_ref, kvw_ref,
                   kvsv_ref, kvwv_ref, gn_ref, ga_ref, gb_ref, qnat_ref, kvsvt_ref, kvwvt_ref,
                   kvct_ref, kvst_ref, kvwt_ref, *, mla_scale):
    xb = x_ref[...].astype(BF16)

    def proj(a, b):
        return _dot(xb, w_ref[:, a:b])

    cos = cs_ref[:, :LANE]
    sin = cs_ref[:, LANE:]

    def rms(z, g):
        return z * lax.rsqrt(jnp.mean(z * z, axis=-1, keepdims=True) + RMS_EPS) * g

    cq = rms(proj(_C_CQ, _C_CKV), gq_ref[...]).astype(BF16)
    q = _dot(cq, wuq_ref[...])
    qs = _dot(cq, wuqs_ref[...])
    ckv = rms(proj(_C_CKV, _C_Q), gkv_ref[...])
    ckv_ref[...] = ckv
    ckvb = ckv.astype(BF16)
    kr = proj(_C_KRP, _C_KRS) * cos + proj(_C_KRS, _C_GN) * sin
    kr_ref[...] = kr[:, MLA_NOPE:MLA_NOPE + MLA_ROPE]
    kn = _dot(ckvb, wuk_ref[...])
    v = _dot(ckvb, wuv_ref[...])
    for h in range(MLA_HEADS):
        sl = slice(h * LANE, (h + 1) * LANE)
        qm_ref[:, sl] = ((q[:, sl] * cos + qs[:, sl] * sin) * mla_scale).astype(BF16)
        km_ref[:, sl] = (kn[:, sl] + kr).astype(BF16)
        vm_ref[sl, :] = v[:, sl].T.astype(BF16)
    qn = proj(_C_Q, _C_KVC) * (NSA_HD ** -0.5 * LOG2E)
    qnat_ref[...] = qn.astype(BF16)
    lo = lax.broadcasted_iota(I32, (qn.shape[0], LANE), 1) < HALF
    for j in range(NSA_HEADS // 2):
        blk = qn[:, j * LANE:(j + 1) * LANE]
        qn_ref[:, (2 * j) * LANE:(2 * j + 1) * LANE] = jnp.where(lo, blk, 0.0).astype(BF16)
        qn_ref[:, (2 * j + 1) * LANE:(2 * j + 2) * LANE] = jnp.where(lo, 0.0, blk).astype(BF16)
    zc = proj(_C_KVC, _C_KVS)
    kvc_ref[...] = zc
    kvct_ref[...] = zc.T
    for src, dst, dst_t, var, vart in ((_C_KVS, kvs_ref, kvst_ref, kvsv_ref, kvsvt_ref),
                                       (_C_KVW, kvw_ref, kvwt_ref, kvwv_ref, kvwvt_ref)):
        z = proj(src, src + NSA_KV_COLS)
        dst[...] = z
        dst_t[...] = z.T
        for i, t in enumerate(_kv_variants(z[:, :LANE], z[:, LANE:])):
            var[:, i * LANE:(i + 1) * LANE] = t.astype(BF16)
            vart[i * LANE:(i + 1) * LANE, :] = t.T.astype(BF16)
    gn_ref[...] = _sigmoid(proj(_C_GN, _C_GA))
    d = ga_ref.shape[1]
    ga_ref[...] = _sigmoid(proj(_C_GA, _C_GA + d)).astype(BF16)
    gb_ref[...] = _sigmoid(proj(_C_GA + d, _C_GA + 2 * d)).astype(BF16)


def _in_proj(x, cs, pw, tm, seq):
    n, d = x.shape
    per_seq = seq // tm
    n_cs = cs.shape[0] // tm
    row = lambda i: (i, 0)
    full = lambda i: (0, 0)
    wcols = pw['w_all'].shape[1]
    outs = [
        ('qm', 8 * LANE, BF16), ('km', 8 * LANE, BF16), ('vm', 8 * LANE, BF16),
        ('ckv', MLA_KV_RANK, F32), ('kr', MLA_ROPE, F32), ('qn', 8 * LANE, BF16),
        ('kvc', NSA_KV_COLS, F32), ('kvs', NSA_KV_COLS, F32), ('kvw', NSA_KV_COLS, F32),
        ('kvsv', 4 * LANE, BF16), ('kvwv', 4 * LANE, BF16), ('gn', LANE, F32),
        ('ga', d, BF16), ('gb', d, BF16), ('qnat', NSA_HEADS * NSA_HD, BF16),
        ('kvsvt', 4 * LANE, BF16), ('kvwvt', 4 * LANE, BF16),
        ('kvct', NSA_KV_COLS, F32), ('kvst', NSA_KV_COLS, F32), ('kvwt', NSA_KV_COLS, F32)]
    transposed = ('vm', 'kvsvt', 'kvwvt')
    by_seq = ('kvct', 'kvst', 'kvwt')

    def out_spec(k, c):
        if k in transposed:
            return pl.BlockSpec((c, tm), lambda i: (0, i))
        if k in by_seq:
            return pl.BlockSpec((None, c, tm), lambda i: (i // per_seq, 0, i % per_seq))
        return pl.BlockSpec((tm, c), row)

    def out_shape(k, c, t):
        shape = (c, n) if k in transposed else (n // seq, c, seq) if k in by_seq else (n, c)
        return jax.ShapeDtypeStruct(shape, t)

    res = pl.pallas_call(
        functools.partial(_inproj_kernel, mla_scale=(MLA_NOPE + MLA_ROPE) ** -0.5 * LOG2E),
        grid=(n // tm,),
        in_specs=[pl.BlockSpec((tm, d), row),
                  pl.BlockSpec((tm, 2 * LANE), lambda i: (i % n_cs, 0)),
                  pl.BlockSpec((d, wcols), full),
                  pl.BlockSpec((MLA_Q_RANK, 8 * LANE), full),
                  pl.BlockSpec((MLA_Q_RANK, 8 * LANE), full),
                  pl.BlockSpec((MLA_KV_RANK, 8 * LANE), full),
                  pl.BlockSpec((MLA_KV_RANK, 8 * LANE), full),
                  pl.BlockSpec((1, MLA_Q_RANK), full),
                  pl.BlockSpec((1, MLA_KV_RANK), full)],
        out_specs=[out_spec(k, c) for k, c, _ in outs],
        out_shape=[out_shape(k, c, t) for k, c, t in outs],
        compiler_params=_cparams(("parallel",)),
        name="in_proj",
    )(x, cs, pw['w_all'], pw['wuq'], pw['wuqs'], pw['wuk'], pw['wuv'], pw['gq'], pw['gkv'])
    return {k: v for (k, _, _), v in zip(outs, res)}


def _flash_kernel(*refs, mode, t, alibi, swap):
    if mode == 'select':
        slopes_ref, q_ref, k_ref, vt_ref, selt_ref, et_ref, o_ref, m_scr, l_scr, acc_scr = refs
    else:
        slopes_ref, q_ref, k_ref, vt_ref, o_ref, m_scr, l_scr, acc_scr = refs
    hp = pl.program_id(1)
    qi = pl.program_id(2)
    q0 = qi * t
    krow = lax.broadcasted_iota(I32, (t, t), 0)
    dmat = lax.broadcasted_iota(I32, (t, t), 1) - krow
    krow_f = krow.astype(F32)
    qs = [q_ref[:, e * LANE:(e + 1) * LANE] for e in range(2)]
    for e in range(2):
        m_scr[e] = jnp.full((1, t), NEG, F32)
        l_scr[e] = jnp.zeros((1, t), F32)
        acc_scr[e] = jnp.zeros((LANE, t), F32)

    def step(kt, edge):
        k0 = pl.multiple_of(kt * t, t)
        ok = None
        if edge:
            dist = dmat + (q0 - k0)
            ok = dist >= 0
            if mode == 'window':
                ok = ok & (dist <= WINDOW)
        if mode == 'select':
            picked = _dot(et_ref[pl.ds(k0, t), :], selt_ref[...]) > 0.5
            ok = picked if ok is None else ok & picked
        for e in range(2):
            k = k_ref[pl.ds(k0, t), e * LANE:(e + 1) * LANE]
            vt = vt_ref[e * LANE:(e + 1) * LANE, pl.ds(k0, t)]
            shift = 0.0
            if alibi:
                slope = slopes_ref[2 * hp + e] * LOG2E
                shift = slope * (k0 - q0).astype(F32)
            s = _dot_nt(k, qs[e])
            if alibi:
                s = s + slope * krow_f
            if ok is not None:
                s = jnp.where(ok, s, NEG)
            m_prev = m_scr[e]
            m_new = jnp.maximum(m_prev, jnp.max(s, axis=0, keepdims=True) + shift)
            alpha = jnp.exp2(m_prev - m_new)
            p = jnp.exp2(s - (m_new - shift))
            l_scr[e] = alpha * l_scr[e] + jnp.sum(p, axis=0, keepdims=True)
            acc_scr[e] = alpha * acc_scr[e] + _dot(vt, p.astype(BF16))
            m_scr[e] = m_new

    def loop(lo, hi, edge):
        n = hi - lo

        def pair(j, c):
            step(lo + 2 * j, edge)
            step(lo + 2 * j + 1, edge)
            return c

        lax.fori_loop(0, n >> 1, pair, 0)

        @pl.when((n & 1) == 1)
        def _():
            step(hi - 1, edge)

    if mode == 'window':
        loop(jnp.maximum(q0 - WINDOW, 0) // t, qi + 1, True)
    else:
        loop(0, qi, False)
        step(qi, True)
    o0 = acc_scr[0] / l_scr[0]
    o1 = acc_scr[1] / l_scr[1]
    lo = lax.broadcasted_iota(I32, (LANE, t), 0) < HALF
    o_ref[...] = (jnp.where(lo, o1, o0) if swap else jnp.where(lo, o0, o1)).T.astype(o_ref.dtype)


def _flash(q, k, vt, slopes, b, s, *, mode, kv_shared, alibi, swap, selt=None, emat_t=None):
    tq = min(512, s)
    nq = s // tq
    k_idx = (lambda bi, hp, qi: (bi, hp // 2)) if kv_shared else (lambda bi, hp, qi: (bi, hp))
    vt_idx = (lambda bi, hp, qi: (hp // 2, bi)) if kv_shared else (lambda bi, hp, qi: (hp, bi))
    in_specs = [pl.BlockSpec(memory_space=pltpu.SMEM),
                pl.BlockSpec((tq, 2 * LANE), lambda bi, hp, qi: (bi * nq + qi, hp)),
                pl.BlockSpec((s, 2 * LANE), k_idx),
                pl.BlockSpec((2 * LANE, s), vt_idx)]
    args = [slopes, q, k, vt]
    if mode == 'select':
        in_specs += [pl.BlockSpec((None, None, LANE, tq), lambda bi, hp, qi: (bi, hp // 2, 0, qi)),
                     pl.BlockSpec((s, LANE), lambda bi, hp, qi: (0, 0))]
        args += [selt, emat_t]
    return pl.pallas_call(
        functools.partial(_flash_kernel, mode=mode, t=tq, alibi=alibi, swap=swap),
        grid=(b, 4, nq),
        in_specs=in_specs,
        out_specs=pl.BlockSpec((tq, LANE), lambda bi, hp, qi: (bi * nq + qi, hp)),
        out_shape=jax.ShapeDtypeStruct((b * s, 4 * LANE), BF16),
        scratch_shapes=[pltpu.VMEM((2, 1, tq), F32), pltpu.VMEM((2, 1, tq), F32),
                        pltpu.VMEM((2, LANE, tq), F32)],
        compiler_params=_cparams(("parallel", "parallel", "arbitrary")),
        name="flash_" + mode,
    )(*args)


def _cmp_first_linear(lo, hi, w_ref, o_ref, r0=0, r=None):
    r = o_ref.shape[0] if r is None else r
    acc = jnp.zeros((r, o_ref.shape[1]), F32)
    for p in range(CMP_STRIDE):
        rows = pl.ds(r0 * CMP_STRIDE + p, r, stride=CMP_STRIDE)
        x = jnp.concatenate([lo[rows, :], hi[rows, :]], axis=1)
        acc = acc + _dot(x.astype(BF16), w_ref[p])
    o_ref[r0:r0 + r, :] = acc


def _cmp_fs_kernel(lo, hi, w_ref, o_ref):
    _cmp_first_linear(lo, hi, w_ref, o_ref)


def _cmp_fs_dense(x, w_big):
    n, c = x.shape
    tr = min(256, n // CMP_STRIDE)
    return pl.pallas_call(
        _cmp_fs_kernel,
        grid=(n // CMP_STRIDE // tr,),
        in_specs=[pl.BlockSpec((tr * CMP_STRIDE, LANE), lambda i: (i, 0)),
                  pl.BlockSpec((tr * CMP_STRIDE, LANE), lambda i: (i, 1)),
                  pl.BlockSpec(w_big.shape, lambda i: (0, 0, 0))],
        out_specs=pl.BlockSpec((tr, 4 * LANE), lambda i: (i, 0)),
        out_shape=jax.ShapeDtypeStruct((n // CMP_STRIDE, 4 * LANE), F32),
        compiler_params=_cparams(("parallel",)),
        name="cmp_fs",
    )(x, x, w_big)


def _prefetched_pages(pt_ref, srcs, bufs, sems, n_pages):
    bi = pl.program_id(0)
    slot = bi % 2

    def copies(seq, s, j):
        pg = pt_ref[seq, j]
        return [pltpu.make_async_copy(src.at[pg], buf.at[s, j], sems.at[s]) for src, buf in zip(srcs, bufs)]

    def start_all(seq, s):
        def body(j, c):
            for cp in copies(seq, s, j):
                cp.start()
            return c
        lax.fori_loop(0, n_pages, body, 0, unroll=DMA_UNROLL)

    @pl.when(bi == 0)
    def _():
        start_all(0, 0)

    @pl.when(bi + 1 < pl.num_programs(0))
    def _():
        start_all(bi + 1, 1 - slot)

    def wait(j, c):
        for cp in copies(bi, slot, j):
            cp.wait()
        return c

    lax.fori_loop(0, n_pages, wait, 0, unroll=DMA_UNROLL)
    return slot


def _cmp_fs_paged_kernel(pt_ref, cache_hbm, w_ref, o_ref, pbuf, sems, lo, hi, *, n_pages):
    slot = _prefetched_pages(pt_ref, (cache_hbm,), (pbuf,), sems, n_pages)

    groups = 4 if n_pages % 4 == 0 else 1
    per = n_pages // groups
    for g in range(groups):
        for j in range(g * per, (g + 1) * per):
            lo[j * PAGE:(j + 1) * PAGE, :] = pbuf[slot, j, :LANE, :].T
            hi[j * PAGE:(j + 1) * PAGE, :] = pbuf[slot, j, LANE:, :].T
        blocks = per * (PAGE // CMP_STRIDE)
        _cmp_first_linear(lo, hi, w_ref, o_ref, g * blocks, blocks)


def _cmp_fs_paged(cache_t, page_table, w_big):
    b, n_pages = page_table.shape
    rows = n_pages * (PAGE // CMP_STRIDE)
    return pl.pallas_call(
        functools.partial(_cmp_fs_paged_kernel, n_pages=n_pages),
        grid_spec=pltpu.PrefetchScalarGridSpec(
            num_scalar_prefetch=1, grid=(b,),
            in_specs=[pl.BlockSpec(memory_space=pl.ANY),
                      pl.BlockSpec(w_big.shape, lambda bi, pt: (0, 0, 0))],
            out_specs=pl.BlockSpec((rows, 4 * LANE), lambda bi, pt: (bi, 0)),
            scratch_shapes=[pltpu.VMEM((2, n_pages, NSA_KV_COLS, PAGE), F32), pltpu.SemaphoreType.DMA((2,)),
                            pltpu.VMEM((n_pages * PAGE, LANE), F32), pltpu.VMEM((n_pages * PAGE, LANE), F32)]),
        out_shape=jax.ShapeDtypeStruct((b * rows, 4 * LANE), F32),
        compiler_params=_cparams(("arbitrary",)),
        name="cmp_fs_paged",
    )(page_table, cache_t, w_big)


def _gelu_tanh(x):
    return 0.5 * x * (1.0 + jnp.tanh(0.7978845608028654 * (x + 0.044715 * x * x * x)))


def _cmp_pe_kernel(pe_ref, w1_ref, o_ref):
    o_ref[...] = _split_dot(pe_ref[...], w1_ref[0], w1_ref[1])


def _cmp_pe(pw):
    return pl.pallas_call(
        _cmp_pe_kernel,
        grid=(1,),
        in_specs=[pl.BlockSpec(pw['pe_flat'].shape, lambda i: (0, 0)),
                  pl.BlockSpec(pw['w1_pe'].shape, lambda i: (0, 0, 0))],
        out_specs=pl.BlockSpec((1, 2 * LANE), lambda i: (0, 0)),
        out_shape=jax.ShapeDtypeStruct((1, 2 * LANE), F32),
        compiler_params=_cparams(("arbitrary",)),
        name="cmp_pe",
    )(pw['pe_flat'], pw['w1_pe'])


def _cmp_finish_kernel(fs_ref, pe_ref, w2_ref, o_ref, ot_ref):
    fs = fs_ref[...]
    n = fs.shape[0]
    h = _gelu_tanh(fs[:, :2 * LANE] + pltpu.roll(fs[:, 2 * LANE:], n - 1, 0) + pe_ref[...])
    kc = _dot(h.astype(BF16), w2_ref[...])
    for i, t in enumerate(_kv_variants(kc[:, :LANE], kc[:, LANE:])):
        o_ref[:, i * LANE:(i + 1) * LANE] = t.astype(BF16)
        ot_ref[i * LANE:(i + 1) * LANE, :] = t.T.astype(BF16)


def _cmp_finish(fs, nb, pe, pw):
    r = fs.shape[0]
    full2 = lambda i: (0, 0)
    return pl.pallas_call(
        _cmp_finish_kernel,
        grid=(r // nb,),
        in_specs=[pl.BlockSpec((nb, 4 * LANE), lambda i: (i, 0)),
                  pl.BlockSpec(pe.shape, full2),
                  pl.BlockSpec(pw['w2_cmp'].shape, full2)],
        out_specs=[pl.BlockSpec((nb, 4 * LANE), lambda i: (i, 0)),
                   pl.BlockSpec((None, 4 * LANE, nb), lambda i: (i, 0, 0))],
        out_shape=[jax.ShapeDtypeStruct((r, 4 * LANE), BF16),
                   jax.ShapeDtypeStruct((r // nb, 4 * LANE, nb), BF16)],
        compiler_params=_cparams(("parallel",)),
        name="cmp_finish",
    )(fs, pe, pw['w2_cmp'])


def _masked_softmax(s, ok, axis=-1):
    s = jnp.where(ok, s, -jnp.inf)
    m = jnp.max(s, axis=axis, keepdims=True)
    m = jnp.where(m > -jnp.inf, m, 0.0)
    e = jnp.exp2(s - m)
    d = jnp.sum(e, axis=axis, keepdims=True)
    return e / jnp.where(d > 0.0, d, 1.0)


def _select_rank(imp, cur, n_slc):
    r, w = imp.shape
    blk = lax.broadcasted_iota(I32, (r, w), 1)
    valid = (blk <= cur) & (blk < n_slc)
    forced = (blk == 0) | (valid & (blk > cur - N_LOCAL_SLC))
    score = jnp.where(forced, FORCE_SCORE, jnp.where(valid, imp, -1.0))
    score = jnp.where(blk < n_slc, score, -2.0)

    def body(i, rank):
        col = jnp.sum(jnp.where(blk == i, score, 0.0), axis=1, keepdims=True)
        beats = (col > score) | ((col == score) & (i < blk))
        return rank + jnp.where(beats, 1.0, 0.0)

    rank = lax.fori_loop(0, n_slc, body, jnp.zeros((r, w), F32), unroll=True if n_slc <= 32 else 4)
    return rank, valid


def _cmp_select_kernel(slopes_ref, q_ref, kc_ref, kct_ref, wmapt_ref, o_ref, sel_ref, *, tq, n_slc):
    kvh = pl.program_id(1)
    q0 = pl.program_id(2) * tq
    nk = kc_ref.shape[0]
    qpos = q0 + lax.broadcasted_iota(I32, (nk, tq), 1)
    kend = lax.broadcasted_iota(I32, (nk, tq), 0) * CMP_STRIDE + (CMP_BLOCK - 1)
    ok = kend <= qpos
    dist = (qpos - kend).astype(F32)
    psum = jnp.zeros((nk, tq), F32)
    outs = []
    for g in range(NSA_GQA):
        sl = slice((g % 2) * LANE, (g % 2 + 1) * LANE)
        s = _dot_nt(kc_ref[:, sl], q_ref[:, g * LANE:(g + 1) * LANE]) - (slopes_ref[kvh * NSA_GQA + g] * LOG2E) * dist
        p = _masked_softmax(s, ok, axis=0)
        psum = psum + p
        outs.append(_dot(kct_ref[sl, :], p.astype(BF16)))
    lo = lax.broadcasted_iota(I32, (LANE, tq), 0) < HALF
    o_ref[:, :LANE] = jnp.where(lo, outs[1], outs[0]).T.astype(o_ref.dtype)
    o_ref[:, LANE:] = jnp.where(lo, outs[3], outs[2]).T.astype(o_ref.dtype)
    ps_hi = psum.astype(BF16)
    ps_lo = (psum - ps_hi.astype(F32)).astype(BF16)
    imp = _dot(wmapt_ref[...], ps_hi) + _dot(wmapt_ref[...], ps_lo)
    rows = -(-n_slc // 8) * 8
    imp = imp[:rows]
    blk = lax.broadcasted_iota(I32, (rows, tq), 0)
    cur = (q0 + lax.broadcasted_iota(I32, (1, tq), 1)) >> SLC_SHIFT
    valid = (blk <= cur) & (blk < n_slc)
    forced = (blk == 0) | (valid & (blk > cur - N_LOCAL_SLC))
    score = jnp.where(forced, FORCE_SCORE, jnp.where(valid, imp, -1.0))
    rank = jnp.zeros((rows, tq), F32)
    for i in range(n_slc):
        other = score[i:i + 1, :]
        rank = rank + jnp.where((other > score) | ((other == score) & (i < blk)), 1.0, 0.0)
    sel = jnp.where((rank < SLC_TOP_N) & valid, 1.0, 0.0)
    sel_ref[...] = jnp.concatenate([sel, jnp.zeros((LANE - rows, tq), F32)], axis=0).astype(sel_ref.dtype)


def _cmp_select(qn, kc_var, kc_var_t, slopes, wmap_t, b, s, n_slc):
    tq = min(256, s)
    nq = s // tq
    nk = kc_var.shape[0] // b
    assert n_slc <= LANE
    return pl.pallas_call(
        functools.partial(_cmp_select_kernel, tq=tq, n_slc=n_slc),
        grid=(b, NSA_KV_HEADS, nq),
        in_specs=[pl.BlockSpec(memory_space=pltpu.SMEM),
                  pl.BlockSpec((tq, 4 * LANE), lambda bi, kh, qi: (bi * nq + qi, kh)),
                  pl.BlockSpec((nk, 2 * LANE), lambda bi, kh, qi: (bi, kh)),
                  pl.BlockSpec((None, 2 * LANE, nk), lambda bi, kh, qi: (bi, kh, 0)),
                  pl.BlockSpec(wmap_t.shape, lambda bi, kh, qi: (0, 0))],
        out_specs=[pl.BlockSpec((tq, 2 * LANE), lambda bi, kh, qi: (bi * nq + qi, kh)),
                   pl.BlockSpec((None, None, LANE, tq), lambda bi, kh, qi: (bi, kh, 0, qi))],
        out_shape=[jax.ShapeDtypeStruct((b * s, 4 * LANE), BF16),
                   jax.ShapeDtypeStruct((b, NSA_KV_HEADS, LANE, s), BF16)],
        compiler_params=_cparams(("parallel", "parallel", "parallel")),
        name="cmp_select",
    )(slopes, qn, kc_var, kc_var_t, wmap_t)


def _layer_norm(x, g, b):
    mu = jnp.mean(x, axis=-1, keepdims=True)
    xc = x - mu
    var = jnp.mean(xc * xc, axis=-1, keepdims=True)
    return xc * lax.rsqrt(var + LN_EPS) * g + b


def _merge_kernel(x_ref, omla_ref, ocmp_ref, oslc_ref, owin_ref, gn_ref, ga_ref, gb_ref,
                  wo_ref, wn_ref, wout_ref, e3_ref, g1_ref, b1_ref, wr_ref, br_ref,
                  h_ref, rt_ref):
    gn = gn_ref[...]
    w = 4 * LANE
    ge = _split_dot(gn, e3_ref[...])
    o_nsa = (ge[:, :w] * ocmp_ref[...].astype(F32) + ge[:, w:2 * w] * oslc_ref[...].astype(F32)
             + ge[:, 2 * w:] * owin_ref[...].astype(F32))
    u = (ga_ref[...].astype(F32) * _dot(omla_ref[...], wo_ref[...])
         + gb_ref[...].astype(F32) * _dot(o_nsa.astype(BF16), wn_ref[...]))
    h = _layer_norm(DEEPNORM_ALPHA * x_ref[...] + _dot(u.astype(BF16), wout_ref[...]),
                    g1_ref[...], b1_ref[...])
    h_ref[...] = h
    lg = _split_dot(h, wr_ref[0], wr_ref[1]) + br_ref[...]
    tm = lg.shape[0]
    lane = lax.broadcasted_iota(I32, (tm, LANE), 1)
    is_g = lane < N_GROUPS
    gm = jnp.max(jnp.where(is_g, lg, -jnp.inf), axis=1, keepdims=True)
    gidx = jnp.min(jnp.where(is_g & (lg == gm), lane, LANE), axis=1, keepdims=True)
    p_group = 1.0 / jnp.sum(jnp.where(is_g, jnp.exp(lg - gm), 0.0), axis=1, keepdims=True)
    in_g = (lane >= HALF) & (((lane - HALF) >> 3) == gidx)
    m1 = jnp.max(jnp.where(in_g, lg, -jnp.inf), axis=1, keepdims=True)
    i1 = jnp.min(jnp.where(in_g & (lg == m1), lane, 2 * LANE), axis=1, keepdims=True)
    rest = in_g & (lane != i1)
    m2 = jnp.max(jnp.where(rest, lg, -jnp.inf), axis=1, keepdims=True)
    i2 = jnp.min(jnp.where(rest & (lg == m2), lane, 2 * LANE), axis=1, keepdims=True)
    t = jnp.exp(m2 - m1)
    g1 = p_group / (1.0 + t)
    g2 = p_group * t / (1.0 + t)
    rt = jnp.where(lane == 0, (i1 - HALF).astype(F32),
                   jnp.where(lane == 1, (i2 - HALF).astype(F32),
                             jnp.where(lane == 2, g1, jnp.where(lane == 3, g2, 0.0))))
    rt_ref[...] = rt


def _merge(x, omla, ocmp, oslc, owin, z, pw, tm):
    n, d = x.shape
    row = lambda i: (i, 0)
    full = lambda i: (0, 0)
    w = 4 * LANE
    return pl.pallas_call(
        _merge_kernel,
        grid=(n // tm,),
        in_specs=[pl.BlockSpec((tm, d), row), pl.BlockSpec((tm, w), row), pl.BlockSpec((tm, w), row),
                  pl.BlockSpec((tm, w), row), pl.BlockSpec((tm, w), row), pl.BlockSpec((tm, LANE), row),
                  pl.BlockSpec((tm, d), row), pl.BlockSpec((tm, d), row),
                  pl.BlockSpec((w, d), full), pl.BlockSpec((w, d), full), pl.BlockSpec((d, d), full),
                  pl.BlockSpec((LANE, 3 * w), full), pl.BlockSpec((1, d), full), pl.BlockSpec((1, d), full),
                  pl.BlockSpec((2, d, LANE), lambda i: (0, 0, 0)), pl.BlockSpec((1, LANE), full)],
        out_specs=[pl.BlockSpec((tm, d), row), pl.BlockSpec((tm, LANE), row)],
        out_shape=[jax.ShapeDtypeStruct((n, d), F32), jax.ShapeDtypeStruct((n, LANE), F32)],
        compiler_params=_cparams(("parallel",)),
        name="merge_router",
    )(x, omla, ocmp, oslc, owin, z['gn'], z['ga'], z['gb'], pw['w_o'], pw['w_on'], pw['w_out'],
      pw['e3'], pw['ln1_g'], pw['ln1_b'], pw['w_router'], pw['b_router'])


def _row_copy(src, dst, sem, src_row, dst_row):
    return pltpu.make_async_copy(src.at[pl.ds(src_row, 1)], dst.at[pl.ds(dst_row, 1)], sem)


def _moe_rank_kernel(rt_ref, tri_ref, rank_ref, cnt_ref, run_scr):
    @pl.when(pl.program_id(0) == 0)
    def _():
        run_scr[...] = jnp.zeros_like(run_scr)

    rt = rt_ref[...]
    lane = lax.broadcasted_iota(I32, rt.shape, 1)
    lanef = lane.astype(F32)
    e0 = jnp.sum(jnp.where(lane == 0, rt, 0.0), axis=1, keepdims=True)
    e1 = jnp.sum(jnp.where(lane == 1, rt, 0.0), axis=1, keepdims=True)
    hit0 = lanef == e0
    hit1 = lanef == e1
    onehot = jnp.where(hit0 | hit1, 1.0, 0.0)
    before = _dot(tri_ref[...], onehot.astype(BF16)) + run_scr[...]
    r0 = jnp.sum(jnp.where(hit0, before, 0.0), axis=1, keepdims=True)
    r1 = jnp.sum(jnp.where(hit1, before, 0.0), axis=1, keepdims=True)
    rank_ref[...] = jnp.where(lane == 0, r0, jnp.where(lane == 1, r1, 0.0))
    run = run_scr[...] + jnp.sum(onehot, axis=0, keepdims=True)
    run_scr[...] = run
    cnt_ref[...] = run


def _moe_rank(rt):
    n = rt.shape[0]
    tm = min(512, n)
    tri = jnp.asarray(np.tril(np.ones((tm, tm), np.float32), -1), BF16)
    return pl.pallas_call(
        _moe_rank_kernel,
        grid=(n // tm,),
        in_specs=[pl.BlockSpec((tm, LANE), lambda i: (i, 0)), pl.BlockSpec((tm, tm), lambda i: (0, 0))],
        out_specs=[pl.BlockSpec((tm, LANE), lambda i: (i, 0)), pl.BlockSpec((1, LANE), lambda i: (0, 0))],
        out_shape=[jax.ShapeDtypeStruct((n, LANE), F32), jax.ShapeDtypeStruct((1, LANE), F32)],
        scratch_shapes=[pltpu.VMEM((1, LANE), F32)],
        compiler_params=_cparams(("arbitrary",)),
        name="moe_rank",
    )(rt, tri)


def _moe_dispatch_kernel(dest_ref, h_ref, xs_in, xs_out, sem, *, tm):
    del xs_in

    def copies(t):
        return [_row_copy(h_ref, xs_out, sem, t, dest_ref[0, 0, TOP_K * t + k]) for k in range(TOP_K)]

    def start(t, c):
        for cp in copies(t):
            cp.start()
        return c

    def wait(t, c):
        for cp in copies(t):
            cp.wait()
        return c

    lax.fori_loop(0, tm, start, 0, unroll=DMA_UNROLL)
    lax.fori_loop(0, tm, wait, 0, unroll=DMA_UNROLL)


def _moe_dispatch(h, dest, n_slots):
    n, d = h.shape
    tm = min(256, n)
    return pl.pallas_call(
        functools.partial(_moe_dispatch_kernel, tm=tm),
        grid=(n // tm,),
        in_specs=[pl.BlockSpec((1, 1, TOP_K * tm), lambda i: (i, 0, 0), memory_space=pltpu.SMEM),
                  pl.BlockSpec((tm, d), lambda i: (i, 0)),
                  pl.BlockSpec(memory_space=pl.ANY)],
        out_specs=pl.BlockSpec(memory_space=pl.ANY),
        out_shape=jax.ShapeDtypeStruct((n_slots, d), F32),
        scratch_shapes=[pltpu.SemaphoreType.DMA(())],
        input_output_aliases={2: 0},
        compiler_params=_cparams(("arbitrary",)),
        name="moe_dispatch",
    )(dest.reshape(n // tm, 1, TOP_K * tm), h, jnp.zeros((n_slots, d), F32))


def _moe_expert_kernel(be_ref, x_ref, w1_ref, w3_ref, w2_ref, y_ref):
    del be_ref
    xb = x_ref[...].astype(BF16)
    a = _dot(xb, w1_ref[...].astype(BF16))
    hmid = a * _sigmoid(a) * _dot(xb, w3_ref[...].astype(BF16))
    y_ref[...] = _dot(hmid.astype(BF16), w2_ref[...].astype(BF16))


def _moe_experts(xs, blk_expert, w1, w3, w2):
    d = xs.shape[1]
    n_blocks = blk_expert.shape[0]
    de = w1.shape[-1]
    return pl.pallas_call(
        _moe_expert_kernel,
        grid_spec=pltpu.PrefetchScalarGridSpec(
            num_scalar_prefetch=1, grid=(n_blocks,),
            in_specs=[pl.BlockSpec((MOE_BLOCK, d), lambda i, be: (i, 0)),
                      pl.BlockSpec((None, None, d, de), lambda i, be: (0, be[i], 0, 0)),
                      pl.BlockSpec((None, None, d, de), lambda i, be: (0, be[i], 0, 0)),
                      pl.BlockSpec((None, None, de, d), lambda i, be: (0, be[i], 0, 0))],
            out_specs=pl.BlockSpec((MOE_BLOCK, d), lambda i, be: (i, 0))),
        out_shape=jax.ShapeDtypeStruct((n_blocks * MOE_BLOCK, d), F32),
        compiler_params=_cparams(("parallel",)),
        name="moe_experts",
    )(blk_expert, xs, w1, w3, w2)


def _moe_combine_kernel(dest_ref, nxt_ref, y_hbm, h_ref, rt_ref, g2_ref, b2_ref, o_ref, ybuf, sems, *, tm):
    i = pl.program_id(0)
    slot = i % 2

    def copies(ref, s, t):
        return [_row_copy(y_hbm, ybuf.at[s, k], sems.at[s], ref[0, 0, TOP_K * t + k], t) for k in range(TOP_K)]

    def start_all(ref, s):
        def body(t, c):
            for cp in copies(ref, s, t):
                cp.start()
            return c
        lax.fori_loop(0, tm, body, 0, unroll=DMA_UNROLL)

    @pl.when(i == 0)
    def _():
        start_all(dest_ref, 0)

    @pl.when(i + 1 < pl.num_programs(0))
    def _():
        start_all(nxt_ref, 1 - slot)

    def wait(t, c):
        for cp in copies(dest_ref, slot, t):
            cp.wait()
        return c

    lax.fori_loop(0, tm, wait, 0, unroll=DMA_UNROLL)
    y0 = ybuf[slot, 0]
    y1 = ybuf[slot, 1]
    rt = rt_ref[...]
    lane = lax.broadcasted_iota(I32, rt.shape, 1)
    g1 = jnp.sum(jnp.where(lane == 2, rt, 0.0), axis=1, keepdims=True)
    g2 = jnp.sum(jnp.where(lane == 3, rt, 0.0), axis=1, keepdims=True)
    moe = g1 * y0 + g2 * y1
    o_ref[...] = _layer_norm(DEEPNORM_ALPHA * h_ref[...] + moe, g2_ref[...], b2_ref[...])


def _moe_combine(y, dest, h, rt, ln_g, ln_b):
    n, d = h.shape
    tm = min(128, n)
    nt = n // tm
    row = lambda i: (i, 0)
    full = lambda i: (0, 0)
    dest3 = dest.reshape(nt, 1, TOP_K * tm)
    return pl.pallas_call(
        functools.partial(_moe_combine_kernel, tm=tm),
        grid=(nt,),
        in_specs=[pl.BlockSpec((1, 1, TOP_K * tm), lambda i: (i, 0, 0), memory_space=pltpu.SMEM),
                  pl.BlockSpec((1, 1, TOP_K * tm), lambda i: (jnp.minimum(i + 1, nt - 1), 0, 0),
                               memory_space=pltpu.SMEM),
                  pl.BlockSpec(memory_space=pl.ANY),
                  pl.BlockSpec((tm, d), row), pl.BlockSpec((tm, LANE), row),
                  pl.BlockSpec((1, d), full), pl.BlockSpec((1, d), full)],
        out_specs=pl.BlockSpec((tm, d), row),
        out_shape=jax.ShapeDtypeStruct((n, d), F32),
        scratch_shapes=[pltpu.VMEM((2, TOP_K, tm, d), F32), pltpu.SemaphoreType.DMA((2,))],
        compiler_params=_cparams(("arbitrary",)),
        name="moe_combine",
    )(dest3, dest3, y, h, rt, ln_g, ln_b)


def _moe(h, rt, pw, w1, w3, w2):
    n = h.shape[0]
    a = n * TOP_K
    rank, cnt = _moe_rank(rt)
    counts = cnt[0, :N_EXPERTS].astype(I32)
    padded = (counts + MOE_BLOCK - 1) // MOE_BLOCK * MOE_BLOCK
    pad_end = jnp.cumsum(padded)
    pad_start = pad_end - padded
    expert = rt[:, :TOP_K].astype(I32)
    first_slot = jnp.sum(jnp.where(expert[..., None] == jnp.arange(N_EXPERTS), pad_start, 0), axis=-1)
    dest = (first_slot + rank[:, :TOP_K].astype(I32)).reshape(-1)
    n_blocks = -(-a // MOE_BLOCK) + N_EXPERTS
    blk_start = jnp.arange(n_blocks) * MOE_BLOCK
    blk_expert = jnp.minimum(jnp.sum(pad_end[None, :] <= blk_start[:, None], axis=1), N_EXPERTS - 1).astype(I32)
    xs = _moe_dispatch(h, dest, n_blocks * MOE_BLOCK)
    y = _moe_experts(xs, blk_expert, w1, w3, w2)
    return _moe_combine(y, dest, h, rt, pw['ln2_g'], pw['ln2_b'])


def _qlat_kernel(q_ref, w_ref, o_ref):
    o_ref[...] = _dot(q_ref[...], w_ref[...]).astype(o_ref.dtype)


def _q_latent(qm, wabs):
    b = qm.shape[0]
    return pl.pallas_call(
        _qlat_kernel,
        grid=(MLA_HEADS,),
        in_specs=[pl.BlockSpec((b, LANE), lambda h: (0, h)),
                  pl.BlockSpec((None, LANE, MLA_KV_RANK), lambda h: (h, 0, 0))],
        out_specs=pl.BlockSpec((b, MLA_KV_RANK), lambda h: (0, h)),
        out_shape=jax.ShapeDtypeStruct((b, MLA_HEADS * MLA_KV_RANK), BF16),
        compiler_params=_cparams(("parallel",)),
        name="q_latent",
    )(qm, wabs)


def _mla_decode_kernel(pt_ref, ql_ref, qlt_ref, qr_ref, cn_ref, kn_ref, c_hbm, k_hbm, o_ref,
                       cbuf, kbuf, sems, m_scr, l_scr, acc_scr, *, n_pages, pp):
    slot = _prefetched_pages(pt_ref, (c_hbm, k_hbm), (cbuf, kbuf), sems, n_pages)
    qlt = qlt_ref[...]
    qr = qr_ref[...]
    cn = cn_ref[...].astype(BF16).astype(F32)
    kn = kn_ref[...].astype(BF16).astype(F32)
    m_scr[...] = (jnp.sum(ql_ref[...].astype(F32) * cn, axis=1, keepdims=True)
                  + jnp.sum(qr.astype(F32) * kn, axis=1, keepdims=True))
    l_scr[...] = jnp.ones_like(l_scr)
    acc_scr[...] = jnp.broadcast_to(cn, acc_scr.shape)

    def chunk(ci, carry):
        cs = [cbuf[slot, ci * pp + j].astype(BF16) for j in range(pp)]
        s = jnp.concatenate([_dot(c, qlt).T[:MLA_HEADS] + _dot(qr, kbuf[slot, ci * pp + j].astype(BF16))
                             for j, c in enumerate(cs)], axis=1)
        m_prev = m_scr[...]
        m_new = jnp.maximum(m_prev, jnp.max(s, axis=1, keepdims=True))
        alpha = jnp.exp2(m_prev - m_new)
        p = jnp.exp2(s - m_new)
        l_scr[...] = alpha * l_scr[...] + jnp.sum(p, axis=1, keepdims=True)
        pb = p.astype(BF16)
        acc = alpha * acc_scr[...]
        for j, c in enumerate(cs):
            acc = acc + _dot(pb[:, j * PAGE:(j + 1) * PAGE], c)
        acc_scr[...] = acc
        m_scr[...] = m_new
        return carry

    lax.fori_loop(0, n_pages // pp, chunk, 0)
    o_ref[...] = (acc_scr[...] / l_scr[...]).astype(o_ref.dtype)


def _mla_decode(qlat, qr, ckv_new, kr_new, cache_ckv, cache_kr_t, page_table):
    b, n_pages = page_table.shape
    pp = min(32, n_pages)
    r = MLA_KV_RANK
    ql3 = qlat.reshape(b, MLA_HEADS, r)
    qlt = jnp.pad(jnp.transpose(ql3, (0, 2, 1)), ((0, 0), (0, 0), (0, LANE - MLA_HEADS)))
    per_b = lambda bi, pt: (bi, 0, 0)
    return pl.pallas_call(
        functools.partial(_mla_decode_kernel, n_pages=n_pages, pp=pp),
        grid_spec=pltpu.PrefetchScalarGridSpec(
            num_scalar_prefetch=1, grid=(b,),
            in_specs=[pl.BlockSpec((None, MLA_HEADS, r), per_b),
                      pl.BlockSpec((None, r, LANE), per_b),
                      pl.BlockSpec((None, MLA_HEADS, MLA_ROPE), per_b),
                      pl.BlockSpec((None, 1, r), per_b),
                      pl.BlockSpec((None, 1, MLA_ROPE), per_b),
                      pl.BlockSpec(memory_space=pl.ANY),
                      pl.BlockSpec(memory_space=pl.ANY)],
            out_specs=pl.BlockSpec((None, MLA_HEADS, r), per_b),
            scratch_shapes=[pltpu.VMEM((2, n_pages, PAGE, r), F32), pltpu.VMEM((2, n_pages, MLA_ROPE, PAGE), F32),
                            pltpu.SemaphoreType.DMA((2,)),
                            pltpu.VMEM((MLA_HEADS, 1), F32), pltpu.VMEM((MLA_HEADS, 1), F32),
                            pltpu.VMEM((MLA_HEADS, r), F32)]),
        out_shape=jax.ShapeDtypeStruct((b, MLA_HEADS, r), BF16),
        compiler_params=_cparams(("arbitrary",)),
        name="mla_decode",
    )(page_table, ql3, qlt, qr, ckv_new.reshape(b, 1, r), kr_new.reshape(b, 1, MLA_ROPE),
      cache_ckv, cache_kr_t)


def _oproj_kernel(o_ref, w_ref, out_ref):
    out_ref[...] = (_dot(o_ref[:, :MLA_KV_RANK], w_ref[:, :LANE])
                    + _dot(o_ref[:, MLA_KV_RANK:], w_ref[:, LANE:])).astype(out_ref.dtype)


def _mla_out_up(o_lat, wuv):
    b = o_lat.shape[0]
    return pl.pallas_call(
        _oproj_kernel,
        grid=(MLA_HEADS // 2,),
        in_specs=[pl.BlockSpec((b, 2 * MLA_KV_RANK), lambda j: (0, j)),
                  pl.BlockSpec((MLA_KV_RANK, 2 * LANE), lambda j: (0, j))],
        out_specs=pl.BlockSpec((b, LANE), lambda j: (0, j)),
        out_shape=jax.ShapeDtypeStruct((b, 4 * LANE), BF16),
        compiler_params=_cparams(("parallel",)),
        name="mla_out_up",
    )(o_lat, wuv)


def _attend8(q8, variants, bias, ok):
    hrow = lax.broadcasted_iota(I32, (MLA_HEADS, 1), 0)
    vsel = (hrow >> GQA_SHIFT) * 2 + (hrow & 1)
    s = jnp.zeros(bias.shape, F32)
    for c, kv in enumerate(variants):
        s = s + jnp.where(vsel == c, _dot_nt(q8, kv), 0.0)
    p = _masked_softmax(s + bias, ok)
    pb = p.astype(BF16)
    o = jnp.zeros((MLA_HEADS, LANE), F32)
    for c, kv in enumerate(variants):
        o = o + jnp.where(vsel == c, _dot(pb, kv), 0.0)
    return o, p


def _slope_col():
    h = lax.broadcasted_iota(I32, (NSA_HEADS, 1), 0)
    return jnp.exp2(-(h + 1).astype(F32) * (8.0 / NSA_HEADS)) * LOG2E


def _attend_t(q8, kts, vts, bias, ok, new_row):
    hrow = lax.broadcasted_iota(I32, (NSA_HEADS, 1), 0)
    first = (hrow >> GQA_SHIFT) == 0
    s = jnp.where(first, _dot(q8, kts[0]), _dot(q8, kts[1])) + bias
    s = jnp.where(ok, s, NEG)
    new8 = jnp.broadcast_to(new_row, (NSA_HEADS, new_row.shape[1])).astype(BF16).astype(F32)
    kk, vv = new8[:, :LANE], new8[:, LANE:]
    k_new = jnp.where(first, kk, pltpu.roll(kk, HALF, 1))[:, :NSA_HD]
    v_new = jnp.where(first, vv, pltpu.roll(vv, HALF, 1))[:, :NSA_HD]
    s_new = jnp.sum(q8.astype(F32) * k_new, axis=1, keepdims=True)
    m = jnp.maximum(jnp.max(s, axis=1, keepdims=True), s_new)
    p = jnp.exp2(s - m)
    p_new = jnp.exp2(s_new - m)
    l = jnp.sum(p, axis=1, keepdims=True) + p_new
    pb = p.astype(BF16)
    o = jnp.where(first, _dot_nt(pb, vts[0]), _dot_nt(pb, vts[1])) + p_new * v_new
    return o / l


def _sample_cmp_win_kernel(qx_ref, q_ref, kc_ref, wmap_ref, st_ref, kvw_ref, newt_ref,
                           ocmp_ref, imp_ref, owin_ref, wout_ref, *, past_len):
    q8 = qx_ref[...]
    slope = _slope_col()
    nk = kc_ref.shape[0]
    kend = lax.broadcasted_iota(I32, (NSA_HEADS, nk), 1) * CMP_STRIDE + (CMP_BLOCK - 1)
    ok = kend <= past_len
    bias = -slope * (past_len - kend).astype(F32)
    kc = kc_ref[...]
    o, p = _attend8(q8, [kc[:, i * LANE:(i + 1) * LANE] for i in range(4)], bias, ok)
    ocmp_ref[...] = o.astype(ocmp_ref.dtype)
    hrow = lax.broadcasted_iota(I32, (NSA_HEADS, 1), 0)
    for kh in range(NSA_KV_HEADS):
        psum = jnp.sum(jnp.where((hrow >> GQA_SHIFT) == kh, p, 0.0), axis=0, keepdims=True)
        imp_ref[kh:kh + 1, :] = _split_dot(psum, wmap_ref[...])
    st = st_ref[...]
    wbuf = st.shape[1]
    stb = st.astype(BF16)
    j = lax.broadcasted_iota(I32, (NSA_HEADS, wbuf), 1)
    dist = wbuf - j
    okw = (dist <= WINDOW) & (past_len - dist >= 0)
    owin_ref[...] = _attend_t(q_ref[...], [stb[:NSA_HD], stb[NSA_HD:2 * NSA_HD]],
                              [stb[2 * NSA_HD:3 * NSA_HD], stb[3 * NSA_HD:]],
                              -slope * dist.astype(F32), okw, kvw_ref[...])
    newt = newt_ref[...]
    pick = lax.broadcasted_iota(I32, newt.shape, 1) == pl.program_id(0)
    new_col = jnp.sum(jnp.where(pick, newt, 0.0), axis=1, keepdims=True)
    nblk = wbuf // LANE
    rolled = [pltpu.roll(st[:, k * LANE:(k + 1) * LANE], LANE - 1, 1) for k in range(nblk)]
    keep = lax.broadcasted_iota(I32, (st.shape[0], LANE), 1) < LANE - 1
    for k in range(nblk):
        nxt = rolled[k + 1] if k + 1 < nblk else jnp.broadcast_to(new_col, rolled[k].shape)
        wout_ref[:, k * LANE:(k + 1) * LANE] = jnp.where(keep, rolled[k], nxt)


def _sample_cmp_win(qx, q8, kc_var, wmap, state_t, kvw, past_len):
    b = qx.shape[0]
    nk = kc_var.shape[0] // b
    wbuf = state_t.shape[2]
    wcols = wmap.shape[1]
    per_b3 = lambda bi: (bi, 0, 0)
    return pl.pallas_call(
        functools.partial(_sample_cmp_win_kernel, past_len=past_len),
        grid=(b,),
        in_specs=[pl.BlockSpec((None, NSA_HEADS, LANE), per_b3),
                  pl.BlockSpec((None, NSA_HEADS, NSA_HD), per_b3),
                  pl.BlockSpec((nk, 4 * LANE), lambda bi: (bi, 0)),
                  pl.BlockSpec(wmap.shape, lambda bi: (0, 0)),
                  pl.BlockSpec((None, NSA_KV_COLS, wbuf), per_b3),
                  pl.BlockSpec((None, 1, NSA_KV_COLS), per_b3),
                  pl.BlockSpec((NSA_KV_COLS, b), lambda bi: (0, 0))],
        out_specs=[pl.BlockSpec((None, NSA_HEADS, LANE), per_b3),
                   pl.BlockSpec((None, NSA_KV_HEADS, wcols), per_b3),
                   pl.BlockSpec((None, NSA_HEADS, NSA_HD), per_b3),
                   pl.BlockSpec((None, NSA_KV_COLS, wbuf), per_b3)],
        out_shape=[jax.ShapeDtypeStruct((b, NSA_HEADS, LANE), F32),
                   jax.ShapeDtypeStruct((b, NSA_KV_HEADS, wcols), F32),
                   jax.ShapeDtypeStruct((b, NSA_HEADS, NSA_HD), F32),
                   jax.ShapeDtypeStruct((b, NSA_KV_COLS, wbuf), F32)],
        compiler_params=_cparams(("parallel",)),
        name="sample_cmp_win",
    )(qx.reshape(b, NSA_HEADS, LANE), q8, kc_var, wmap, state_t, kvw.reshape(b, 1, NSA_KV_COLS),
      jnp.transpose(kvw))


def _sample_select_kernel(imp_ref, idx_ref, *, cur, n_slc):
    imp = imp_ref[...]
    rank, valid = _select_rank(imp, cur, n_slc)
    r, w = imp.shape
    blk = lax.broadcasted_iota(I32, (r, w), 1).astype(F32)
    lane = lax.broadcasted_iota(I32, (r, LANE), 1)
    out = jnp.zeros((r, LANE), F32)
    for t in range(SLC_TOP_N):
        pick = jnp.sum(jnp.where((rank == t) & valid, blk, 0.0), axis=1, keepdims=True)
        out = jnp.where(lane == t, pick, out)
    idx_ref[...] = out.astype(I32)


def _sample_select(imp, cur, n_slc):
    r = imp.shape[0]
    return pl.pallas_call(
        functools.partial(_sample_select_kernel, cur=cur, n_slc=n_slc),
        grid=(1,),
        in_specs=[pl.BlockSpec(imp.shape, lambda i: (0, 0))],
        out_specs=pl.BlockSpec((r, LANE), lambda i: (0, 0)),
        out_shape=jax.ShapeDtypeStruct((r, LANE), I32),
        compiler_params=_cparams(("arbitrary",)),
        name="sample_select",
    )(imp)


def _sample_slc_kernel(*refs, n_sel, past_len):
    blk_ref, page_ref = refs[0], refs[1]
    del page_ref
    q_ref, new_ref = refs[2], refs[3]
    c_refs = refs[4:4 + NSA_KV_HEADS * n_sel]
    o_ref = refs[4 + NSA_KV_HEADS * n_sel]
    bi = pl.program_id(0)
    hrow = lax.broadcasted_iota(I32, (NSA_HEADS, 1), 0)
    first = (hrow >> GQA_SHIFT) == 0
    nk = n_sel * PAGE
    col = lax.broadcasted_iota(I32, (NSA_HEADS, nk), 1)
    tile = col >> PAGE_SHIFT
    row = col & (PAGE - 1)
    kts, vts, blks = [], [], []
    for kh in range(NSA_KV_HEADS):
        mine = c_refs[kh * n_sel:(kh + 1) * n_sel]
        kts.append(jnp.concatenate([c[kh * NSA_HD:(kh + 1) * NSA_HD, :] for c in mine], axis=1).astype(BF16))
        vts.append(jnp.concatenate([c[(2 + kh) * NSA_HD:(3 + kh) * NSA_HD, :] for c in mine],
                                   axis=1).astype(BF16))
        blkv = jnp.zeros(col.shape, I32)
        for t in range(n_sel):
            blkv = jnp.where(tile == t, blk_ref[bi, kh * n_sel + t], blkv)
        blks.append(blkv)
    blkv = jnp.where(first, blks[0], blks[1])
    kpos = (blkv >> 1) * PAGE + row
    ok = ((row >> SLC_SHIFT) == (blkv & 1)) & (kpos < past_len)
    dist = past_len - kpos
    o_ref[...] = _attend_t(q_ref[...], kts, vts, -_slope_col() * dist.astype(F32), ok, new_ref[...])


def _sample_slc(q8, kvs_new, cache_slc_t, blk_idx, page_idx, past_len):
    b = q8.shape[0]
    n_sel = blk_idx.shape[1] // NSA_KV_HEADS
    per_b3 = lambda bi, blk, pg: (bi, 0, 0)
    in_specs = [pl.BlockSpec((None, NSA_HEADS, NSA_HD), per_b3),
                pl.BlockSpec((None, 1, NSA_KV_COLS), per_b3)]
    in_specs += [pl.BlockSpec((None, NSA_KV_COLS, PAGE),
                              functools.partial(lambda bi, blk, pg, j: (pg[bi, j], 0, 0), j=j))
                 for j in range(NSA_KV_HEADS * n_sel)]
    return pl.pallas_call(
        functools.partial(_sample_slc_kernel, n_sel=n_sel, past_len=past_len),
        grid_spec=pltpu.PrefetchScalarGridSpec(
            num_scalar_prefetch=2, grid=(b,), in_specs=in_specs,
            out_specs=pl.BlockSpec((None, NSA_HEADS, NSA_HD), per_b3)),
        out_shape=jax.ShapeDtypeStruct((b, NSA_HEADS, NSA_HD), F32),
        compiler_params=_cparams(("parallel",)),
        name="sample_slc",
    )(blk_idx, page_idx, q8, kvs_new.reshape(b, 1, NSA_KV_COLS), *([cache_slc_t] * (NSA_KV_HEADS * n_sel)))


def _heads_to_swapped(o8):
    b = o8.shape[0]
    return o8.reshape(b, NSA_HEADS // 2, 2, NSA_HD)[:, :, ::-1].reshape(b, NSA_HEADS * NSA_HD).astype(BF16)


def _pair_swap(o8):
    b = o8.shape[0]
    o = o8.reshape(b, NSA_HEADS // 2, 2, LANE)
    lo = jnp.arange(LANE) < HALF
    return jnp.where(lo, o[:, :, 1], o[:, :, 0]).reshape(b, 4 * LANE).astype(BF16)


def _rope_table(pos):
    half = MLA_ROPE // 2
    freqs = ROPE_THETA ** (-jnp.arange(half, dtype=F32) / half)
    ang = pos.astype(F32)[:, None] * freqs
    cos, sin = jnp.cos(ang), jnp.sin(ang)
    n = pos.shape[0]
    one, zero = jnp.ones((n, MLA_NOPE), F32), jnp.zeros((n, MLA_NOPE), F32)
    pad = jnp.zeros((n, LANE - MLA_NOPE - MLA_ROPE), F32)
    return jnp.concatenate([one, cos, cos, pad, zero, -sin, sin, pad], axis=1)


def _cmp_to_slc(n_cmp, n_slc, rows, cols):
    cs = np.arange(n_cmp)[:, None] * CMP_STRIDE
    ss = np.arange(n_slc)[None, :] * SLC_BLOCK
    inter = np.clip(np.minimum(cs + CMP_BLOCK, ss + SLC_BLOCK) - np.maximum(cs, ss), 0, None)
    w = np.zeros((rows, cols), np.float32)
    w[:n_cmp, :n_slc] = inter.astype(np.float32) / CMP_STRIDE
    return jnp.asarray(w, BF16)


def _prep_weights(w_in, mla_g_q, mla_g_kv, mla_w_uq, mla_w_uk, mla_w_uv, mla_w_o, nsa_pe_k, nsa_w1_k,
                  nsa_w2_k, nsa_pe_v, nsa_w1_v, nsa_w2_v, nsa_w_o, w_out, ln1_g, ln1_b, router_g_w,
                  router_g_b, router_e_w, router_e_b, ln2_g, ln2_b):
    d = w_in.shape[0]
    splits = (MLA_Q_RANK, MLA_KV_RANK, MLA_ROPE, NSA_HEADS * NSA_HD, NSA_KV_COLS, NSA_KV_COLS, NSA_KV_COLS,
              3 * NSA_HEADS, d, d)
    offs = np.cumsum(splits)[:-1].tolist()
    cq, ckv, kr, q, kvc, kvs, kvw, gn, ga, gb = jnp.split(w_in, offs, axis=1)
    r2 = MLA_ROPE // 2
    zc = lambda n: jnp.zeros((d, n), F32)
    tail = LANE - MLA_NOPE - MLA_ROPE
    kr_pad = jnp.concatenate([zc(MLA_NOPE), kr, zc(tail)], axis=1)
    kr_sw = jnp.concatenate([zc(MLA_NOPE), kr[:, r2:], kr[:, :r2], zc(tail)], axis=1)
    gn_pad = jnp.concatenate([gn, zc(LANE - gn.shape[1])], axis=1)
    w_all = jnp.concatenate([cq, ckv, q, kvc, kvs, kvw, kr_pad, kr_sw, gn_pad, ga, gb], axis=1).astype(BF16)

    def pad_heads(w, lo_cols):
        r, hh, c = w.shape
        out = jnp.zeros((r, hh, LANE), F32).at[:, :, lo_cols:lo_cols + c].set(w)
        return out

    nope, rope = mla_w_uq[:, :, :MLA_NOPE], mla_w_uq[:, :, MLA_NOPE:]
    wuq = jnp.concatenate([nope, rope, jnp.zeros(nope.shape[:2] + (tail,), F32)], axis=2)
    wuqs = jnp.concatenate([jnp.zeros_like(nope), rope[:, :, r2:], rope[:, :, :r2],
                            jnp.zeros(nope.shape[:2] + (tail,), F32)], axis=2)
    wuk = pad_heads(mla_w_uk, 0)
    odd = (jnp.arange(MLA_HEADS) % 2 == 1)[None, :, None]
    wuv = jnp.where(odd, pad_heads(mla_w_uv, HALF), pad_heads(mla_w_uv, 0))
    flat = lambda w: w.reshape(w.shape[0], -1).astype(BF16)
    wabs = jnp.zeros((MLA_HEADS, LANE, MLA_KV_RANK), F32).at[:, :MLA_NOPE, :].set(
        jnp.transpose(mla_w_uk, (1, 2, 0))).astype(BF16)

    def w1_cols(w1):
        return jnp.transpose(w1.reshape(2, CMP_STRIDE, NSA_HD, CMP_HIDDEN), (1, 2, 0, 3))

    wt = jnp.stack([w1_cols(nsa_w1_k), w1_cols(nsa_w1_v)])
    eye2 = jnp.eye(2, dtype=F32)
    w_big = jnp.einsum('tpdfh,tu,kv->ptkdfuvh', wt, eye2, eye2).reshape(
        CMP_STRIDE, NSA_KV_COLS, 2 * 2 * NSA_KV_HEADS * CMP_HIDDEN).astype(BF16)
    zpe = jnp.zeros((CMP_BLOCK * NSA_HD, CMP_HIDDEN), F32)
    w1_pe = jnp.concatenate([
        jnp.concatenate([nsa_w1_k, nsa_w1_k, zpe, zpe], axis=1),
        jnp.concatenate([zpe, zpe, nsa_w1_v, nsa_w1_v], axis=1)], axis=0)
    pe_flat = jnp.concatenate([nsa_pe_k.reshape(1, -1), nsa_pe_v.reshape(1, -1)], axis=1)
    w1_hi = w1_pe.astype(BF16)
    w1_lo = (w1_pe - w1_hi.astype(F32)).astype(BF16)
    z64 = jnp.zeros((CMP_HIDDEN, NSA_HD), F32)
    rows = [[nsa_w2_k, z64, z64, z64], [z64, nsa_w2_k, z64, z64], [z64, z64, nsa_w2_v, z64],
            [z64, z64, z64, nsa_w2_v]]
    w2_cmp = jnp.concatenate([jnp.concatenate(r, axis=1) for r in rows], axis=0).astype(BF16)

    perm = np.arange(NSA_HEADS * NSA_HD).reshape(NSA_HEADS // 2, 2, NSA_HD)[:, ::-1].reshape(-1)
    w_on = nsa_w_o[perm].astype(BF16)
    e3 = np.zeros((LANE, 3, NSA_HEADS * NSA_HD), np.float32)
    for h in range(NSA_HEADS):
        pos = (h // 2) * LANE + (0 if h % 2 else HALF)
        for j in range(3):
            e3[3 * h + j, j, pos:pos + NSA_HD] = 1.0
    e3 = jnp.asarray(e3.reshape(LANE, -1), BF16)
    w_r = jnp.concatenate([router_g_w, zc(HALF - N_GROUPS), router_e_w], axis=1)
    w_r_hi = w_r.astype(BF16)
    w_r_lo = (w_r - w_r_hi.astype(F32)).astype(BF16)
    b_r = jnp.concatenate([router_g_b, jnp.zeros((HALF - N_GROUPS,), F32), router_e_b])[None, :]
    return dict(
        w_all=w_all, wuq=flat(wuq), wuqs=flat(wuqs), wuk=flat(wuk), wuv=flat(wuv),
        gq=mla_g_q[None, :], gkv=mla_g_kv[None, :], wabs=wabs, w_big=w_big,
        pe_flat=pe_flat, w1_pe=jnp.stack([w1_hi, w1_lo]), w2_cmp=w2_cmp,
        w_o=mla_w_o.astype(BF16), w_on=w_on, w_out=w_out.astype(BF16), e3=e3,
        ln1_g=ln1_g[None, :], ln1_b=ln1_b[None, :], w_router=jnp.stack([w_r_hi, w_r_lo]), b_router=b_r,
        ln2_g=ln2_g[None, :], ln2_b=ln2_b[None, :])


def _prompt_layer(x, pw, pe, slopes, moe_w):
    b, s, d = x.shape
    n = b * s
    tm = min(256, s)
    z = _in_proj(x.reshape(n, d), _rope_table(jnp.arange(s)), pw, tm, s)
    o_mla = _flash(z['qm'], z['km'], z['vm'], slopes, b, s, mode='causal', kv_shared=False, alibi=False,
                   swap=False)
    nb = s // CMP_STRIDE
    fs = _cmp_fs_dense(z['kvc'], pw['w_big'])
    kc_var, kc_var_t = _cmp_finish(fs, nb, pe, pw)
    n_slc = -(-s // SLC_BLOCK)
    wmap_t = jnp.transpose(_cmp_to_slc(nb - 1, n_slc, nb, LANE))
    o_cmp, sel = _cmp_select(z['qn'], kc_var, kc_var_t, slopes, wmap_t, b, s, n_slc)
    emat_t = jnp.asarray((np.arange(s)[:, None] // SLC_BLOCK) == np.arange(LANE)[None, :], BF16)
    o_slc = _flash(z['qn'], z['kvsv'], z['kvsvt'], slopes, b, s, mode='select', kv_shared=True, alibi=True,
                   swap=True, selt=sel, emat_t=emat_t)
    o_win = _flash(z['qn'], z['kvwv'], z['kvwvt'], slopes, b, s, mode='window', kv_shared=True, alibi=True,
                   swap=True)
    h, rt = _merge(x.reshape(n, d), o_mla, o_cmp, o_slc, o_win, z, pw, min(512, s))
    y = _moe(h, rt, pw, *moe_w)
    return y.reshape(b, s, d), z


def _sample_layer(x, pw, pe, slopes, moe_w, cache_ckv, cache_kr_t, cache_cmp_t, cache_slc_t, state_t, page_table):
    b, t, d = x.shape
    n_pages = page_table.shape[1]
    past_len = n_pages * PAGE
    z = _in_proj(x.reshape(b, d), _rope_table(jnp.full((b,), past_len)), pw, b, b)
    qlat = _q_latent(z['qm'], pw['wabs'])
    qr = z['qm'].reshape(b, MLA_HEADS, LANE)[:, :, MLA_NOPE:MLA_NOPE + MLA_ROPE]
    o_lat = _mla_decode(qlat, qr, z['ckv'], z['kr'], cache_ckv, cache_kr_t, page_table)
    o_mla = _mla_out_up(o_lat.reshape(b, MLA_HEADS * MLA_KV_RANK), pw['wuv'])
    nb = past_len // CMP_STRIDE
    fs = _cmp_fs_paged(cache_cmp_t, page_table, pw['w_big'])
    kc_var, _ = _cmp_finish(fs, nb, pe, pw)
    n_slc = -(-(past_len + t) // SLC_BLOCK)
    assert n_slc >= SLC_TOP_N and past_len % SLC_BLOCK == 0
    wcols = -(-n_slc // LANE) * LANE
    wmap = _cmp_to_slc(nb - 1, n_slc, nb, wcols)
    q8 = z['qnat'].reshape(b, NSA_HEADS, NSA_HD)
    o_cmp8, imp, o_win8, win_out = _sample_cmp_win(z['qn'], q8, kc_var, wmap, state_t, z['kvw'], past_len)
    cur = past_len // SLC_BLOCK
    idx = _sample_select(imp.reshape(b * NSA_KV_HEADS, wcols), cur, n_slc)[:, :SLC_TOP_N]
    blk = idx.reshape(b, NSA_KV_HEADS * SLC_TOP_N)
    safe = jnp.minimum(blk, past_len // SLC_BLOCK - 1)
    pages = jnp.take_along_axis(page_table, safe // (PAGE // SLC_BLOCK), axis=1).astype(I32)
    o_slc8 = _sample_slc(q8, z['kvs'], cache_slc_t, blk, pages, past_len)
    h, rt = _merge(x.reshape(b, d), o_mla, _pair_swap(o_cmp8), _heads_to_swapped(o_slc8),
                   _heads_to_swapped(o_win8), z, pw, b)
    y = _moe(h, rt, pw, *moe_w)
    return y.reshape(b, t, d), z, win_out


def kernel(x_prompt, x_sample, cache_mla_ckv, cache_mla_krope, cache_nsa_cmp_kv, cache_nsa_slc_kv,
           state_nsa_win_kv, page_table, w_in, mla_g_q, mla_g_kv, mla_w_uq, mla_w_uk, mla_w_uv, mla_w_o,
           nsa_pe_k, nsa_w1_k, nsa_w2_k, nsa_pe_v, nsa_w1_v, nsa_w2_v, nsa_w_o, w_out, ln1_g, ln1_b,
           router_g_w, router_g_b, router_e_w, router_e_b, moe_w1, moe_w3, moe_w2, ln2_g, ln2_b):
    assert w_in.shape[0] == DEPTH and x_sample.shape[1] == 1
    b, s, d = x_prompt.shape
    bs = x_sample.shape[0]
    kv_shape = (2, NSA_KV_HEADS, NSA_HD)
    pw = _prep_weights(w_in[0], mla_g_q[0], mla_g_kv[0], mla_w_uq[0], mla_w_uk[0], mla_w_uv[0], mla_w_o[0],
                       nsa_pe_k[0], nsa_w1_k[0], nsa_w2_k[0], nsa_pe_v[0], nsa_w1_v[0], nsa_w2_v[0],
                       nsa_w_o[0], w_out[0], ln1_g[0], ln1_b[0], router_g_w[0], router_g_b[0],
                       router_e_w[0], router_e_b[0], ln2_g[0], ln2_b[0])
    slopes = 2.0 ** (-8.0 * jnp.arange(1, NSA_HEADS + 1, dtype=F32) / NSA_HEADS)
    moe_w = (moe_w1, moe_w3, moe_w2)
    pe = _cmp_pe(pw)
    y_p, zp = _prompt_layer(x_prompt, pw, pe, slopes, moe_w)
    def rows_last(c):
        return jnp.transpose(c, (0, 2, 3, 4, 1)).reshape(c.shape[0], NSA_KV_COLS, c.shape[1])

    y_s, zs, win_t = _sample_layer(
        x_sample, pw, pe, slopes, moe_w, cache_mla_ckv[0], jnp.transpose(cache_mla_krope[0], (0, 2, 1)),
        rows_last(cache_nsa_cmp_kv[0]), rows_last(cache_nsa_slc_kv[0]), rows_last(state_nsa_win_kv[0]),
        page_table)
    def rows_first(t):
        return jnp.transpose(t.reshape((t.shape[0],) + kv_shape + (-1,)), (0, 4, 1, 2, 3))[None]

    wp = min(WINDOW, s)
    return (y_p, y_s,
            zp['ckv'].reshape(1, b, s, MLA_KV_RANK), zs['ckv'].reshape(1, bs, 1, MLA_KV_RANK),
            zp['kr'].reshape(1, b, s, MLA_ROPE), zs['kr'].reshape(1, bs, 1, MLA_ROPE),
            rows_first(zp['kvct']), zs['kvc'].reshape((1, bs, 1) + kv_shape),
            rows_first(zp['kvst']), zs['kvs'].reshape((1, bs, 1) + kv_shape),
            rows_first(zp['kvwt'][:, :, s - wp:]), rows_first(win_t))
```

```python
import functools

import numpy as np
import jax
import jax.numpy as jnp
from jax import lax
from jax.experimental import pallas as pl
from jax.experimental.pallas import tpu as pltpu

F32 = jnp.float32
BF16 = jnp.bfloat16
I32 = jnp.int32

PAGE = 128
MLA_HEADS = 8
MLA_NOPE = 64
MLA_ROPE = 32
MLA_V = 64
MLA_Q_RANK = 256
MLA_KV_RANK = 256
ROPE_THETA = 10000.0
NSA_HEADS = 8
NSA_KV_HEADS = 2
NSA_GQA = NSA_HEADS // NSA_KV_HEADS
NSA_HD = 64
CMP_BLOCK = 32
CMP_STRIDE = 16
CMP_HIDDEN = 64
SLC_BLOCK = 64
SLC_TOP_N = 16
N_LOCAL_SLC = 2
WINDOW = 512
FORCE_SCORE = 1.0e4
N_GROUPS = 8
EXPERTS_PER_GROUP = 8
N_EXPERTS = N_GROUPS * EXPERTS_PER_GROUP
TOP_K = 2
D_EXPERT = 256
MOE_BLOCK = 256
LN_EPS = 1e-5
RMS_EPS = 1e-6
DEPTH = 1
DEEPNORM_ALPHA = (2.0 * DEPTH) ** 0.25
NSA_KV_COLS = 2 * NSA_KV_HEADS * NSA_HD

PAGE_SHIFT = PAGE.bit_length() - 1
SLC_SHIFT = SLC_BLOCK.bit_length() - 1
GQA_SHIFT = NSA_GQA.bit_length() - 1
LANE = 128
HALF = LANE // 2
VMEM_LIMIT = 56 * 1024 * 1024
NEG = -1e30
DMA_UNROLL = 8
LOG2E = 1.4426950408889634

_C_CQ, _C_CKV, _C_Q, _C_KVC, _C_KVS, _C_KVW, _C_KRP, _C_KRS, _C_GN, _C_GA = (
    0, 256, 512, 1024, 1280, 1536, 1792, 1920, 2048, 2176)


def _cparams(sem):
    return pltpu.CompilerParams(dimension_semantics=sem, vmem_limit_bytes=VMEM_LIMIT)


def _dot(a, b):
    return jnp.dot(a, b, preferred_element_type=F32)


def _dot_nt(a, b):
    return lax.dot_general(a, b, (((1,), (1,)), ((), ())), preferred_element_type=F32)


def _split_dot(a, w_hi, w_lo=None):
    a_hi = a.astype(BF16)
    a_lo = (a - a_hi.astype(F32)).astype(BF16)
    r = _dot(a_hi, w_hi) + _dot(a_lo, w_hi)
    if w_lo is not None:
        r = r + _dot(a_hi, w_lo)
    return r


def _sigmoid(x):
    return 1.0 / (1.0 + jnp.exp(-x))


def _kv_variants(x0, x1):
    lo = lax.broadcasted_iota(I32, x0.shape, 1) < HALF
    r0 = pltpu.roll(x0, HALF, 1)
    r1 = pltpu.roll(x1, HALF, 1)
    return (jnp.where(lo, x0, r1), jnp.where(lo, x1, r0), jnp.where(lo, r0, x1), jnp.where(lo, r1, x0))


def _inproj_kernel(x_ref, cs_ref, w_ref, wuq_ref, wuqs_ref, wuk_ref, wuv_ref, gq_ref, gkv_ref,
                   qm_ref, km_ref, vm_ref, ckv_ref, kr_ref, qn_ref, kvc_ref, kvs_ref, kvw_ref,
                   kvsv_ref, kvwv_ref, gn_ref, ga_ref, gb_ref, qnat_ref, kvsvt_ref, kvwvt_ref,
                   kvct_ref, kvst_ref, kvwt_ref, *, mla_scale):
    xb = x_ref[...].astype(BF16)

    def proj(a, b):
        return _dot(xb, w_ref[:, a:b])

    cos = cs_ref[:, :LANE]
    sin = cs_ref[:, LANE:]

    def rms(z, g):
        return z * lax.rsqrt(jnp.mean(z * z, axis=-1, keepdims=True) + RMS_EPS) * g

    cq = rms(proj(_C_CQ, _C_CKV), gq_ref[...]).astype(BF16)
    q = _dot(cq, wuq_ref[...])
    qs = _dot(cq, wuqs_ref[...])
    ckv = rms(proj(_C_CKV, _C_Q), gkv_ref[...])
    ckv_ref[...] = ckv
    ckvb = ckv.astype(BF16)
    kr = proj(_C_KRP, _C_KRS) * cos + proj(_C_KRS, _C_GN) * sin
    kr_ref[...] = kr[:, MLA_NOPE:MLA_NOPE + MLA_ROPE]
    kn = _dot(ckvb, wuk_ref[...])
    v = _dot(ckvb, wuv_ref[...])
    for h in range(MLA_HEADS):
        sl = slice(h * LANE, (h + 1) * LANE)
        qm_ref[:, sl] = ((q[:, sl] * cos + qs[:, sl] * sin) * mla_scale).astype(BF16)
        km_ref[:, sl] = (kn[:, sl] + kr).astype(BF16)
        vm_ref[sl, :] = v[:, sl].T.astype(BF16)
    qn = proj(_C_Q, _C_KVC) * (NSA_HD ** -0.5 * LOG2E)
    qnat_ref[...] = qn.astype(BF16)
    lo = lax.broadcasted_iota(I32, (qn.shape[0], LANE), 1) < HALF
    for j in range(NSA_HEADS // 2):
        blk = qn[:, j * LANE:(j + 1) * LANE]
        qn_ref[:, (2 * j) * LANE:(2 * j + 1) * LANE] = jnp.where(lo, blk, 0.0).astype(BF16)
        qn_ref[:, (2 * j + 1) * LANE:(2 * j + 2) * LANE] = jnp.where(lo, 0.0, blk).astype(BF16)
    zc = proj(_C_KVC, _C_KVS)
    kvc_ref[...] = zc
    kvct_ref[...] = zc.T
    for src, dst, dst_t, var, vart in ((_C_KVS, kvs_ref, kvst_ref, kvsv_ref, kvsvt_ref),
                                       (_C_KVW, kvw_ref, kvwt_ref, kvwv_ref, kvwvt_ref)):
        z = proj(src, src + NSA_KV_COLS)
        dst[...] = z
        dst_t[...] = z.T
        for i, t in enumerate(_kv_variants(z[:, :LANE], z[:, LANE:])):
            var[:, i * LANE:(i + 1) * LANE] = t.astype(BF16)
            vart[i * LANE:(i + 1) * LANE, :] = t.T.astype(BF16)
    gn_ref[...] = _sigmoid(proj(_C_GN, _C_GA))
    d = ga_ref.shape[1]
    ga_ref[...] = _sigmoid(proj(_C_GA, _C_GA + d)).astype(BF16)
    gb_ref[...] = _sigmoid(proj(_C_GA + d, _C_GA + 2 * d)).astype(BF16)


def _in_proj(x, cs, pw, tm, seq):
    n, d = x.shape
    per_seq = seq // tm
    n_cs = cs.shape[0] // tm
    row = lambda i: (i, 0)
    full = lambda i: (0, 0)
    wcols = pw['w_all'].shape[1]
    outs = [
        ('qm', 8 * LANE, BF16), ('km', 8 * LANE, BF16), ('vm', 8 * LANE, BF16),
        ('ckv', MLA_KV_RANK, F32), ('kr', MLA_ROPE, F32), ('qn', 8 * LANE, BF16),
        ('kvc', NSA_KV_COLS, F32), ('kvs', NSA_KV_COLS, F32), ('kvw', NSA_KV_COLS, F32),
        ('kvsv', 4 * LANE, BF16), ('kvwv', 4 * LANE, BF16), ('gn', LANE, F32),
        ('ga', d, BF16), ('gb', d, BF16), ('qnat', NSA_HEADS * NSA_HD, BF16),
        ('kvsvt', 4 * LANE, BF16), ('kvwvt', 4 * LANE, BF16),
        ('kvct', NSA_KV_COLS, F32), ('kvst', NSA_KV_COLS, F32), ('kvwt', NSA_KV_COLS, F32)]
    transposed = ('vm', 'kvsvt', 'kvwvt')
    by_seq = ('kvct', 'kvst', 'kvwt')

    def out_spec(k, c):
        if k in transposed:
            return pl.BlockSpec((c, tm), lambda i: (0, i))
        if k in by_seq:
            return pl.BlockSpec((None, c, tm), lambda i: (i // per_seq, 0, i % per_seq))
        return pl.BlockSpec((tm, c), row)

    def out_shape(k, c, t):
        shape = (c, n) if k in transposed else (n // seq, c, seq) if k in by_seq else (n, c)
        return jax.ShapeDtypeStruct(shape, t)

    res = pl.pallas_call(
        functools.partial(_inproj_kernel, mla_scale=(MLA_NOPE + MLA_ROPE) ** -0.5 * LOG2E),
        grid=(n // tm,),
        in_specs=[pl.BlockSpec((tm, d), row),
                  pl.BlockSpec((tm, 2 * LANE), lambda i: (i % n_cs, 0)),
                  pl.BlockSpec((d, wcols), full),
                  pl.BlockSpec((MLA_Q_RANK, 8 * LANE), full),
                  pl.BlockSpec((MLA_Q_RANK, 8 * LANE), full),
                  pl.BlockSpec((MLA_KV_RANK, 8 * LANE), full),
                  pl.BlockSpec((MLA_KV_RANK, 8 * LANE), full),
                  pl.BlockSpec((1, MLA_Q_RANK), full),
                  pl.BlockSpec((1, MLA_KV_RANK), full)],
        out_specs=[out_spec(k, c) for k, c, _ in outs],
        out_shape=[out_shape(k, c, t) for k, c, t in outs],
        compiler_params=_cparams(("parallel",)),
        name="in_proj",
    )(x, cs, pw['w_all'], pw['wuq'], pw['wuqs'], pw['wuk'], pw['wuv'], pw['gq'], pw['gkv'])
    return {k: v for (k, _, _), v in zip(outs, res)}


def _flash_kernel(*refs, mode, t, alibi, swap):
    if mode == 'select':
        slopes_ref, q_ref, k_ref, vt_ref, selt_ref, et_ref, o_ref, m_scr, l_scr, acc_scr = refs
    else:
        slopes_ref, q_ref, k_ref, vt_ref, o_ref, m_scr, l_scr, acc_scr = refs
    hp = pl.program_id(1)
    qi = pl.program_id(2)
    q0 = qi * t
    krow = lax.broadcasted_iota(I32, (t, t), 0)
    dmat = lax.broadcasted_iota(I32, (t, t), 1) - krow
    krow_f = krow.astype(F32)
    qs = [q_ref[:, e * LANE:(e + 1) * LANE] for e in range(2)]
    for e in range(2):
        m_scr[e] = jnp.full((1, t), NEG, F32)
        l_scr[e] = jnp.zeros((1, t), F32)
        acc_scr[e] = jnp.zeros((LANE, t), F32)

    def step(kt, edge):
        k0 = pl.multiple_of(kt * t, t)
        ok = None
        if edge:
            dist = dmat + (q0 - k0)
            ok = dist <= WINDOW if edge == 'low' else dist >= 0
            if edge == 'both':
                ok = ok & (dist <= WINDOW)
        if mode == 'select':
            picked = _dot(et_ref[pl.ds(k0, t), :], selt_ref[...]) > 0.5
            ok = picked if ok is None else ok & picked
        for e in range(2):
            k = k_ref[pl.ds(k0, t), e * LANE:(e + 1) * LANE]
            vt = vt_ref[e * LANE:(e + 1) * LANE, pl.ds(k0, t)]
            shift = 0.0
            if alibi:
                slope = slopes_ref[2 * hp + e] * LOG2E
                shift = slope * (k0 - q0).astype(F32)
            s = _dot_nt(k, qs[e])
            if alibi:
                s = s + slope * krow_f
            if ok is not None:
                s = jnp.where(ok, s, NEG)
            m_prev = m_scr[e]
            m_new = jnp.maximum(m_prev, jnp.max(s, axis=0, keepdims=True) + shift)
            alpha = jnp.exp2(m_prev - m_new)
            p = jnp.exp2(s - (m_new - shift))
            l_scr[e] = alpha * l_scr[e] + jnp.sum(p, axis=0, keepdims=True)
            acc_scr[e] = alpha * acc_scr[e] + _dot(vt, p.astype(BF16))
            m_scr[e] = m_new

    def loop(lo, hi, edge):
        n = hi - lo

        def pair(j, c):
            step(lo + 2 * j, edge)
            step(lo + 2 * j + 1, edge)
            return c

        lax.fori_loop(0, n >> 1, pair, 0)

        @pl.when((n & 1) == 1)
        def _():
            step(hi - 1, edge)

    if mode == 'window' and t == WINDOW:
        @pl.when(qi > 0)
        def _():
            step(qi - 1, 'low')
        step(qi, 'diag')
    elif mode == 'window':
        loop(jnp.maximum(q0 - WINDOW, 0) // t, qi + 1, 'both')
    else:
        loop(0, qi, False)
        step(qi, 'diag')
    o0 = acc_scr[0] / l_scr[0]
    o1 = acc_scr[1] / l_scr[1]
    lo = lax.broadcasted_iota(I32, (LANE, t), 0) < HALF
    o_ref[...] = (jnp.where(lo, o1, o0) if swap else jnp.where(lo, o0, o1)).T.astype(o_ref.dtype)


def _flash(q, k, vt, slopes, b, s, *, mode, kv_shared, alibi, swap, selt=None, emat_t=None):
    tq = min(512, s)
    nq = s // tq
    k_idx = (lambda bi, hp, qi: (bi, hp // 2)) if kv_shared else (lambda bi, hp, qi: (bi, hp))
    vt_idx = (lambda bi, hp, qi: (hp // 2, bi)) if kv_shared else (lambda bi, hp, qi: (hp, bi))
    in_specs = [pl.BlockSpec(memory_space=pltpu.SMEM),
                pl.BlockSpec((tq, 2 * LANE), lambda bi, hp, qi: (bi * nq + qi, hp)),
                pl.BlockSpec((s, 2 * LANE), k_idx),
                pl.BlockSpec((2 * LANE, s), vt_idx)]
    args = [slopes, q, k, vt]
    if mode == 'select':
        in_specs += [pl.BlockSpec((None, None, LANE, tq), lambda bi, hp, qi: (bi, hp // 2, 0, qi)),
                     pl.BlockSpec((s, LANE), lambda bi, hp, qi: (0, 0))]
        args += [selt, emat_t]
    return pl.pallas_call(
        functools.partial(_flash_kernel, mode=mode, t=tq, alibi=alibi, swap=swap),
        grid=(b, 4, nq),
        in_specs=in_specs,
        out_specs=pl.BlockSpec((tq, LANE), lambda bi, hp, qi: (bi * nq + qi, hp)),
        out_shape=jax.ShapeDtypeStruct((b * s, 4 * LANE), BF16),
        scratch_shapes=[pltpu.VMEM((2, 1, tq), F32), pltpu.VMEM((2, 1, tq), F32),
                        pltpu.VMEM((2, LANE, tq), F32)],
        compiler_params=_cparams(("parallel", "parallel", "arbitrary")),
        name="flash_" + mode,
    )(*args)


def _cmp_first_linear(lo, hi, w_ref, o_ref, r0=0, r=None):
    r = o_ref.shape[0] if r is None else r
    acc = jnp.zeros((r, o_ref.shape[1]), F32)
    for p in range(CMP_STRIDE):
        rows = pl.ds(r0 * CMP_STRIDE + p, r, stride=CMP_STRIDE)
        x = jnp.concatenate([lo[rows, :], hi[rows, :]], axis=1)
        acc = acc + _dot(x.astype(BF16), w_ref[p])
    o_ref[r0:r0 + r, :] = acc


def _cmp_fs_kernel(lo, hi, w_ref, o_ref):
    _cmp_first_linear(lo, hi, w_ref, o_ref)


def _cmp_fs_dense(x, w_big):
    n, c = x.shape
    tr = min(256, n // CMP_STRIDE)
    return pl.pallas_call(
        _cmp_fs_kernel,
        grid=(n // CMP_STRIDE // tr,),
        in_specs=[pl.BlockSpec((tr * CMP_STRIDE, LANE), lambda i: (i, 0)),
                  pl.BlockSpec((tr * CMP_STRIDE, LANE), lambda i: (i, 1)),
                  pl.BlockSpec(w_big.shape, lambda i: (0, 0, 0))],
        out_specs=pl.BlockSpec((tr, 4 * LANE), lambda i: (i, 0)),
        out_shape=jax.ShapeDtypeStruct((n // CMP_STRIDE, 4 * LANE), F32),
        compiler_params=_cparams(("parallel",)),
        name="cmp_fs",
    )(x, x, w_big)


def _prefetched_pages(pt_ref, srcs, bufs, sems, n_pages):
    bi = pl.program_id(0)
    slot = bi % 2

    def copies(seq, s, j):
        pg = pt_ref[seq, j]
        return [pltpu.make_async_copy(src.at[pg], buf.at[s, j], sems.at[s]) for src, buf in zip(srcs, bufs)]

    def start_all(seq, s):
        def body(j, c):
            for cp in copies(seq, s, j):
                cp.start()
            return c
        lax.fori_loop(0, n_pages, body, 0, unroll=DMA_UNROLL)

    @pl.when(bi == 0)
    def _():
        start_all(0, 0)

    @pl.when(bi + 1 < pl.num_programs(0))
    def _():
        start_all(bi + 1, 1 - slot)

    def wait(j, c):
        for cp in copies(bi, slot, j):
            cp.wait()
        return c

    lax.fori_loop(0, n_pages, wait, 0, unroll=DMA_UNROLL)
    return slot


def _cmp_fs_paged_kernel(pt_ref, cache_hbm, w_ref, o_ref, pbuf, sems, lo, hi, *, n_pages):
    slot = _prefetched_pages(pt_ref, (cache_hbm,), (pbuf,), sems, n_pages)

    groups = 4 if n_pages % 4 == 0 else 1
    per = n_pages // groups
    for g in range(groups):
        for j in range(g * per, (g + 1) * per):
            lo[j * PAGE:(j + 1) * PAGE, :] = pbuf[slot, j, :LANE, :].T
            hi[j * PAGE:(j + 1) * PAGE, :] = pbuf[slot, j, LANE:, :].T
        blocks = per * (PAGE // CMP_STRIDE)
        _cmp_first_linear(lo, hi, w_ref, o_ref, g * blocks, blocks)


def _cmp_fs_paged(cache_t, page_table, w_big):
    b, n_pages = page_table.shape
    rows = n_pages * (PAGE // CMP_STRIDE)
    return pl.pallas_call(
        functools.partial(_cmp_fs_paged_kernel, n_pages=n_pages),
        grid_spec=pltpu.PrefetchScalarGridSpec(
            num_scalar_prefetch=1, grid=(b,),
            in_specs=[pl.BlockSpec(memory_space=pl.ANY),
                      pl.BlockSpec(w_big.shape, lambda bi, pt: (0, 0, 0))],
            out_specs=pl.BlockSpec((rows, 4 * LANE), lambda bi, pt: (bi, 0)),
            scratch_shapes=[pltpu.VMEM((2, n_pages, NSA_KV_COLS, PAGE), F32), pltpu.SemaphoreType.DMA((2,)),
                            pltpu.VMEM((n_pages * PAGE, LANE), F32), pltpu.VMEM((n_pages * PAGE, LANE), F32)]),
        out_shape=jax.ShapeDtypeStruct((b * rows, 4 * LANE), F32),
        compiler_params=_cparams(("arbitrary",)),
        name="cmp_fs_paged",
    )(page_table, cache_t, w_big)


def _gelu_tanh(x):
    return 0.5 * x * (1.0 + jnp.tanh(0.7978845608028654 * (x + 0.044715 * x * x * x)))


def _cmp_pe_kernel(pe_ref, w1_ref, o_ref):
    o_ref[...] = _split_dot(pe_ref[...], w1_ref[0], w1_ref[1])


def _cmp_pe(pw):
    return pl.pallas_call(
        _cmp_pe_kernel,
        grid=(1,),
        in_specs=[pl.BlockSpec(pw['pe_flat'].shape, lambda i: (0, 0)),
                  pl.BlockSpec(pw['w1_pe'].shape, lambda i: (0, 0, 0))],
        out_specs=pl.BlockSpec((1, 2 * LANE), lambda i: (0, 0)),
        out_shape=jax.ShapeDtypeStruct((1, 2 * LANE), F32),
        compiler_params=_cparams(("arbitrary",)),
        name="cmp_pe",
    )(pw['pe_flat'], pw['w1_pe'])


def _cmp_finish_kernel(fs_ref, pe_ref, w2_ref, o_ref, ot_ref):
    fs = fs_ref[...]
    n = fs.shape[0]
    h = _gelu_tanh(fs[:, :2 * LANE] + pltpu.roll(fs[:, 2 * LANE:], n - 1, 0) + pe_ref[...])
    kc = _dot(h.astype(BF16), w2_ref[...])
    for i, t in enumerate(_kv_variants(kc[:, :LANE], kc[:, LANE:])):
        o_ref[:, i * LANE:(i + 1) * LANE] = t.astype(BF16)
        ot_ref[i * LANE:(i + 1) * LANE, :] = t.T.astype(BF16)


def _cmp_finish(fs, nb, pe, pw):
    r = fs.shape[0]
    full2 = lambda i: (0, 0)
    return pl.pallas_call(
        _cmp_finish_kernel,
        grid=(r // nb,),
        in_specs=[pl.BlockSpec((nb, 4 * LANE), lambda i: (i, 0)),
                  pl.BlockSpec(pe.shape, full2),
                  pl.BlockSpec(pw['w2_cmp'].shape, full2)],
        out_specs=[pl.BlockSpec((nb, 4 * LANE), lambda i: (i, 0)),
                   pl.BlockSpec((None, 4 * LANE, nb), lambda i: (i, 0, 0))],
        out_shape=[jax.ShapeDtypeStruct((r, 4 * LANE), BF16),
                   jax.ShapeDtypeStruct((r // nb, 4 * LANE, nb), BF16)],
        compiler_params=_cparams(("parallel",)),
        name="cmp_finish",
    )(fs, pe, pw['w2_cmp'])


def _masked_softmax(s, ok, axis=-1):
    s = jnp.where(ok, s, -jnp.inf)
    m = jnp.max(s, axis=axis, keepdims=True)
    m = jnp.where(m > -jnp.inf, m, 0.0)
    e = jnp.exp2(s - m)
    d = jnp.sum(e, axis=axis, keepdims=True)
    return e / jnp.where(d > 0.0, d, 1.0)


def _select_rank(imp, cur, n_slc):
    r, w = imp.shape
    blk = lax.broadcasted_iota(I32, (r, w), 1)
    valid = (blk <= cur) & (blk < n_slc)
    forced = (blk == 0) | (valid & (blk > cur - N_LOCAL_SLC))
    score = jnp.where(forced, FORCE_SCORE, jnp.where(valid, imp, -1.0))
    score = jnp.where(blk < n_slc, score, -2.0)

    def body(i, rank):
        col = jnp.sum(jnp.where(blk == i, score, 0.0), axis=1, keepdims=True)
        beats = (col > score) | ((col == score) & (i < blk))
        return rank + jnp.where(beats, 1.0, 0.0)

    rank = lax.fori_loop(0, n_slc, body, jnp.zeros((r, w), F32), unroll=True if n_slc <= 32 else 4)
    return rank, valid


def _cmp_select_kernel(slopes_ref, q_ref, kc_ref, kct_ref, wmapt_ref, o_ref, sel_ref, *, tq, n_slc):
    kvh = pl.program_id(1)
    q0 = pl.program_id(2) * tq
    nk = kc_ref.shape[0]
    qpos = q0 + lax.broadcasted_iota(I32, (nk, tq), 1)
    kend = lax.broadcasted_iota(I32, (nk, tq), 0) * CMP_STRIDE + (CMP_BLOCK - 1)
    ok = kend <= qpos
    dist = (qpos - kend).astype(F32)
    psum = jnp.zeros((nk, tq), F32)
    outs = []
    for g in range(NSA_GQA):
        sl = slice((g % 2) * LANE, (g % 2 + 1) * LANE)
        s = _dot_nt(kc_ref[:, sl], q_ref[:, g * LANE:(g + 1) * LANE]) - (slopes_ref[kvh * NSA_GQA + g] * LOG2E) * dist
        p = _masked_softmax(s, ok, axis=0)
        psum = psum + p
        outs.append(_dot(kct_ref[sl, :], p.astype(BF16)))
    lo = lax.broadcasted_iota(I32, (LANE, tq), 0) < HALF
    o_ref[:, :LANE] = jnp.where(lo, outs[1], outs[0]).T.astype(o_ref.dtype)
    o_ref[:, LANE:] = jnp.where(lo, outs[3], outs[2]).T.astype(o_ref.dtype)
    ps_hi = psum.astype(BF16)
    ps_lo = (psum - ps_hi.astype(F32)).astype(BF16)
    imp = _dot(wmapt_ref[...], ps_hi) + _dot(wmapt_ref[...], ps_lo)
    rows = -(-n_slc // 8) * 8
    imp = imp[:rows]
    blk = lax.broadcasted_iota(I32, (rows, tq), 0)
    cur = (q0 + lax.broadcasted_iota(I32, (1, tq), 1)) >> SLC_SHIFT
    valid = (blk <= cur) & (blk < n_slc)
    forced = (blk == 0) | (valid & (blk > cur - N_LOCAL_SLC))
    score = jnp.where(forced, FORCE_SCORE, jnp.where(valid, imp, -1.0))
    rank = jnp.zeros((rows, tq), F32)
    for i in range(n_slc):
        other = score[i:i + 1, :]
        rank = rank + jnp.where((other > score) | ((other == score) & (i < blk)), 1.0, 0.0)
    sel = jnp.where((rank < SLC_TOP_N) & valid, 1.0, 0.0)
    sel_ref[...] = jnp.concatenate([sel, jnp.zeros((LANE - rows, tq), F32)], axis=0).astype(sel_ref.dtype)


def _cmp_select(qn, kc_var, kc_var_t, slopes, wmap_t, b, s, n_slc):
    tq = min(256, s)
    nq = s // tq
    nk = kc_var.shape[0] // b
    assert n_slc <= LANE
    return pl.pallas_call(
        functools.partial(_cmp_select_kernel, tq=tq, n_slc=n_slc),
        grid=(b, NSA_KV_HEADS, nq),
        in_specs=[pl.BlockSpec(memory_space=pltpu.SMEM),
                  pl.BlockSpec((tq, 4 * LANE), lambda bi, kh, qi: (bi * nq + qi, kh)),
                  pl.BlockSpec((nk, 2 * LANE), lambda bi, kh, qi: (bi, kh)),
                  pl.BlockSpec((None, 2 * LANE, nk), lambda bi, kh, qi: (bi, kh, 0)),
                  pl.BlockSpec(wmap_t.shape, lambda bi, kh, qi: (0, 0))],
        out_specs=[pl.BlockSpec((tq, 2 * LANE), lambda bi, kh, qi: (bi * nq + qi, kh)),
                   pl.BlockSpec((None, None, LANE, tq), lambda bi, kh, qi: (bi, kh, 0, qi))],
        out_shape=[jax.ShapeDtypeStruct((b * s, 4 * LANE), BF16),
                   jax.ShapeDtypeStruct((b, NSA_KV_HEADS, LANE, s), BF16)],
        compiler_params=_cparams(("parallel", "parallel", "parallel")),
        name="cmp_select",
    )(slopes, qn, kc_var, kc_var_t, wmap_t)


def _layer_norm(x, g, b):
    mu = jnp.mean(x, axis=-1, keepdims=True)
    xc = x - mu
    var = jnp.mean(xc * xc, axis=-1, keepdims=True)
    return xc * lax.rsqrt(var + LN_EPS) * g + b


def _merge_kernel(x_ref, omla_ref, ocmp_ref, oslc_ref, owin_ref, gn_ref, ga_ref, gb_ref,
                  wo_ref, wn_ref, wout_ref, e3_ref, g1_ref, b1_ref, wr_ref, br_ref,
                  h_ref, rt_ref):
    gn = gn_ref[...]
    w = 4 * LANE
    ge = _split_dot(gn, e3_ref[...])
    o_nsa = (ge[:, :w] * ocmp_ref[...].astype(F32) + ge[:, w:2 * w] * oslc_ref[...].astype(F32)
             + ge[:, 2 * w:] * owin_ref[...].astype(F32))
    u = (ga_ref[...].astype(F32) * _dot(omla_ref[...], wo_ref[...])
         + gb_ref[...].astype(F32) * _dot(o_nsa.astype(BF16), wn_ref[...]))
    h = _layer_norm(DEEPNORM_ALPHA * x_ref[...] + _dot(u.astype(BF16), wout_ref[...]),
                    g1_ref[...], b1_ref[...])
    h_ref[...] = h
    lg = _split_dot(h, wr_ref[0], wr_ref[1]) + br_ref[...]
    tm = lg.shape[0]
    lane = lax.broadcasted_iota(I32, (tm, LANE), 1)
    is_g = lane < N_GROUPS
    gm = jnp.max(jnp.where(is_g, lg, -jnp.inf), axis=1, keepdims=True)
    gidx = jnp.min(jnp.where(is_g & (lg == gm), lane, LANE), axis=1, keepdims=True)
    p_group = 1.0 / jnp.sum(jnp.where(is_g, jnp.exp(lg - gm), 0.0), axis=1, keepdims=True)
    in_g = (lane >= HALF) & (((lane - HALF) >> 3) == gidx)
    m1 = jnp.max(jnp.where(in_g, lg, -jnp.inf), axis=1, keepdims=True)
    i1 = jnp.min(jnp.where(in_g & (lg == m1), lane, 2 * LANE), axis=1, keepdims=True)
    rest = in_g & (lane != i1)
    m2 = jnp.max(jnp.where(rest, lg, -jnp.inf), axis=1, keepdims=True)
    i2 = jnp.min(jnp.where(rest & (lg == m2), lane, 2 * LANE), axis=1, keepdims=True)
    t = jnp.exp(m2 - m1)
    g1 = p_group / (1.0 + t)
    g2 = p_group * t / (1.0 + t)
    rt = jnp.where(lane == 0, (i1 - HALF).astype(F32),
                   jnp.where(lane == 1, (i2 - HALF).astype(F32),
                             jnp.where(lane == 2, g1, jnp.where(lane == 3, g2, 0.0))))
    rt_ref[...] = rt


def _merge(x, omla, ocmp, oslc, owin, z, pw, tm):
    n, d = x.shape
    row = lambda i: (i, 0)
    full = lambda i: (0, 0)
    w = 4 * LANE
    return pl.pallas_call(
        _merge_kernel,
        grid=(n // tm,),
        in_specs=[pl.BlockSpec((tm, d), row), pl.BlockSpec((tm, w), row), pl.BlockSpec((tm, w), row),
                  pl.BlockSpec((tm, w), row), pl.BlockSpec((tm, w), row), pl.BlockSpec((tm, LANE), row),
                  pl.BlockSpec((tm, d), row), pl.BlockSpec((tm, d), row),
                  pl.BlockSpec((w, d), full), pl.BlockSpec((w, d), full), pl.BlockSpec((d, d), full),
                  pl.BlockSpec((LANE, 3 * w), full), pl.BlockSpec((1, d), full), pl.BlockSpec((1, d), full),
                  pl.BlockSpec((2, d, LANE), lambda i: (0, 0, 0)), pl.BlockSpec((1, LANE), full)],
        out_specs=[pl.BlockSpec((tm, d), row), pl.BlockSpec((tm, LANE), row)],
        out_shape=[jax.ShapeDtypeStruct((n, d), F32), jax.ShapeDtypeStruct((n, LANE), F32)],
        compiler_params=_cparams(("parallel",)),
        name="merge_router",
    )(x, omla, ocmp, oslc, owin, z['gn'], z['ga'], z['gb'], pw['w_o'], pw['w_on'], pw['w_out'],
      pw['e3'], pw['ln1_g'], pw['ln1_b'], pw['w_router'], pw['b_router'])


def _row_copy(src, dst, sem, src_row, dst_row):
    return pltpu.make_async_copy(src.at[pl.ds(src_row, 1)], dst.at[pl.ds(dst_row, 1)], sem)


def _moe_rank_kernel(rt_ref, tri_ref, rank_ref, cnt_ref, run_scr):
    @pl.when(pl.program_id(0) == 0)
    def _():
        run_scr[...] = jnp.zeros_like(run_scr)

    rt = rt_ref[...]
    lane = lax.broadcasted_iota(I32, rt.shape, 1)
    lanef = lane.astype(F32)
    e0 = jnp.sum(jnp.where(lane == 0, rt, 0.0), axis=1, keepdims=True)
    e1 = jnp.sum(jnp.where(lane == 1, rt, 0.0), axis=1, keepdims=True)
    hit0 = lanef == e0
    hit1 = lanef == e1
    onehot = jnp.where(hit0 | hit1, 1.0, 0.0)
    before = _dot(tri_ref[...], onehot.astype(BF16)) + run_scr[...]
    r0 = jnp.sum(jnp.where(hit0, before, 0.0), axis=1, keepdims=True)
    r1 = jnp.sum(jnp.where(hit1, before, 0.0), axis=1, keepdims=True)
    rank_ref[...] = jnp.where(lane == 0, r0, jnp.where(lane == 1, r1, 0.0))
    run = run_scr[...] + jnp.sum(onehot, axis=0, keepdims=True)
    run_scr[...] = run
    cnt_ref[...] = run


def _moe_rank(rt):
    n = rt.shape[0]
    tm = min(512, n)
    tri = jnp.asarray(np.tril(np.ones((tm, tm), np.float32), -1), BF16)
    return pl.pallas_call(
        _moe_rank_kernel,
        grid=(n // tm,),
        in_specs=[pl.BlockSpec((tm, LANE), lambda i: (i, 0)), pl.BlockSpec((tm, tm), lambda i: (0, 0))],
        out_specs=[pl.BlockSpec((tm, LANE), lambda i: (i, 0)), pl.BlockSpec((1, LANE), lambda i: (0, 0))],
        out_shape=[jax.ShapeDtypeStruct((n, LANE), F32), jax.ShapeDtypeStruct((1, LANE), F32)],
        scratch_shapes=[pltpu.VMEM((1, LANE), F32)],
        compiler_params=_cparams(("arbitrary",)),
        name="moe_rank",
    )(rt, tri)


def _moe_dispatch_kernel(dest_ref, h_ref, xs_in, xs_out, sem, *, tm):
    del xs_in

    def copies(t):
        return [_row_copy(h_ref, xs_out, sem, t, dest_ref[0, 0, TOP_K * t + k]) for k in range(TOP_K)]

    def start(t, c):
        for cp in copies(t):
            cp.start()
        return c

    def wait(t, c):
        for cp in copies(t):
            cp.wait()
        return c

    lax.fori_loop(0, tm, start, 0, unroll=DMA_UNROLL)
    lax.fori_loop(0, tm, wait, 0, unroll=DMA_UNROLL)


def _moe_dispatch(h, dest, n_slots):
    n, d = h.shape
    tm = min(256, n)
    return pl.pallas_call(
        functools.partial(_moe_dispatch_kernel, tm=tm),
        grid=(n // tm,),
        in_specs=[pl.BlockSpec((1, 1, TOP_K * tm), lambda i: (i, 0, 0), memory_space=pltpu.SMEM),
                  pl.BlockSpec((tm, d), lambda i: (i, 0)),
                  pl.BlockSpec(memory_space=pl.ANY)],
        out_specs=pl.BlockSpec(memory_space=pl.ANY),
        out_shape=jax.ShapeDtypeStruct((n_slots, d), F32),
        scratch_shapes=[pltpu.SemaphoreType.DMA(())],
        input_output_aliases={2: 0},
        compiler_params=_cparams(("arbitrary",)),
        name="moe_dispatch",
    )(dest.reshape(n // tm, 1, TOP_K * tm), h, jnp.zeros((n_slots, d), F32))


def _moe_expert_kernel(be_ref, x_ref, w1_ref, w3_ref, w2_ref, y_ref):
    del be_ref
    xb = x_ref[...].astype(BF16)
    a = _dot(xb, w1_ref[...].astype(BF16))
    hmid = a * _sigmoid(a) * _dot(xb, w3_ref[...].astype(BF16))
    y_ref[...] = _dot(hmid.astype(BF16), w2_ref[...].astype(BF16))


def _moe_experts(xs, blk_expert, w1, w3, w2):
    d = xs.shape[1]
    n_blocks = blk_expert.shape[0]
    de = w1.shape[-1]
    return pl.pallas_call(
        _moe_expert_kernel,
        grid_spec=pltpu.PrefetchScalarGridSpec(
            num_scalar_prefetch=1, grid=(n_blocks,),
            in_specs=[pl.BlockSpec((MOE_BLOCK, d), lambda i, be: (i, 0)),
                      pl.BlockSpec((None, None, d, de), lambda i, be: (0, be[i], 0, 0)),
                      pl.BlockSpec((None, None, d, de), lambda i, be: (0, be[i], 0, 0)),
                      pl.BlockSpec((None, None, de, d), lambda i, be: (0, be[i], 0, 0))],
            out_specs=pl.BlockSpec((MOE_BLOCK, d), lambda i, be: (i, 0))),
        out_shape=jax.ShapeDtypeStruct((n_blocks * MOE_BLOCK, d), F32),
        compiler_params=_cparams(("parallel",)),
        name="moe_experts",
    )(blk_expert, xs, w1, w3, w2)


def _moe_combine_kernel(dest_ref, nxt_ref, y_hbm, h_ref, rt_ref, g2_ref, b2_ref, o_ref, ybuf, sems, *, tm):
    i = pl.program_id(0)
    slot = i % 2

    def copies(ref, s, t):
        return [_row_copy(y_hbm, ybuf.at[s, k], sems.at[s], ref[0, 0, TOP_K * t + k], t) for k in range(TOP_K)]

    def start_all(ref, s):
        def body(t, c):
            for cp in copies(ref, s, t):
                cp.start()
            return c
        lax.fori_loop(0, tm, body, 0, unroll=DMA_UNROLL)

    @pl.when(i == 0)
    def _():
        start_all(dest_ref, 0)

    @pl.when(i + 1 < pl.num_programs(0))
    def _():
        start_all(nxt_ref, 1 - slot)

    def wait(t, c):
        for cp in copies(dest_ref, slot, t):
            cp.wait()
        return c

    lax.fori_loop(0, tm, wait, 0, unroll=DMA_UNROLL)
    y0 = ybuf[slot, 0]
    y1 = ybuf[slot, 1]
    rt = rt_ref[...]
    lane = lax.broadcasted_iota(I32, rt.shape, 1)
    g1 = jnp.sum(jnp.where(lane == 2, rt, 0.0), axis=1, keepdims=True)
    g2 = jnp.sum(jnp.where(lane == 3, rt, 0.0), axis=1, keepdims=True)
    moe = g1 * y0 + g2 * y1
    o_ref[...] = _layer_norm(DEEPNORM_ALPHA * h_ref[...] + moe, g2_ref[...], b2_ref[...])


def _moe_combine(y, dest, h, rt, ln_g, ln_b):
    n, d = h.shape
    tm = min(256, n)
    nt = n // tm
    row = lambda i: (i, 0)
    full = lambda i: (0, 0)
    dest3 = dest.reshape(nt, 1, TOP_K * tm)
    return pl.pallas_call(
        functools.partial(_moe_combine_kernel, tm=tm),
        grid=(nt,),
        in_specs=[pl.BlockSpec((1, 1, TOP_K * tm), lambda i: (i, 0, 0), memory_space=pltpu.SMEM),
                  pl.BlockSpec((1, 1, TOP_K * tm), lambda i: (jnp.minimum(i + 1, nt - 1), 0, 0),
                               memory_space=pltpu.SMEM),
                  pl.BlockSpec(memory_space=pl.ANY),
                  pl.BlockSpec((tm, d), row), pl.BlockSpec((tm, LANE), row),
                  pl.BlockSpec((1, d), full), pl.BlockSpec((1, d), full)],
        out_specs=pl.BlockSpec((tm, d), row),
        out_shape=jax.ShapeDtypeStruct((n, d), F32),
        scratch_shapes=[pltpu.VMEM((2, TOP_K, tm, d), F32), pltpu.SemaphoreType.DMA((2,))],
        compiler_params=_cparams(("arbitrary",)),
        name="moe_combine",
    )(dest3, dest3, y, h, rt, ln_g, ln_b)


def _moe(h, rt, pw, w1, w3, w2):
    n = h.shape[0]
    a = n * TOP_K
    rank, cnt = _moe_rank(rt)
    counts = cnt[0, :N_EXPERTS].astype(I32)
    padded = (counts + MOE_BLOCK - 1) // MOE_BLOCK * MOE_BLOCK
    pad_end = jnp.cumsum(padded)
    pad_start = pad_end - padded
    expert = rt[:, :TOP_K].astype(I32)
    first_slot = jnp.sum(jnp.where(expert[..., None] == jnp.arange(N_EXPERTS), pad_start, 0), axis=-1)
    dest = (first_slot + rank[:, :TOP_K].astype(I32)).reshape(-1)
    n_blocks = -(-a // MOE_BLOCK) + N_EXPERTS
    blk_start = jnp.arange(n_blocks) * MOE_BLOCK
    blk_expert = jnp.minimum(jnp.sum(pad_end[None, :] <= blk_start[:, None], axis=1), N_EXPERTS - 1).astype(I32)
    xs = _moe_dispatch(h, dest, n_blocks * MOE_BLOCK)
    y = _moe_experts(xs, blk_expert, w1, w3, w2)
    return _moe_combine(y, dest, h, rt, pw['ln2_g'], pw['ln2_b'])


def _qlat_kernel(q_ref, w_ref, o_ref):
    o_ref[...] = _dot(q_ref[...], w_ref[...]).astype(o_ref.dtype)


def _q_latent(qm, wabs):
    b = qm.shape[0]
    return pl.pallas_call(
        _qlat_kernel,
        grid=(MLA_HEADS,),
        in_specs=[pl.BlockSpec((b, LANE), lambda h: (0, h)),
                  pl.BlockSpec((None, LANE, MLA_KV_RANK), lambda h: (h, 0, 0))],
        out_specs=pl.BlockSpec((b, MLA_KV_RANK), lambda h: (0, h)),
        out_shape=jax.ShapeDtypeStruct((b, MLA_HEADS * MLA_KV_RANK), BF16),
        compiler_params=_cparams(("parallel",)),
        name="q_latent",
    )(qm, wabs)


def _mla_decode_kernel(pt_ref, ql_ref, qlt_ref, qr_ref, cn_ref, kn_ref, c_hbm, k_hbm, o_ref,
                       cbuf, kbuf, sems, m_scr, l_scr, acc_scr, *, n_pages, pp):
    slot = _prefetched_pages(pt_ref, (c_hbm, k_hbm), (cbuf, kbuf), sems, n_pages)
    qlt = qlt_ref[...]
    qr = qr_ref[...]
    cn = cn_ref[...].astype(BF16).astype(F32)
    kn = kn_ref[...].astype(BF16).astype(F32)
    m_scr[...] = (jnp.sum(ql_ref[...].astype(F32) * cn, axis=1, keepdims=True)
                  + jnp.sum(qr.astype(F32) * kn, axis=1, keepdims=True))
    l_scr[...] = jnp.ones_like(l_scr)
    acc_scr[...] = jnp.broadcast_to(cn, acc_scr.shape)

    def chunk(ci, carry):
        cs = [cbuf[slot, ci * pp + j].astype(BF16) for j in range(pp)]
        s = jnp.concatenate([_dot(c, qlt).T[:MLA_HEADS] + _dot(qr, kbuf[slot, ci * pp + j].astype(BF16))
                             for j, c in enumerate(cs)], axis=1)
        m_prev = m_scr[...]
        m_new = jnp.maximum(m_prev, jnp.max(s, axis=1, keepdims=True))
        alpha = jnp.exp2(m_prev - m_new)
        p = jnp.exp2(s - m_new)
        l_scr[...] = alpha * l_scr[...] + jnp.sum(p, axis=1, keepdims=True)
        pb = p.astype(BF16)
        acc = alpha * acc_scr[...]
        for j, c in enumerate(cs):
            acc = acc + _dot(pb[:, j * PAGE:(j + 1) * PAGE], c)
        acc_scr[...] = acc
        m_scr[...] = m_new
        return carry

    lax.fori_loop(0, n_pages // pp, chunk, 0)
    o_ref[...] = (acc_scr[...] / l_scr[...]).astype(o_ref.dtype)


def _mla_decode(qlat, qr, ckv_new, kr_new, cache_ckv, cache_kr_t, page_table):
    b, n_pages = page_table.shape
    pp = min(64, n_pages)
    r = MLA_KV_RANK
    ql3 = qlat.reshape(b, MLA_HEADS, r)
    qlt = jnp.pad(jnp.transpose(ql3, (0, 2, 1)), ((0, 0), (0, 0), (0, LANE - MLA_HEADS)))
    per_b = lambda bi, pt: (bi, 0, 0)
    return pl.pallas_call(
        functools.partial(_mla_decode_kernel, n_pages=n_pages, pp=pp),
        grid_spec=pltpu.PrefetchScalarGridSpec(
            num_scalar_prefetch=1, grid=(b,),
            in_specs=[pl.BlockSpec((None, MLA_HEADS, r), per_b),
                      pl.BlockSpec((None, r, LANE), per_b),
                      pl.BlockSpec((None, MLA_HEADS, MLA_ROPE), per_b),
                      pl.BlockSpec((None, 1, r), per_b),
                      pl.BlockSpec((None, 1, MLA_ROPE), per_b),
                      pl.BlockSpec(memory_space=pl.ANY),
                      pl.BlockSpec(memory_space=pl.ANY)],
            out_specs=pl.BlockSpec((None, MLA_HEADS, r), per_b),
            scratch_shapes=[pltpu.VMEM((2, n_pages, PAGE, r), F32), pltpu.VMEM((2, n_pages, MLA_ROPE, PAGE), F32),
                            pltpu.SemaphoreType.DMA((2,)),
                            pltpu.VMEM((MLA_HEADS, 1), F32), pltpu.VMEM((MLA_HEADS, 1), F32),
                            pltpu.VMEM((MLA_HEADS, r), F32)]),
        out_shape=jax.ShapeDtypeStruct((b, MLA_HEADS, r), BF16),
        compiler_params=_cparams(("arbitrary",)),
        name="mla_decode",
    )(page_table, ql3, qlt, qr, ckv_new.reshape(b, 1, r), kr_new.reshape(b, 1, MLA_ROPE),
      cache_ckv, cache_kr_t)


def _oproj_kernel(o_ref, w_ref, out_ref):
    out_ref[...] = (_dot(o_ref[:, :MLA_KV_RANK], w_ref[:, :LANE])
                    + _dot(o_ref[:, MLA_KV_RANK:], w_ref[:, LANE:])).astype(out_ref.dtype)


def _mla_out_up(o_lat, wuv):
    b = o_lat.shape[0]
    return pl.pallas_call(
        _oproj_kernel,
        grid=(MLA_HEADS // 2,),
        in_specs=[pl.BlockSpec((b, 2 * MLA_KV_RANK), lambda j: (0, j)),
                  pl.BlockSpec((MLA_KV_RANK, 2 * LANE), lambda j: (0, j))],
        out_specs=pl.BlockSpec((b, LANE), lambda j: (0, j)),
        out_shape=jax.ShapeDtypeStruct((b, 4 * LANE), BF16),
        compiler_params=_cparams(("parallel",)),
        name="mla_out_up",
    )(o_lat, wuv)


def _attend8(q8, variants, bias, ok):
    hrow = lax.broadcasted_iota(I32, (MLA_HEADS, 1), 0)
    vsel = (hrow >> GQA_SHIFT) * 2 + (hrow & 1)
    s = jnp.zeros(bias.shape, F32)
    for c, kv in enumerate(variants):
        s = s + jnp.where(vsel == c, _dot_nt(q8, kv), 0.0)
    p = _masked_softmax(s + bias, ok)
    pb = p.astype(BF16)
    o = jnp.zeros((MLA_HEADS, LANE), F32)
    for c, kv in enumerate(variants):
        o = o + jnp.where(vsel == c, _dot(pb, kv), 0.0)
    return o, p


def _slope_col():
    h = lax.broadcasted_iota(I32, (NSA_HEADS, 1), 0)
    return jnp.exp2(-(h + 1).astype(F32) * (8.0 / NSA_HEADS)) * LOG2E


def _attend_t(q8, kts, vts, bias, ok, new_row):
    hrow = lax.broadcasted_iota(I32, (NSA_HEADS, 1), 0)
    first = (hrow >> GQA_SHIFT) == 0
    s = jnp.where(first, _dot(q8, kts[0]), _dot(q8, kts[1])) + bias
    s = jnp.where(ok, s, NEG)
    new8 = jnp.broadcast_to(new_row, (NSA_HEADS, new_row.shape[1])).astype(BF16).astype(F32)
    kk, vv = new8[:, :LANE], new8[:, LANE:]
    k_new = jnp.where(first, kk, pltpu.roll(kk, HALF, 1))[:, :NSA_HD]
    v_new = jnp.where(first, vv, pltpu.roll(vv, HALF, 1))[:, :NSA_HD]
    s_new = jnp.sum(q8.astype(F32) * k_new, axis=1, keepdims=True)
    m = jnp.maximum(jnp.max(s, axis=1, keepdims=True), s_new)
    p = jnp.exp2(s - m)
    p_new = jnp.exp2(s_new - m)
    l = jnp.sum(p, axis=1, keepdims=True) + p_new
    pb = p.astype(BF16)
    o = jnp.where(first, _dot_nt(pb, vts[0]), _dot_nt(pb, vts[1])) + p_new * v_new
    return o / l


def _sample_cmp_win_kernel(qx_ref, q_ref, kc_ref, wmap_ref, st_ref, kvw_ref, newt_ref,
                           ocmp_ref, imp_ref, owin_ref, wout_ref, *, past_len):
    q8 = qx_ref[...]
    slope = _slope_col()
    nk = kc_ref.shape[0]
    kend = lax.broadcasted_iota(I32, (NSA_HEADS, nk), 1) * CMP_STRIDE + (CMP_BLOCK - 1)
    ok = kend <= past_len
    bias = -slope * (past_len - kend).astype(F32)
    kc = kc_ref[...]
    o, p = _attend8(q8, [kc[:, i * LANE:(i + 1) * LANE] for i in range(4)], bias, ok)
    ocmp_ref[...] = o.astype(ocmp_ref.dtype)
    hrow = lax.broadcasted_iota(I32, (NSA_HEADS, 1), 0)
    for kh in range(NSA_KV_HEADS):
        psum = jnp.sum(jnp.where((hrow >> GQA_SHIFT) == kh, p, 0.0), axis=0, keepdims=True)
        imp_ref[kh:kh + 1, :] = _split_dot(psum, wmap_ref[...])
    st = st_ref[...]
    wbuf = st.shape[1]
    stb = st.astype(BF16)
    j = lax.broadcasted_iota(I32, (NSA_HEADS, wbuf), 1)
    dist = wbuf - j
    okw = (dist <= WINDOW) & (past_len - dist >= 0)
    owin_ref[...] = _attend_t(q_ref[...], [stb[:NSA_HD], stb[NSA_HD:2 * NSA_HD]],
                              [stb[2 * NSA_HD:3 * NSA_HD], stb[3 * NSA_HD:]],
                              -slope * dist.astype(F32), okw, kvw_ref[...])
    newt = newt_ref[...]
    pick = lax.broadcasted_iota(I32, newt.shape, 1) == pl.program_id(0)
    new_col = jnp.sum(jnp.where(pick, newt, 0.0), axis=1, keepdims=True)
    nblk = wbuf // LANE
    rolled = [pltpu.roll(st[:, k * LANE:(k + 1) * LANE], LANE - 1, 1) for k in range(nblk)]
    keep = lax.broadcasted_iota(I32, (st.shape[0], LANE), 1) < LANE - 1
    for k in range(nblk):
        nxt = rolled[k + 1] if k + 1 < nblk else jnp.broadcast_to(new_col, rolled[k].shape)
        wout_ref[:, k * LANE:(k + 1) * LANE] = jnp.where(keep, rolled[k], nxt)


def _sample_cmp_win(qx, q8, kc_var, wmap, state_t, kvw, past_len):
    b = qx.shape[0]
    nk = kc_var.shape[0] // b
    wbuf = state_t.shape[2]
    wcols = wmap.shape[1]
    per_b3 = lambda bi: (bi, 0, 0)
    return pl.pallas_call(
        functools.partial(_sample_cmp_win_kernel, past_len=past_len),
        grid=(b,),
        in_specs=[pl.BlockSpec((None, NSA_HEADS, LANE), per_b3),
                  pl.BlockSpec((None, NSA_HEADS, NSA_HD), per_b3),
                  pl.BlockSpec((nk, 4 * LANE), lambda bi: (bi, 0)),
                  pl.BlockSpec(wmap.shape, lambda bi: (0, 0)),
                  pl.BlockSpec((None, NSA_KV_COLS, wbuf), per_b3),
                  pl.BlockSpec((None, 1, NSA_KV_COLS), per_b3),
                  pl.BlockSpec((NSA_KV_COLS, b), lambda bi: (0, 0))],
        out_specs=[pl.BlockSpec((None, NSA_HEADS, LANE), per_b3),
                   pl.BlockSpec((None, NSA_KV_HEADS, wcols), per_b3),
                   pl.BlockSpec((None, NSA_HEADS, NSA_HD), per_b3),
                   pl.BlockSpec((None, NSA_KV_COLS, wbuf), per_b3)],
        out_shape=[jax.ShapeDtypeStruct((b, NSA_HEADS, LANE), F32),
                   jax.ShapeDtypeStruct((b, NSA_KV_HEADS, wcols), F32),
                   jax.ShapeDtypeStruct((b, NSA_HEADS, NSA_HD), F32),
                   jax.ShapeDtypeStruct((b, NSA_KV_COLS, wbuf), F32)],
        compiler_params=_cparams(("parallel",)),
        name="sample_cmp_win",
    )(qx.reshape(b, NSA_HEADS, LANE), q8, kc_var, wmap, state_t, kvw.reshape(b, 1, NSA_KV_COLS),
      jnp.transpose(kvw))


def _sample_select_kernel(imp_ref, idx_ref, *, cur, n_slc):
    imp = imp_ref[...]
    rank, valid = _select_rank(imp, cur, n_slc)
    r, w = imp.shape
    blk = lax.broadcasted_iota(I32, (r, w), 1).astype(F32)
    lane = lax.broadcasted_iota(I32, (r, LANE), 1)
    out = jnp.zeros((r, LANE), F32)
    for t in range(SLC_TOP_N):
        pick = jnp.sum(jnp.where((rank == t) & valid, blk, 0.0), axis=1, keepdims=True)
        out = jnp.where(lane == t, pick, out)
    idx_ref[...] = out.astype(I32)


def _sample_select(imp, cur, n_slc):
    r = imp.shape[0]
    return pl.pallas_call(
        functools.partial(_sample_select_kernel, cur=cur, n_slc=n_slc),
        grid=(1,),
        in_specs=[pl.BlockSpec(imp.shape, lambda i: (0, 0))],
        out_specs=pl.BlockSpec((r, LANE), lambda i: (0, 0)),
        out_shape=jax.ShapeDtypeStruct((r, LANE), I32),
        compiler_params=_cparams(("arbitrary",)),
        name="sample_select",
    )(imp)


def _sample_slc_kernel(*refs, n_sel, past_len):
    blk_ref, page_ref = refs[0], refs[1]
    del page_ref
    q_ref, new_ref = refs[2], refs[3]
    c_refs = refs[4:4 + NSA_KV_HEADS * n_sel]
    o_ref = refs[4 + NSA_KV_HEADS * n_sel]
    bi = pl.program_id(0)
    hrow = lax.broadcasted_iota(I32, (NSA_HEADS, 1), 0)
    first = (hrow >> GQA_SHIFT) == 0
    nk = n_sel * PAGE
    col = lax.broadcasted_iota(I32, (NSA_HEADS, nk), 1)
    tile = col >> PAGE_SHIFT
    row = col & (PAGE - 1)
    kts, vts, blks = [], [], []
    for kh in range(NSA_KV_HEADS):
        mine = c_refs[kh * n_sel:(kh + 1) * n_sel]
        kts.append(jnp.concatenate([c[kh * NSA_HD:(kh + 1) * NSA_HD, :] for c in mine], axis=1).astype(BF16))
        vts.append(jnp.concatenate([c[(2 + kh) * NSA_HD:(3 + kh) * NSA_HD, :] for c in mine],
                                   axis=1).astype(BF16))
        blkv = jnp.zeros(col.shape, I32)
        for t in range(n_sel):
            blkv = jnp.where(tile == t, blk_ref[bi, kh * n_sel + t], blkv)
        blks.append(blkv)
    blkv = jnp.where(first, blks[0], blks[1])
    kpos = (blkv >> 1) * PAGE + row
    ok = ((row >> SLC_SHIFT) == (blkv & 1)) & (kpos < past_len)
    dist = past_len - kpos
    o_ref[...] = _attend_t(q_ref[...], kts, vts, -_slope_col() * dist.astype(F32), ok, new_ref[...])


def _sample_slc(q8, kvs_new, cache_slc_t, blk_idx, page_idx, past_len):
    b = q8.shape[0]
    n_sel = blk_idx.shape[1] // NSA_KV_HEADS
    per_b3 = lambda bi, blk, pg: (bi, 0, 0)
    in_specs = [pl.BlockSpec((None, NSA_HEADS, NSA_HD), per_b3),
                pl.BlockSpec((None, 1, NSA_KV_COLS), per_b3)]
    in_specs += [pl.BlockSpec((None, NSA_KV_COLS, PAGE),
                              functools.partial(lambda bi, blk, pg, j: (pg[bi, j], 0, 0), j=j))
                 for j in range(NSA_KV_HEADS * n_sel)]
    return pl.pallas_call(
        functools.partial(_sample_slc_kernel, n_sel=n_sel, past_len=past_len),
        grid_spec=pltpu.PrefetchScalarGridSpec(
            num_scalar_prefetch=2, grid=(b,), in_specs=in_specs,
            out_specs=pl.BlockSpec((None, NSA_HEADS, NSA_HD), per_b3)),
        out_shape=jax.ShapeDtypeStruct((b, NSA_HEADS, NSA_HD), F32),
        compiler_params=_cparams(("parallel",)),
        name="sample_slc",
    )(blk_idx, page_idx, q8, kvs_new.reshape(b, 1, NSA_KV_COLS), *([cache_slc_t] * (NSA_KV_HEADS * n_sel)))


def _heads_to_swapped(o8):
    b = o8.shape[0]
    return o8.reshape(b, NSA_HEADS // 2, 2, NSA_HD)[:, :, ::-1].reshape(b, NSA_HEADS * NSA_HD).astype(BF16)


def _pair_swap(o8):
    b = o8.shape[0]
    o = o8.reshape(b, NSA_HEADS // 2, 2, LANE)
    lo = jnp.arange(LANE) < HALF
    return jnp.where(lo, o[:, :, 1], o[:, :, 0]).reshape(b, 4 * LANE).astype(BF16)


def _rope_table(pos):
    half = MLA_ROPE // 2
    freqs = ROPE_THETA ** (-jnp.arange(half, dtype=F32) / half)
    ang = pos.astype(F32)[:, None] * freqs
    cos, sin = jnp.cos(ang), jnp.sin(ang)
    n = pos.shape[0]
    one, zero = jnp.ones((n, MLA_NOPE), F32), jnp.zeros((n, MLA_NOPE), F32)
    pad = jnp.zeros((n, LANE - MLA_NOPE - MLA_ROPE), F32)
    return jnp.concatenate([one, cos, cos, pad, zero, -sin, sin, pad], axis=1)


def _cmp_to_slc(n_cmp, n_slc, rows, cols):
    cs = np.arange(n_cmp)[:, None] * CMP_STRIDE
    ss = np.arange(n_slc)[None, :] * SLC_BLOCK
    inter = np.clip(np.minimum(cs + CMP_BLOCK, ss + SLC_BLOCK) - np.maximum(cs, ss), 0, None)
    w = np.zeros((rows, cols), np.float32)
    w[:n_cmp, :n_slc] = inter.astype(np.float32) / CMP_STRIDE
    return jnp.asarray(w, BF16)


def _prep_weights(w_in, mla_g_q, mla_g_kv, mla_w_uq, mla_w_uk, mla_w_uv, mla_w_o, nsa_pe_k, nsa_w1_k,
                  nsa_w2_k, nsa_pe_v, nsa_w1_v, nsa_w2_v, nsa_w_o, w_out, ln1_g, ln1_b, router_g_w,
                  router_g_b, router_e_w, router_e_b, ln2_g, ln2_b):
    d = w_in.shape[0]
    splits = (MLA_Q_RANK, MLA_KV_RANK, MLA_ROPE, NSA_HEADS * NSA_HD, NSA_KV_COLS, NSA_KV_COLS, NSA_KV_COLS,
              3 * NSA_HEADS, d, d)
    offs = np.cumsum(splits)[:-1].tolist()
    cq, ckv, kr, q, kvc, kvs, kvw, gn, ga, gb = jnp.split(w_in, offs, axis=1)
    r2 = MLA_ROPE // 2
    zc = lambda n: jnp.zeros((d, n), F32)
    tail = LANE - MLA_NOPE - MLA_ROPE
    kr_pad = jnp.concatenate([zc(MLA_NOPE), kr, zc(tail)], axis=1)
    kr_sw = jnp.concatenate([zc(MLA_NOPE), kr[:, r2:], kr[:, :r2], zc(tail)], axis=1)
    gn_pad = jnp.concatenate([gn, zc(LANE - gn.shape[1])], axis=1)
    w_all = jnp.concatenate([cq, ckv, q, kvc, kvs, kvw, kr_pad, kr_sw, gn_pad, ga, gb], axis=1).astype(BF16)

    def pad_heads(w, lo_cols):
        r, hh, c = w.shape
        out = jnp.zeros((r, hh, LANE), F32).at[:, :, lo_cols:lo_cols + c].set(w)
        return out

    nope, rope = mla_w_uq[:, :, :MLA_NOPE], mla_w_uq[:, :, MLA_NOPE:]
    wuq = jnp.concatenate([nope, rope, jnp.zeros(nope.shape[:2] + (tail,), F32)], axis=2)
    wuqs = jnp.concatenate([jnp.zeros_like(nope), rope[:, :, r2:], rope[:, :, :r2],
                            jnp.zeros(nope.shape[:2] + (tail,), F32)], axis=2)
    wuk = pad_heads(mla_w_uk, 0)
    odd = (jnp.arange(MLA_HEADS) % 2 == 1)[None, :, None]
    wuv = jnp.where(odd, pad_heads(mla_w_uv, HALF), pad_heads(mla_w_uv, 0))
    flat = lambda w: w.reshape(w.shape[0], -1).astype(BF16)
    wabs = jnp.zeros((MLA_HEADS, LANE, MLA_KV_RANK), F32).at[:, :MLA_NOPE, :].set(
        jnp.transpose(mla_w_uk, (1, 2, 0))).astype(BF16)

    def w1_cols(w1):
        return jnp.transpose(w1.reshape(2, CMP_STRIDE, NSA_HD, CMP_HIDDEN), (1, 2, 0, 3))

    wt = jnp.stack([w1_cols(nsa_w1_k), w1_cols(nsa_w1_v)])
    eye2 = jnp.eye(2, dtype=F32)
    w_big = jnp.einsum('tpdfh,tu,kv->ptkdfuvh', wt, eye2, eye2).reshape(
        CMP_STRIDE, NSA_KV_COLS, 2 * 2 * NSA_KV_HEADS * CMP_HIDDEN).astype(BF16)
    zpe = jnp.zeros((CMP_BLOCK * NSA_HD, CMP_HIDDEN), F32)
    w1_pe = jnp.concatenate([
        jnp.concatenate([nsa_w1_k, nsa_w1_k, zpe, zpe], axis=1),
        jnp.concatenate([zpe, zpe, nsa_w1_v, nsa_w1_v], axis=1)], axis=0)
    pe_flat = jnp.concatenate([nsa_pe_k.reshape(1, -1), nsa_pe_v.reshape(1, -1)], axis=1)
    w1_hi = w1_pe.astype(BF16)
    w1_lo = (w1_pe - w1_hi.astype(F32)).astype(BF16)
    z64 = jnp.zeros((CMP_HIDDEN, NSA_HD), F32)
    rows = [[nsa_w2_k, z64, z64, z64], [z64, nsa_w2_k, z64, z64], [z64, z64, nsa_w2_v, z64],
            [z64, z64, z64, nsa_w2_v]]
    w2_cmp = jnp.concatenate([jnp.concatenate(r, axis=1) for r in rows], axis=0).astype(BF16)

    perm = np.arange(NSA_HEADS * NSA_HD).reshape(NSA_HEADS // 2, 2, NSA_HD)[:, ::-1].reshape(-1)
    w_on = nsa_w_o[perm].astype(BF16)
    e3 = np.zeros((LANE, 3, NSA_HEADS * NSA_HD), np.float32)
    for h in range(NSA_HEADS):
        pos = (h // 2) * LANE + (0 if h % 2 else HALF)
        for j in range(3):
            e3[3 * h + j, j, pos:pos + NSA_HD] = 1.0
    e3 = jnp.asarray(e3.reshape(LANE, -1), BF16)
    w_r = jnp.concatenate([router_g_w, zc(HALF - N_GROUPS), router_e_w], axis=1)
    w_r_hi = w_r.astype(BF16)
    w_r_lo = (w_r - w_r_hi.astype(F32)).astype(BF16)
    b_r = jnp.concatenate([router_g_b, jnp.zeros((HALF - N_GROUPS,), F32), router_e_b])[None, :]
    return dict(
        w_all=w_all, wuq=flat(wuq), wuqs=flat(wuqs), wuk=flat(wuk), wuv=flat(wuv),
        gq=mla_g_q[None, :], gkv=mla_g_kv[None, :], wabs=wabs, w_big=w_big,
        pe_flat=pe_flat, w1_pe=jnp.stack([w1_hi, w1_lo]), w2_cmp=w2_cmp,
        w_o=mla_w_o.astype(BF16), w_on=w_on, w_out=w_out.astype(BF16), e3=e3,
        ln1_g=ln1_g[None, :], ln1_b=ln1_b[None, :], w_router=jnp.stack([w_r_hi, w_r_lo]), b_router=b_r,
        ln2_g=ln2_g[None, :], ln2_b=ln2_b[None, :])


def _prompt_layer(x, pw, pe, slopes, moe_w):
    b, s, d = x.shape
    n = b * s
    tm = min(256, s)
    z = _in_proj(x.reshape(n, d), _rope_table(jnp.arange(s)), pw, tm, s)
    o_mla = _flash(z['qm'], z['km'], z['vm'], slopes, b, s, mode='causal', kv_shared=False, alibi=False,
                   swap=False)
    nb = s // CMP_STRIDE
    fs = _cmp_fs_dense(z['kvc'], pw['w_big'])
    kc_var, kc_var_t = _cmp_finish(fs, nb, pe, pw)
    n_slc = -(-s // SLC_BLOCK)
    wmap_t = jnp.transpose(_cmp_to_slc(nb - 1, n_slc, nb, LANE))
    o_cmp, sel = _cmp_select(z['qn'], kc_var, kc_var_t, slopes, wmap_t, b, s, n_slc)
    emat_t = jnp.asarray((np.arange(s)[:, None] // SLC_BLOCK) == np.arange(LANE)[None, :], BF16)
    o_slc = _flash(z['qn'], z['kvsv'], z['kvsvt'], slopes, b, s, mode='select', kv_shared=True, alibi=True,
                   swap=True, selt=sel, emat_t=emat_t)
    o_win = _flash(z['qn'], z['kvwv'], z['kvwvt'], slopes, b, s, mode='window', kv_shared=True, alibi=True,
                   swap=True)
    h, rt = _merge(x.reshape(n, d), o_mla, o_cmp, o_slc, o_win, z, pw, min(512, s))
    y = _moe(h, rt, pw, *moe_w)
    return y.reshape(b, s, d), z


def _sample_layer(x, pw, pe, slopes, moe_w, cache_ckv, cache_kr_t, cache_cmp_t, cache_slc_t, state_t, page_table):
    b, t, d = x.shape
    n_pages = page_table.shape[1]
    past_len = n_pages * PAGE
    z = _in_proj(x.reshape(b, d), _rope_table(jnp.full((b,), past_len)), pw, b, b)
    qlat = _q_latent(z['qm'], pw['wabs'])
    qr = z['qm'].reshape(b, MLA_HEADS, LANE)[:, :, MLA_NOPE:MLA_NOPE + MLA_ROPE]
    o_lat = _mla_decode(qlat, qr, z['ckv'], z['kr'], cache_ckv, cache_kr_t, page_table)
    o_mla = _mla_out_up(o_lat.reshape(b, MLA_HEADS * MLA_KV_RANK), pw['wuv'])
    nb = past_len // CMP_STRIDE
    fs = _cmp_fs_paged(cache_cmp_t, page_table, pw['w_big'])
    kc_var, _ = _cmp_finish(fs, nb, pe, pw)
    n_slc = -(-(past_len + t) // SLC_BLOCK)
    assert n_slc >= SLC_TOP_N and past_len % SLC_BLOCK == 0
    wcols = -(-n_slc // LANE) * LANE
    wmap = _cmp_to_slc(nb - 1, n_slc, nb, wcols)
    q8 = z['qnat'].reshape(b, NSA_HEADS, NSA_HD)
    o_cmp8, imp, o_win8, win_out = _sample_cmp_win(z['qn'], q8, kc_var, wmap, state_t, z['kvw'], past_len)
    cur = past_len // SLC_BLOCK
    idx = _sample_select(imp.reshape(b * NSA_KV_HEADS, wcols), cur, n_slc)[:, :SLC_TOP_N]
    blk = idx.reshape(b, NSA_KV_HEADS * SLC_TOP_N)
    safe = jnp.minimum(blk, past_len // SLC_BLOCK - 1)
    pages = jnp.take_along_axis(page_table, safe // (PAGE // SLC_BLOCK), axis=1).astype(I32)
    o_slc8 = _sample_slc(q8, z['kvs'], cache_slc_t, blk, pages, past_len)
    h, rt = _merge(x.reshape(b, d), o_mla, _pair_swap(o_cmp8), _heads_to_swapped(o_slc8),
                   _heads_to_swapped(o_win8), z, pw, b)
    y = _moe(h, rt, pw, *moe_w)
    return y.reshape(b, t, d), z, win_out


def kernel(x_prompt, x_sample, cache_mla_ckv, cache_mla_krope, cache_nsa_cmp_kv, cache_nsa_slc_kv,
           state_nsa_win_kv, page_table, w_in, mla_g_q, mla_g_kv, mla_w_uq, mla_w_uk, mla_w_uv, mla_w_o,
           nsa_pe_k, nsa_w1_k, nsa_w2_k, nsa_pe_v, nsa_w1_v, nsa_w2_v, nsa_w_o, w_out, ln1_g, ln1_b,
           router_g_w, router_g_b, router_e_w, router_e_b, moe_w1, moe_w3, moe_w2, ln2_g, ln2_b):
    assert w_in.shape[0] == DEPTH and x_sample.shape[1] == 1
    b, s, d = x_prompt.shape
    bs = x_sample.shape[0]
    kv_shape = (2, NSA_KV_HEADS, NSA_HD)
    pw = _prep_weights(w_in[0], mla_g_q[0], mla_g_kv[0], mla_w_uq[0], mla_w_uk[0], mla_w_uv[0], mla_w_o[0],
                       nsa_pe_k[0], nsa_w1_k[0], nsa_w2_k[0], nsa_pe_v[0], nsa_w1_v[0], nsa_w2_v[0],
                       nsa_w_o[0], w_out[0], ln1_g[0], ln1_b[0], router_g_w[0], router_g_b[0],
                       router_e_w[0], router_e_b[0], ln2_g[0], ln2_b[0])
    slopes = 2.0 ** (-8.0 * jnp.arange(1, NSA_HEADS + 1, dtype=F32) / NSA_HEADS)
    moe_w = (moe_w1, moe_w3, moe_w2)
    pe = _cmp_pe(pw)
    y_p, zp = _prompt_layer(x_prompt, pw, pe, slopes, moe_w)
    def rows_last(c):
        return jnp.transpose(c, (0, 2, 3, 4, 1)).reshape(c.shape[0], NSA_KV_COLS, c.shape[1])

    y_s, zs, win_t = _sample_layer(
        x_sample, pw, pe, slopes, moe_w, cache_mla_ckv[0], jnp.transpose(cache_mla_krope[0], (0, 2, 1)),
        rows_last(cache_nsa_cmp_kv[0]), rows_last(cache_nsa_slc_kv[0]), rows_last(state_nsa_win_kv[0]),
        page_table)
    def rows_first(t):
        return jnp.transpose(t.reshape((t.shape[0],) + kv_shape + (-1,)), (0, 4, 1, 2, 3))[None]

    wp = min(WINDOW, s)
    return (y_p, y_s,
            zp['ckv'].reshape(1, b, s, MLA_KV_RANK), zs['ckv'].reshape(1, bs, 1, MLA_KV_RANK),
            zp['kr'].reshape(1, b, s, MLA_ROPE), zs['kr'].reshape(1, bs, 1, MLA_ROPE),
            rows_first(zp['kvct']), zs['kvc'].reshape((1, bs, 1) + kv_shape),
            rows_first(zp['kvst']), zs['kvs'].reshape((1, bs, 1) + kv_shape),
            rows_first(zp['kvwt'][:, :, s - wp:]), rows_first(win_t))
```

```python
import functools

import numpy as np
import jax
import jax.numpy as jnp
from jax import lax
from jax.experimental import pallas as pl
from jax.experimental.pallas import tpu as pltpu

F32 = jnp.float32
BF16 = jnp.bfloat16
I32 = jnp.int32

PAGE = 128
MLA_HEADS = 8
MLA_NOPE = 64
MLA_ROPE = 32
MLA_V = 64
MLA_Q_RANK = 256
MLA_KV_RANK = 256
ROPE_THETA = 10000.0
NSA_HEADS = 8
NSA_KV_HEADS = 2
NSA_GQA = NSA_HEADS // NSA_KV_HEADS
NSA_HD = 64
CMP_BLOCK = 32
CMP_STRIDE = 16
CMP_HIDDEN = 64
SLC_BLOCK = 64
SLC_TOP_N = 16
N_LOCAL_SLC = 2
WINDOW = 512
FORCE_SCORE = 1.0e4
N_GROUPS = 8
EXPERTS_PER_GROUP = 8
N_EXPERTS = N_GROUPS * EXPERTS_PER_GROUP
TOP_K = 2
D_EXPERT = 256
MOE_BLOCK = 256
LN_EPS = 1e-5
RMS_EPS = 1e-6
DEPTH = 1
DEEPNORM_ALPHA = (2.0 * DEPTH) ** 0.25
NSA_KV_COLS = 2 * NSA_KV_HEADS * NSA_HD

PAGE_SHIFT = PAGE.bit_length() - 1
SLC_SHIFT = SLC_BLOCK.bit_length() - 1
GQA_SHIFT = NSA_GQA.bit_length() - 1
LANE = 128
HALF = LANE // 2
VMEM_LIMIT = 56 * 1024 * 1024
NEG = -1e30
DMA_UNROLL = 8
LOG2E = 1.4426950408889634

_C_CQ, _C_CKV, _C_Q, _C_KVC, _C_KVS, _C_KVW, _C_KRP, _C_KRS, _C_GN, _C_GA = (
    0, 256, 512, 1024, 1280, 1536, 1792, 1920, 2048, 2176)


def _cparams(sem):
    return pltpu.CompilerParams(dimension_semantics=sem, vmem_limit_bytes=VMEM_LIMIT)


def _dot(a, b):
    return jnp.dot(a, b, preferred_element_type=F32)


def _dot_nt(a, b):
    return lax.dot_general(a, b, (((1,), (1,)), ((), ())), preferred_element_type=F32)


def _split_dot(a, w_hi, w_lo=None):
    a_hi = a.astype(BF16)
    a_lo = (a - a_hi.astype(F32)).astype(BF16)
    r = _dot(a_hi, w_hi) + _dot(a_lo, w_hi)
    if w_lo is not None:
        r = r + _dot(a_hi, w_lo)
    return r


def _sigmoid(x):
    return 1.0 / (1.0 + jnp.exp(-x))


def _kv_variants(x0, x1):
    lo = lax.broadcasted_iota(I32, x0.shape, 1) < HALF
    r0 = pltpu.roll(x0, HALF, 1)
    r1 = pltpu.roll(x1, HALF, 1)
    return (jnp.where(lo, x0, r1), jnp.where(lo, x1, r0), jnp.where(lo, r0, x1), jnp.where(lo, r1, x0))


def _inproj_kernel(x_ref, cs_ref, w_ref, wuq_ref, wuqs_ref, wuk_ref, wuv_ref, gq_ref, gkv_ref,
                   qm_ref, km_ref, vm_ref, ckv_ref, kr_ref, qn_ref, kvc_ref, kvs_ref, kvw_ref,
                   kvsv_ref, kvwv_ref, gn_ref, ga_ref, gb_ref, qnat_ref, kvsvt_ref, kvwvt_ref,
                   kvct_ref, kvst_ref, kvwt_ref, *, mla_scale):
    xb = x_ref[...].astype(BF16)

    def proj(a, b):
        return _dot(xb, w_ref[:, a:b])

    cos = cs_ref[:, :LANE]
    sin = cs_ref[:, LANE:]

    def rms(z, g):
        return z * lax.rsqrt(jnp.mean(z * z, axis=-1, keepdims=True) + RMS_EPS) * g

    cq = rms(proj(_C_CQ, _C_CKV), gq_ref[...]).astype(BF16)
    q = _dot(cq, wuq_ref[...])
    qs = _dot(cq, wuqs_ref[...])
    ckv = rms(proj(_C_CKV, _C_Q), gkv_ref[...])
    ckv_ref[...] = ckv
    ckvb = ckv.astype(BF16)
    kr = proj(_C_KRP, _C_KRS) * cos + proj(_C_KRS, _C_GN) * sin
    kr_ref[...] = kr[:, MLA_NOPE:MLA_NOPE + MLA_ROPE]
    kn = _dot(ckvb, wuk_ref[...])
    v = _dot(ckvb, wuv_ref[...])
    for h in range(MLA_HEADS):
        sl = slice(h * LANE, (h + 1) * LANE)
        qm_ref[:, sl] = ((q[:, sl] * cos + qs[:, sl] * sin) * mla_scale).astype(BF16)
        km_ref[:, sl] = (kn[:, sl] + kr).astype(BF16)
        vm_ref[sl, :] = v[:, sl].T.astype(BF16)
    qn = proj(_C_Q, _C_KVC) * (NSA_HD ** -0.5 * LOG2E)
    qnat_ref[...] = qn.astype(BF16)
    lo = lax.broadcasted_iota(I32, (qn.shape[0], LANE), 1) < HALF
    for j in range(NSA_HEADS // 2):
        blk = qn[:, j * LANE:(j + 1) * LANE]
        qn_ref[:, (2 * j) * LANE:(2 * j + 1) * LANE] = jnp.where(lo, blk, 0.0).astype(BF16)
        qn_ref[:, (2 * j + 1) * LANE:(2 * j + 2) * LANE] = jnp.where(lo, 0.0, blk).astype(BF16)
    zc = proj(_C_KVC, _C_KVS)
    kvc_ref[...] = zc
    kvct_ref[...] = zc.T
    for src, dst, dst_t, var, vart in ((_C_KVS, kvs_ref, kvst_ref, kvsv_ref, kvsvt_ref),
                                       (_C_KVW, kvw_ref, kvwt_ref, kvwv_ref, kvwvt_ref)):
        z = proj(src, src + NSA_KV_COLS)
        dst[...] = z
        dst_t[...] = z.T
        for i, t in enumerate(_kv_variants(z[:, :LANE], z[:, LANE:])):
            var[:, i * LANE:(i + 1) * LANE] = t.astype(BF16)
            vart[i * LANE:(i + 1) * LANE, :] = t.T.astype(BF16)
    gn_ref[...] = _sigmoid(proj(_C_GN, _C_GA))
    d = ga_ref.shape[1]
    ga_ref[...] = _sigmoid(proj(_C_GA, _C_GA + d)).astype(BF16)
    gb_ref[...] = _sigmoid(proj(_C_GA + d, _C_GA + 2 * d)).astype(BF16)


def _in_proj(x, cs, pw, tm, seq):
    n, d = x.shape
    per_seq = seq // tm
    n_cs = cs.shape[0] // tm
    row = lambda i: (i, 0)
    full = lambda i: (0, 0)
    wcols = pw['w_all'].shape[1]
    outs = [
        ('qm', 8 * LANE, BF16), ('km', 8 * LANE, BF16), ('vm', 8 * LANE, BF16),
        ('ckv', MLA_KV_RANK, F32), ('kr', MLA_ROPE, F32), ('qn', 8 * LANE, BF16),
        ('kvc', NSA_KV_COLS, F32), ('kvs', NSA_KV_COLS, F32), ('kvw', NSA_KV_COLS, F32),
        ('kvsv', 4 * LANE, BF16), ('kvwv', 4 * LANE, BF16), ('gn', LANE, F32),
        ('ga', d, BF16), ('gb', d, BF16), ('qnat', NSA_HEADS * NSA_HD, BF16),
        ('kvsvt', 4 * LANE, BF16), ('kvwvt', 4 * LANE, BF16),
        ('kvct', NSA_KV_COLS, F32), ('kvst', NSA_KV_COLS, F32), ('kvwt', NSA_KV_COLS, F32)]
    transposed = ('vm', 'kvsvt', 'kvwvt')
    by_seq = ('kvct', 'kvst', 'kvwt')

    def out_spec(k, c):
        if k in transposed:
            return pl.BlockSpec((c, tm), lambda i: (0, i))
        if k in by_seq:
            return pl.BlockSpec((None, c, tm), lambda i: (i // per_seq, 0, i % per_seq))
        return pl.BlockSpec((tm, c), row)

    def out_shape(k, c, t):
        shape = (c, n) if k in transposed else (n // seq, c, seq) if k in by_seq else (n, c)
        return jax.ShapeDtypeStruct(shape, t)

    res = pl.pallas_call(
        functools.partial(_inproj_kernel, mla_scale=(MLA_NOPE + MLA_ROPE) ** -0.5 * LOG2E),
        grid=(n // tm,),
        in_specs=[pl.BlockSpec((tm, d), row),
                  pl.BlockSpec((tm, 2 * LANE), lambda i: (i % n_cs, 0)),
                  pl.BlockSpec((d, wcols), full),
                  pl.BlockSpec((MLA_Q_RANK, 8 * LANE), full),
                  pl.BlockSpec((MLA_Q_RANK, 8 * LANE), full),
                  pl.BlockSpec((MLA_KV_RANK, 8 * LANE), full),
                  pl.BlockSpec((MLA_KV_RANK, 8 * LANE), full),
                  pl.BlockSpec((1, MLA_Q_RANK), full),
                  pl.BlockSpec((1, MLA_KV_RANK), full)],
        out_specs=[out_spec(k, c) for k, c, _ in outs],
        out_shape=[out_shape(k, c, t) for k, c, t in outs],
        compiler_params=_cparams(("parallel",)),
        name="in_proj",
    )(x, cs, pw['w_all'], pw['wuq'], pw['wuqs'], pw['wuk'], pw['wuv'], pw['gq'], pw['gkv'])
    return {k: v for (k, _, _), v in zip(outs, res)}


def _flash_kernel(*refs, mode, t, alibi, swap):
    if mode == 'select':
        slopes_ref, q_ref, k_ref, vt_ref, selt_ref, et_ref, o_ref, m_scr, l_scr, acc_scr = refs
    else:
        slopes_ref, q_ref, k_ref, vt_ref, o_ref, m_scr, l_scr, acc_scr = refs
    hp = pl.program_id(1)
    qi = pl.program_id(2)
    q0 = qi * t
    krow = lax.broadcasted_iota(I32, (t, t), 0)
    dmat = lax.broadcasted_iota(I32, (t, t), 1) - krow
    krow_f = krow.astype(F32)
    qs = [q_ref[:, e * LANE:(e + 1) * LANE] for e in range(2)]
    for e in range(2):
        m_scr[e] = jnp.full((1, t), NEG, F32)
        l_scr[e] = jnp.zeros((1, t), F32)
        acc_scr[e] = jnp.zeros((LANE, t), F32)

    def step(kt, edge):
        k0 = pl.multiple_of(kt * t, t)
        ok = None
        if edge:
            dist = dmat + (q0 - k0)
            ok = dist <= WINDOW if edge == 'low' else dist >= 0
            if edge == 'both':
                ok = ok & (dist <= WINDOW)
        if mode == 'select':
            picked = _dot(et_ref[pl.ds(k0, t), :], selt_ref[...]) > 0.5
            ok = picked if ok is None else ok & picked
        for e in range(2):
            k = k_ref[pl.ds(k0, t), e * LANE:(e + 1) * LANE]
            vt = vt_ref[e * LANE:(e + 1) * LANE, pl.ds(k0, t)]
            shift = 0.0
            if alibi:
                slope = slopes_ref[2 * hp + e] * LOG2E
                shift = slope * (k0 - q0).astype(F32)
            s = _dot_nt(k, qs[e])
            if alibi:
                s = s + slope * krow_f
            if ok is not None:
                s = jnp.where(ok, s, NEG)
            m_prev = m_scr[e]
            m_new = jnp.maximum(m_prev, jnp.max(s, axis=0, keepdims=True) + shift)
            alpha = jnp.exp2(m_prev - m_new)
            p = jnp.exp2(s - (m_new - shift))
            l_scr[e] = alpha * l_scr[e] + jnp.sum(p, axis=0, keepdims=True)
            acc_scr[e] = alpha * acc_scr[e] + _dot(vt, p.astype(BF16))
            m_scr[e] = m_new

    def loop(lo, hi, edge):
        n = hi - lo

        def pair(j, c):
            step(lo + 2 * j, edge)
            step(lo + 2 * j + 1, edge)
            return c

        lax.fori_loop(0, n >> 1, pair, 0)

        @pl.when((n & 1) == 1)
        def _():
            step(hi - 1, edge)

    if mode == 'window' and t == WINDOW:
        @pl.when(qi > 0)
        def _():
            step(qi - 1, 'low')
        step(qi, 'diag')
    elif mode == 'window':
        loop(jnp.maximum(q0 - WINDOW, 0) // t, qi + 1, 'both')
    else:
        loop(0, qi, False)
        step(qi, 'diag')
    o0 = acc_scr[0] / l_scr[0]
    o1 = acc_scr[1] / l_scr[1]
    lo = lax.broadcasted_iota(I32, (LANE, t), 0) < HALF
    o_ref[...] = (jnp.where(lo, o1, o0) if swap else jnp.where(lo, o0, o1)).T.astype(o_ref.dtype)


def _flash(q, k, vt, slopes, b, s, *, mode, kv_shared, alibi, swap, selt=None, emat_t=None):
    tq = min(512, s)
    nq = s // tq
    k_idx = (lambda bi, hp, qi: (bi, hp // 2)) if kv_shared else (lambda bi, hp, qi: (bi, hp))
    vt_idx = (lambda bi, hp, qi: (hp // 2, bi)) if kv_shared else (lambda bi, hp, qi: (hp, bi))
    in_specs = [pl.BlockSpec(memory_space=pltpu.SMEM),
                pl.BlockSpec((tq, 2 * LANE), lambda bi, hp, qi: (bi * nq + qi, hp)),
                pl.BlockSpec((s, 2 * LANE), k_idx),
                pl.BlockSpec((2 * LANE, s), vt_idx)]
    args = [slopes, q, k, vt]
    if mode == 'select':
        in_specs += [pl.BlockSpec((None, None, LANE, tq), lambda bi, hp, qi: (bi, hp // 2, 0, qi)),
                     pl.BlockSpec((s, LANE), lambda bi, hp, qi: (0, 0))]
        args += [selt, emat_t]
    return pl.pallas_call(
        functools.partial(_flash_kernel, mode=mode, t=tq, alibi=alibi, swap=swap),
        grid=(b, 4, nq),
        in_specs=in_specs,
        out_specs=pl.BlockSpec((tq, LANE), lambda bi, hp, qi: (bi * nq + qi, hp)),
        out_shape=jax.ShapeDtypeStruct((b * s, 4 * LANE), BF16),
        scratch_shapes=[pltpu.VMEM((2, 1, tq), F32), pltpu.VMEM((2, 1, tq), F32),
                        pltpu.VMEM((2, LANE, tq), F32)],
        compiler_params=_cparams(("parallel", "parallel", "arbitrary")),
        name="flash_" + mode,
    )(*args)


def _cmp_first_linear(lo, hi, w_ref, o_ref, r0=0, r=None):
    r = o_ref.shape[0] if r is None else r
    acc = jnp.zeros((r, o_ref.shape[1]), F32)
    for p in range(CMP_STRIDE):
        rows = pl.ds(r0 * CMP_STRIDE + p, r, stride=CMP_STRIDE)
        x = jnp.concatenate([lo[rows, :], hi[rows, :]], axis=1)
        acc = acc + _dot(x.astype(BF16), w_ref[p])
    o_ref[r0:r0 + r, :] = acc


def _cmp_fs_kernel(lo, hi, w_ref, o_ref):
    _cmp_first_linear(lo, hi, w_ref, o_ref)


def _cmp_fs_dense(x, w_big):
    n, c = x.shape
    tr = min(256, n // CMP_STRIDE)
    return pl.pallas_call(
        _cmp_fs_kernel,
        grid=(n // CMP_STRIDE // tr,),
        in_specs=[pl.BlockSpec((tr * CMP_STRIDE, LANE), lambda i: (i, 0)),
                  pl.BlockSpec((tr * CMP_STRIDE, LANE), lambda i: (i, 1)),
                  pl.BlockSpec(w_big.shape, lambda i: (0, 0, 0))],
        out_specs=pl.BlockSpec((tr, 4 * LANE), lambda i: (i, 0)),
        out_shape=jax.ShapeDtypeStruct((n // CMP_STRIDE, 4 * LANE), F32),
        compiler_params=_cparams(("parallel",)),
        name="cmp_fs",
    )(x, x, w_big)


def _prefetched_pages(pt_ref, srcs, bufs, sems, n_pages):
    bi = pl.program_id(0)
    slot = bi % 2

    def copies(seq, s, j):
        pg = pt_ref[seq, j]
        return [pltpu.make_async_copy(src.at[pg], buf.at[s, j], sems.at[s]) for src, buf in zip(srcs, bufs)]

    def start_all(seq, s):
        def body(j, c):
            for cp in copies(seq, s, j):
                cp.start()
            return c
        lax.fori_loop(0, n_pages, body, 0, unroll=DMA_UNROLL)

    @pl.when(bi == 0)
    def _():
        start_all(0, 0)

    @pl.when(bi + 1 < pl.num_programs(0))
    def _():
        start_all(bi + 1, 1 - slot)

    def wait(j, c):
        for cp in copies(bi, slot, j):
            cp.wait()
        return c

    lax.fori_loop(0, n_pages, wait, 0, unroll=DMA_UNROLL)
    return slot


def _cmp_fs_paged_kernel(pt_ref, cache_hbm, w_ref, o_ref, pbuf, sems, lo, hi, *, n_pages):
    slot = _prefetched_pages(pt_ref, (cache_hbm,), (pbuf,), sems, n_pages)

    groups = 4 if n_pages % 4 == 0 else 1
    per = n_pages // groups
    for g in range(groups):
        for j in range(g * per, (g + 1) * per):
            lo[j * PAGE:(j + 1) * PAGE, :] = pbuf[slot, j, :LANE, :].T
            hi[j * PAGE:(j + 1) * PAGE, :] = pbuf[slot, j, LANE:, :].T
        blocks = per * (PAGE // CMP_STRIDE)
        _cmp_first_linear(lo, hi, w_ref, o_ref, g * blocks, blocks)


def _cmp_fs_paged(cache_t, page_table, w_big):
    b, n_pages = page_table.shape
    rows = n_pages * (PAGE // CMP_STRIDE)
    return pl.pallas_call(
        functools.partial(_cmp_fs_paged_kernel, n_pages=n_pages),
        grid_spec=pltpu.PrefetchScalarGridSpec(
            num_scalar_prefetch=1, grid=(b,),
            in_specs=[pl.BlockSpec(memory_space=pl.ANY),
                      pl.BlockSpec(w_big.shape, lambda bi, pt: (0, 0, 0))],
            out_specs=pl.BlockSpec((rows, 4 * LANE), lambda bi, pt: (bi, 0)),
            scratch_shapes=[pltpu.VMEM((2, n_pages, NSA_KV_COLS, PAGE), F32), pltpu.SemaphoreType.DMA((2,)),
                            pltpu.VMEM((n_pages * PAGE, LANE), F32), pltpu.VMEM((n_pages * PAGE, LANE), F32)]),
        out_shape=jax.ShapeDtypeStruct((b * rows, 4 * LANE), F32),
        compiler_params=_cparams(("arbitrary",)),
        name="cmp_fs_paged",
    )(page_table, cache_t, w_big)


def _gelu_tanh(x):
    return 0.5 * x * (1.0 + jnp.tanh(0.7978845608028654 * (x + 0.044715 * x * x * x)))


def _cmp_pe_kernel(pe_ref, w1_ref, o_ref):
    o_ref[...] = _split_dot(pe_ref[...], w1_ref[0], w1_ref[1])


def _cmp_pe(pw):
    return pl.pallas_call(
        _cmp_pe_kernel,
        grid=(1,),
        in_specs=[pl.BlockSpec(pw['pe_flat'].shape, lambda i: (0, 0)),
                  pl.BlockSpec(pw['w1_pe'].shape, lambda i: (0, 0, 0))],
        out_specs=pl.BlockSpec((1, 2 * LANE), lambda i: (0, 0)),
        out_shape=jax.ShapeDtypeStruct((1, 2 * LANE), F32),
        compiler_params=_cparams(("arbitrary",)),
        name="cmp_pe",
    )(pw['pe_flat'], pw['w1_pe'])


def _cmp_finish_kernel(fs_ref, pe_ref, w2_ref, o_ref, ot_ref):
    fs = fs_ref[...]
    n = fs.shape[0]
    h = _gelu_tanh(fs[:, :2 * LANE] + pltpu.roll(fs[:, 2 * LANE:], n - 1, 0) + pe_ref[...])
    kc = _dot(h.astype(BF16), w2_ref[...])
    for i, t in enumerate(_kv_variants(kc[:, :LANE], kc[:, LANE:])):
        o_ref[:, i * LANE:(i + 1) * LANE] = t.astype(BF16)
        ot_ref[i * LANE:(i + 1) * LANE, :] = t.T.astype(BF16)


def _cmp_finish(fs, nb, pe, pw):
    r = fs.shape[0]
    full2 = lambda i: (0, 0)
    return pl.pallas_call(
        _cmp_finish_kernel,
        grid=(r // nb,),
        in_specs=[pl.BlockSpec((nb, 4 * LANE), lambda i: (i, 0)),
                  pl.BlockSpec(pe.shape, full2),
                  pl.BlockSpec(pw['w2_cmp'].shape, full2)],
        out_specs=[pl.BlockSpec((nb, 4 * LANE), lambda i: (i, 0)),
                   pl.BlockSpec((None, 4 * LANE, nb), lambda i: (i, 0, 0))],
        out_shape=[jax.ShapeDtypeStruct((r, 4 * LANE), BF16),
                   jax.ShapeDtypeStruct((r // nb, 4 * LANE, nb), BF16)],
        compiler_params=_cparams(("parallel",)),
        name="cmp_finish",
    )(fs, pe, pw['w2_cmp'])


def _masked_softmax(s, ok, axis=-1):
    s = jnp.where(ok, s, -jnp.inf)
    m = jnp.max(s, axis=axis, keepdims=True)
    m = jnp.where(m > -jnp.inf, m, 0.0)
    e = jnp.exp2(s - m)
    d = jnp.sum(e, axis=axis, keepdims=True)
    return e / jnp.where(d > 0.0, d, 1.0)


def _select_rank(imp, cur, n_slc):
    r, w = imp.shape
    blk = lax.broadcasted_iota(I32, (r, w), 1)
    valid = (blk <= cur) & (blk < n_slc)
    forced = (blk == 0) | (valid & (blk > cur - N_LOCAL_SLC))
    score = jnp.where(forced, FORCE_SCORE, jnp.where(valid, imp, -1.0))
    score = jnp.where(blk < n_slc, score, -2.0)

    def body(i, rank):
        col = jnp.sum(jnp.where(blk == i, score, 0.0), axis=1, keepdims=True)
        beats = (col > score) | ((col == score) & (i < blk))
        return rank + jnp.where(beats, 1.0, 0.0)

    rank = lax.fori_loop(0, n_slc, body, jnp.zeros((r, w), F32), unroll=True if n_slc <= 32 else 4)
    return rank, valid


def _cmp_select_kernel(slopes_ref, q_ref, kc_ref, kct_ref, wmapt_ref, o_ref, sel_ref, *, tq, n_slc):
    kvh = pl.program_id(1)
    q0 = pl.program_id(2) * tq
    nk = kc_ref.shape[0]
    qpos = q0 + lax.broadcasted_iota(I32, (nk, tq), 1)
    kend = lax.broadcasted_iota(I32, (nk, tq), 0) * CMP_STRIDE + (CMP_BLOCK - 1)
    ok = kend <= qpos
    dist = (qpos - kend).astype(F32)
    psum = jnp.zeros((nk, tq), F32)
    outs = []
    for g in range(NSA_GQA):
        sl = slice((g % 2) * LANE, (g % 2 + 1) * LANE)
        s = _dot_nt(kc_ref[:, sl], q_ref[:, g * LANE:(g + 1) * LANE]) - (slopes_ref[kvh * NSA_GQA + g] * LOG2E) * dist
        p = _masked_softmax(s, ok, axis=0)
        psum = psum + p
        outs.append(_dot(kct_ref[sl, :], p.astype(BF16)))
    lo = lax.broadcasted_iota(I32, (LANE, tq), 0) < HALF
    o_ref[:, :LANE] = jnp.where(lo, outs[1], outs[0]).T.astype(o_ref.dtype)
    o_ref[:, LANE:] = jnp.where(lo, outs[3], outs[2]).T.astype(o_ref.dtype)
    ps_hi = psum.astype(BF16)
    ps_lo = (psum - ps_hi.astype(F32)).astype(BF16)
    imp = _dot(wmapt_ref[...], ps_hi) + _dot(wmapt_ref[...], ps_lo)
    rows = -(-n_slc // 8) * 8
    imp = imp[:rows]
    blk = lax.broadcasted_iota(I32, (rows, tq), 0)
    cur = (q0 + lax.broadcasted_iota(I32, (1, tq), 1)) >> SLC_SHIFT
    valid = (blk <= cur) & (blk < n_slc)
    forced = (blk == 0) | (valid & (blk > cur - N_LOCAL_SLC))
    score = jnp.where(forced, FORCE_SCORE, jnp.where(valid, imp, -1.0))
    rank = jnp.zeros((rows, tq), F32)
    for i in range(n_slc):
        other = score[i:i + 1, :]
        rank = rank + jnp.where((other > score) | ((other == score) & (i < blk)), 1.0, 0.0)
    sel = jnp.where((rank < SLC_TOP_N) & valid, 1.0, 0.0)
    sel_ref[...] = jnp.concatenate([sel, jnp.zeros((LANE - rows, tq), F32)], axis=0).astype(sel_ref.dtype)


def _cmp_select(qn, kc_var, kc_var_t, slopes, wmap_t, b, s, n_slc):
    tq = min(512, s)
    nq = s // tq
    nk = kc_var.shape[0] // b
    assert n_slc <= LANE
    return pl.pallas_call(
        functools.partial(_cmp_select_kernel, tq=tq, n_slc=n_slc),
        grid=(b, NSA_KV_HEADS, nq),
        in_specs=[pl.BlockSpec(memory_space=pltpu.SMEM),
                  pl.BlockSpec((tq, 4 * LANE), lambda bi, kh, qi: (bi * nq + qi, kh)),
                  pl.BlockSpec((nk, 2 * LANE), lambda bi, kh, qi: (bi, kh)),
                  pl.BlockSpec((None, 2 * LANE, nk), lambda bi, kh, qi: (bi, kh, 0)),
                  pl.BlockSpec(wmap_t.shape, lambda bi, kh, qi: (0, 0))],
        out_specs=[pl.BlockSpec((tq, 2 * LANE), lambda bi, kh, qi: (bi * nq + qi, kh)),
                   pl.BlockSpec((None, None, LANE, tq), lambda bi, kh, qi: (bi, kh, 0, qi))],
        out_shape=[jax.ShapeDtypeStruct((b * s, 4 * LANE), BF16),
                   jax.ShapeDtypeStruct((b, NSA_KV_HEADS, LANE, s), BF16)],
        compiler_params=_cparams(("parallel", "parallel", "parallel")),
        name="cmp_select",
    )(slopes, qn, kc_var, kc_var_t, wmap_t)


def _layer_norm(x, g, b):
    mu = jnp.mean(x, axis=-1, keepdims=True)
    xc = x - mu
    var = jnp.mean(xc * xc, axis=-1, keepdims=True)
    return xc * lax.rsqrt(var + LN_EPS) * g + b


def _merge_kernel(x_ref, omla_ref, ocmp_ref, oslc_ref, owin_ref, gn_ref, ga_ref, gb_ref,
                  wo_ref, wn_ref, wout_ref, e3_ref, g1_ref, b1_ref, wr_ref, br_ref,
                  h_ref, rt_ref):
    gn = gn_ref[...]
    w = 4 * LANE
    ge = _split_dot(gn, e3_ref[...])
    o_nsa = (ge[:, :w] * ocmp_ref[...].astype(F32) + ge[:, w:2 * w] * oslc_ref[...].astype(F32)
             + ge[:, 2 * w:] * owin_ref[...].astype(F32))
    u = (ga_ref[...].astype(F32) * _dot(omla_ref[...], wo_ref[...])
         + gb_ref[...].astype(F32) * _dot(o_nsa.astype(BF16), wn_ref[...]))
    h = _layer_norm(DEEPNORM_ALPHA * x_ref[...] + _dot(u.astype(BF16), wout_ref[...]),
                    g1_ref[...], b1_ref[...])
    h_ref[...] = h
    lg = _split_dot(h, wr_ref[0], wr_ref[1]) + br_ref[...]
    tm = lg.shape[0]
    lane = lax.broadcasted_iota(I32, (tm, LANE), 1)
    is_g = lane < N_GROUPS
    gm = jnp.max(jnp.where(is_g, lg, -jnp.inf), axis=1, keepdims=True)
    gidx = jnp.min(jnp.where(is_g & (lg == gm), lane, LANE), axis=1, keepdims=True)
    p_group = 1.0 / jnp.sum(jnp.where(is_g, jnp.exp(lg - gm), 0.0), axis=1, keepdims=True)
    in_g = (lane >= HALF) & (((lane - HALF) >> 3) == gidx)
    m1 = jnp.max(jnp.where(in_g, lg, -jnp.inf), axis=1, keepdims=True)
    i1 = jnp.min(jnp.where(in_g & (lg == m1), lane, 2 * LANE), axis=1, keepdims=True)
    rest = in_g & (lane != i1)
    m2 = jnp.max(jnp.where(rest, lg, -jnp.inf), axis=1, keepdims=True)
    i2 = jnp.min(jnp.where(rest & (lg == m2), lane, 2 * LANE), axis=1, keepdims=True)
    t = jnp.exp(m2 - m1)
    g1 = p_group / (1.0 + t)
    g2 = p_group * t / (1.0 + t)
    rt = jnp.where(lane == 0, (i1 - HALF).astype(F32),
                   jnp.where(lane == 1, (i2 - HALF).astype(F32),
                             jnp.where(lane == 2, g1, jnp.where(lane == 3, g2, 0.0))))
    rt_ref[...] = rt


def _merge(x, omla, ocmp, oslc, owin, z, pw, tm):
    n, d = x.shape
    row = lambda i: (i, 0)
    full = lambda i: (0, 0)
    w = 4 * LANE
    return pl.pallas_call(
        _merge_kernel,
        grid=(n // tm,),
        in_specs=[pl.BlockSpec((tm, d), row), pl.BlockSpec((tm, w), row), pl.BlockSpec((tm, w), row),
                  pl.BlockSpec((tm, w), row), pl.BlockSpec((tm, w), row), pl.BlockSpec((tm, LANE), row),
                  pl.BlockSpec((tm, d), row), pl.BlockSpec((tm, d), row),
                  pl.BlockSpec((w, d), full), pl.BlockSpec((w, d), full), pl.BlockSpec((d, d), full),
                  pl.BlockSpec((LANE, 3 * w), full), pl.BlockSpec((1, d), full), pl.BlockSpec((1, d), full),
                  pl.BlockSpec((2, d, LANE), lambda i: (0, 0, 0)), pl.BlockSpec((1, LANE), full)],
        out_specs=[pl.BlockSpec((tm, d), row), pl.BlockSpec((tm, LANE), row)],
        out_shape=[jax.ShapeDtypeStruct((n, d), F32), jax.ShapeDtypeStruct((n, LANE), F32)],
        compiler_params=_cparams(("parallel",)),
        name="merge_router",
    )(x, omla, ocmp, oslc, owin, z['gn'], z['ga'], z['gb'], pw['w_o'], pw['w_on'], pw['w_out'],
      pw['e3'], pw['ln1_g'], pw['ln1_b'], pw['w_router'], pw['b_router'])


def _row_copy(src, dst, sem, src_row, dst_row):
    return pltpu.make_async_copy(src.at[pl.ds(src_row, 1)], dst.at[pl.ds(dst_row, 1)], sem)


def _moe_rank_kernel(rt_ref, tri_ref, rank_ref, cnt_ref, run_scr):
    @pl.when(pl.program_id(0) == 0)
    def _():
        run_scr[...] = jnp.zeros_like(run_scr)

    rt = rt_ref[...]
    lane = lax.broadcasted_iota(I32, rt.shape, 1)
    lanef = lane.astype(F32)
    e0 = jnp.sum(jnp.where(lane == 0, rt, 0.0), axis=1, keepdims=True)
    e1 = jnp.sum(jnp.where(lane == 1, rt, 0.0), axis=1, keepdims=True)
    hit0 = lanef == e0
    hit1 = lanef == e1
    onehot = jnp.where(hit0 | hit1, 1.0, 0.0)
    before = _dot(tri_ref[...], onehot.astype(BF16)) + run_scr[...]
    r0 = jnp.sum(jnp.where(hit0, before, 0.0), axis=1, keepdims=True)
    r1 = jnp.sum(jnp.where(hit1, before, 0.0), axis=1, keepdims=True)
    rank_ref[...] = jnp.where(lane == 0, r0, jnp.where(lane == 1, r1, 0.0))
    run = run_scr[...] + jnp.sum(onehot, axis=0, keepdims=True)
    run_scr[...] = run
    cnt_ref[...] = run


def _moe_rank(rt):
    n = rt.shape[0]
    tm = min(512, n)
    tri = jnp.asarray(np.tril(np.ones((tm, tm), np.float32), -1), BF16)
    return pl.pallas_call(
        _moe_rank_kernel,
        grid=(n // tm,),
        in_specs=[pl.BlockSpec((tm, LANE), lambda i: (i, 0)), pl.BlockSpec((tm, tm), lambda i: (0, 0))],
        out_specs=[pl.BlockSpec((tm, LANE), lambda i: (i, 0)), pl.BlockSpec((1, LANE), lambda i: (0, 0))],
        out_shape=[jax.ShapeDtypeStruct((n, LANE), F32), jax.ShapeDtypeStruct((1, LANE), F32)],
        scratch_shapes=[pltpu.VMEM((1, LANE), F32)],
        compiler_params=_cparams(("arbitrary",)),
        name="moe_rank",
    )(rt, tri)


def _moe_dispatch_kernel(dest_ref, h_ref, xs_in, xs_out, sem, *, tm):
    del xs_in

    def copies(t):
        return [_row_copy(h_ref, xs_out, sem, t, dest_ref[0, 0, TOP_K * t + k]) for k in range(TOP_K)]

    def start(t, c):
        for cp in copies(t):
            cp.start()
        return c

    def wait(t, c):
        for cp in copies(t):
            cp.wait()
        return c

    lax.fori_loop(0, tm, start, 0, unroll=DMA_UNROLL)
    lax.fori_loop(0, tm, wait, 0, unroll=DMA_UNROLL)


def _moe_dispatch(h, dest, n_slots):
    n, d = h.shape
    tm = min(512, n)
    return pl.pallas_call(
        functools.partial(_moe_dispatch_kernel, tm=tm),
        grid=(n // tm,),
        in_specs=[pl.BlockSpec((1, 1, TOP_K * tm), lambda i: (i, 0, 0), memory_space=pltpu.SMEM),
                  pl.BlockSpec((tm, d), lambda i: (i, 0)),
                  pl.BlockSpec(memory_space=pl.ANY)],
        out_specs=pl.BlockSpec(memory_space=pl.ANY),
        out_shape=jax.ShapeDtypeStruct((n_slots, d), F32),
        scratch_shapes=[pltpu.SemaphoreType.DMA(())],
        input_output_aliases={2: 0},
        compiler_params=_cparams(("arbitrary",)),
        name="moe_dispatch",
    )(dest.reshape(n // tm, 1, TOP_K * tm), h, jnp.zeros((n_slots, d), F32))


def _moe_expert_kernel(be_ref, x_ref, w1_ref, w3_ref, w2_ref, y_ref):
    del be_ref
    xb = x_ref[...].astype(BF16)
    a = _dot(xb, w1_ref[...].astype(BF16))
    hmid = a * _sigmoid(a) * _dot(xb, w3_ref[...].astype(BF16))
    y_ref[...] = _dot(hmid.astype(BF16), w2_ref[...].astype(BF16))


def _moe_experts(xs, blk_expert, w1, w3, w2):
    d = xs.shape[1]
    n_blocks = blk_expert.shape[0]
    de = w1.shape[-1]
    return pl.pallas_call(
        _moe_expert_kernel,
        grid_spec=pltpu.PrefetchScalarGridSpec(
            num_scalar_prefetch=1, grid=(n_blocks,),
            in_specs=[pl.BlockSpec((MOE_BLOCK, d), lambda i, be: (i, 0)),
                      pl.BlockSpec((None, None, d, de), lambda i, be: (0, be[i], 0, 0)),
                      pl.BlockSpec((None, None, d, de), lambda i, be: (0, be[i], 0, 0)),
                      pl.BlockSpec((None, None, de, d), lambda i, be: (0, be[i], 0, 0))],
            out_specs=pl.BlockSpec((MOE_BLOCK, d), lambda i, be: (i, 0))),
        out_shape=jax.ShapeDtypeStruct((n_blocks * MOE_BLOCK, d), F32),
        compiler_params=_cparams(("parallel",)),
        name="moe_experts",
    )(blk_expert, xs, w1, w3, w2)


def _moe_combine_kernel(dest_ref, nxt_ref, y_hbm, h_ref, rt_ref, g2_ref, b2_ref, o_ref, ybuf, sems, *, tm):
    i = pl.program_id(0)
    slot = i % 2

    def copies(ref, s, t):
        return [_row_copy(y_hbm, ybuf.at[s, k], sems.at[s], ref[0, 0, TOP_K * t + k], t) for k in range(TOP_K)]

    def start_all(ref, s):
        def body(t, c):
            for cp in copies(ref, s, t):
                cp.start()
            return c
        lax.fori_loop(0, tm, body, 0, unroll=DMA_UNROLL)

    @pl.when(i == 0)
    def _():
        start_all(dest_ref, 0)

    @pl.when(i + 1 < pl.num_programs(0))
    def _():
        start_all(nxt_ref, 1 - slot)

    def wait(t, c):
        for cp in copies(dest_ref, slot, t):
            cp.wait()
        return c

    lax.fori_loop(0, tm, wait, 0, unroll=DMA_UNROLL)
    y0 = ybuf[slot, 0]
    y1 = ybuf[slot, 1]
    rt = rt_ref[...]
    lane = lax.broadcasted_iota(I32, rt.shape, 1)
    g1 = jnp.sum(jnp.where(lane == 2, rt, 0.0), axis=1, keepdims=True)
    g2 = jnp.sum(jnp.where(lane == 3, rt, 0.0), axis=1, keepdims=True)
    moe = g1 * y0 + g2 * y1
    o_ref[...] = _layer_norm(DEEPNORM_ALPHA * h_ref[...] + moe, g2_ref[...], b2_ref[...])


def _moe_combine(y, dest, h, rt, ln_g, ln_b):
    n, d = h.shape
    tm = min(256, n)
    nt = n // tm
    row = lambda i: (i, 0)
    full = lambda i: (0, 0)
    dest3 = dest.reshape(nt, 1, TOP_K * tm)
    return pl.pallas_call(
        functools.partial(_moe_combine_kernel, tm=tm),
        grid=(nt,),
        in_specs=[pl.BlockSpec((1, 1, TOP_K * tm), lambda i: (i, 0, 0), memory_space=pltpu.SMEM),
                  pl.BlockSpec((1, 1, TOP_K * tm), lambda i: (jnp.minimum(i + 1, nt - 1), 0, 0),
                               memory_space=pltpu.SMEM),
                  pl.BlockSpec(memory_space=pl.ANY),
                  pl.BlockSpec((tm, d), row), pl.BlockSpec((tm, LANE), row),
                  pl.BlockSpec((1, d), full), pl.BlockSpec((1, d), full)],
        out_specs=pl.BlockSpec((tm, d), row),
        out_shape=jax.ShapeDtypeStruct((n, d), F32),
        scratch_shapes=[pltpu.VMEM((2, TOP_K, tm, d), F32), pltpu.SemaphoreType.DMA((2,))],
        compiler_params=_cparams(("arbitrary",)),
        name="moe_combine",
    )(dest3, dest3, y, h, rt, ln_g, ln_b)


def _moe(h, rt, pw, w1, w3, w2):
    n = h.shape[0]
    a = n * TOP_K
    rank, cnt = _moe_rank(rt)
    counts = cnt[0, :N_EXPERTS].astype(I32)
    padded = (counts + MOE_BLOCK - 1) // MOE_BLOCK * MOE_BLOCK
    pad_end = jnp.cumsum(padded)
    pad_start = pad_end - padded
    expert = rt[:, :TOP_K].astype(I32)
    first_slot = jnp.sum(jnp.where(expert[..., None] == jnp.arange(N_EXPERTS), pad_start, 0), axis=-1)
    dest = (first_slot + rank[:, :TOP_K].astype(I32)).reshape(-1)
    n_blocks = -(-a // MOE_BLOCK) + N_EXPERTS
    blk_start = jnp.arange(n_blocks) * MOE_BLOCK
    blk_expert = jnp.minimum(jnp.sum(pad_end[None, :] <= blk_start[:, None], axis=1), N_EXPERTS - 1).astype(I32)
    xs = _moe_dispatch(h, dest, n_blocks * MOE_BLOCK)
    y = _moe_experts(xs, blk_expert, w1, w3, w2)
    return _moe_combine(y, dest, h, rt, pw['ln2_g'], pw['ln2_b'])


def _qlat_kernel(q_ref, w_ref, o_ref):
    o_ref[...] = _dot(q_ref[...], w_ref[...]).astype(o_ref.dtype)


def _q_latent(qm, wabs):
    b = qm.shape[0]
    return pl.pallas_call(
        _qlat_kernel,
        grid=(MLA_HEADS,),
        in_specs=[pl.BlockSpec((b, LANE), lambda h: (0, h)),
                  pl.BlockSpec((None, LANE, MLA_KV_RANK), lambda h: (h, 0, 0))],
        out_specs=pl.BlockSpec((b, MLA_KV_RANK), lambda h: (0, h)),
        out_shape=jax.ShapeDtypeStruct((b, MLA_HEADS * MLA_KV_RANK), BF16),
        compiler_params=_cparams(("parallel",)),
        name="q_latent",
    )(qm, wabs)


def _mla_decode_kernel(pt_ref, ql_ref, qlt_ref, qr_ref, cn_ref, kn_ref, c_hbm, k_hbm, o_ref,
                       cbuf, kbuf, sems, m_scr, l_scr, acc_scr, *, n_pages, pp):
    slot = _prefetched_pages(pt_ref, (c_hbm, k_hbm), (cbuf, kbuf), sems, n_pages)
    qlt = qlt_ref[...]
    qr = qr_ref[...]
    cn = cn_ref[...].astype(BF16).astype(F32)
    kn = kn_ref[...].astype(BF16).astype(F32)
    m_scr[...] = (jnp.sum(ql_ref[...].astype(F32) * cn, axis=1, keepdims=True)
                  + jnp.sum(qr.astype(F32) * kn, axis=1, keepdims=True))
    l_scr[...] = jnp.ones_like(l_scr)
    acc_scr[...] = jnp.broadcast_to(cn, acc_scr.shape)

    def chunk(ci, carry):
        cs = [cbuf[slot, ci * pp + j].astype(BF16) for j in range(pp)]
        s = jnp.concatenate([_dot(c, qlt).T[:MLA_HEADS] + _dot(qr, kbuf[slot, ci * pp + j].astype(BF16))
                             for j, c in enumerate(cs)], axis=1)
        m_prev = m_scr[...]
        m_new = jnp.maximum(m_prev, jnp.max(s, axis=1, keepdims=True))
        alpha = jnp.exp2(m_prev - m_new)
        p = jnp.exp2(s - m_new)
        l_scr[...] = alpha * l_scr[...] + jnp.sum(p, axis=1, keepdims=True)
        pb = p.astype(BF16)
        acc = alpha * acc_scr[...]
        for j, c in enumerate(cs):
            acc = acc + _dot(pb[:, j * PAGE:(j + 1) * PAGE], c)
        acc_scr[...] = acc
        m_scr[...] = m_new
        return carry

    lax.fori_loop(0, n_pages // pp, chunk, 0)
    o_ref[...] = (acc_scr[...] / l_scr[...]).astype(o_ref.dtype)


def _mla_decode(qlat, qr, ckv_new, kr_new, cache_ckv, cache_kr_t, page_table):
    b, n_pages = page_table.shape
    pp = min(64, n_pages)
    r = MLA_KV_RANK
    ql3 = qlat.reshape(b, MLA_HEADS, r)
    qlt = jnp.pad(jnp.transpose(ql3, (0, 2, 1)), ((0, 0), (0, 0), (0, LANE - MLA_HEADS)))
    per_b = lambda bi, pt: (bi, 0, 0)
    return pl.pallas_call(
        functools.partial(_mla_decode_kernel, n_pages=n_pages, pp=pp),
        grid_spec=pltpu.PrefetchScalarGridSpec(
            num_scalar_prefetch=1, grid=(b,),
            in_specs=[pl.BlockSpec((None, MLA_HEADS, r), per_b),
                      pl.BlockSpec((None, r, LANE), per_b),
                      pl.BlockSpec((None, MLA_HEADS, MLA_ROPE), per_b),
                      pl.BlockSpec((None, 1, r), per_b),
                      pl.BlockSpec((None, 1, MLA_ROPE), per_b),
                      pl.BlockSpec(memory_space=pl.ANY),
                      pl.BlockSpec(memory_space=pl.ANY)],
            out_specs=pl.BlockSpec((None, MLA_HEADS, r), per_b),
            scratch_shapes=[pltpu.VMEM((2, n_pages, PAGE, r), F32), pltpu.VMEM((2, n_pages, MLA_ROPE, PAGE), F32),
                            pltpu.SemaphoreType.DMA((2,)),
                            pltpu.VMEM((MLA_HEADS, 1), F32), pltpu.VMEM((MLA_HEADS, 1), F32),
                            pltpu.VMEM((MLA_HEADS, r), F32)]),
        out_shape=jax.ShapeDtypeStruct((b, MLA_HEADS, r), BF16),
        compiler_params=_cparams(("arbitrary",)),
        name="mla_decode",
    )(page_table, ql3, qlt, qr, ckv_new.reshape(b, 1, r), kr_new.reshape(b, 1, MLA_ROPE),
      cache_ckv, cache_kr_t)


def _oproj_kernel(o_ref, w_ref, out_ref):
    out_ref[...] = (_dot(o_ref[:, :MLA_KV_RANK], w_ref[:, :LANE])
                    + _dot(o_ref[:, MLA_KV_RANK:], w_ref[:, LANE:])).astype(out_ref.dtype)


def _mla_out_up(o_lat, wuv):
    b = o_lat.shape[0]
    return pl.pallas_call(
        _oproj_kernel,
        grid=(MLA_HEADS // 2,),
        in_specs=[pl.BlockSpec((b, 2 * MLA_KV_RANK), lambda j: (0, j)),
                  pl.BlockSpec((MLA_KV_RANK, 2 * LANE), lambda j: (0, j))],
        out_specs=pl.BlockSpec((b, LANE), lambda j: (0, j)),
        out_shape=jax.ShapeDtypeStruct((b, 4 * LANE), BF16),
        compiler_params=_cparams(("parallel",)),
        name="mla_out_up",
    )(o_lat, wuv)


def _attend8(q8, variants, bias, ok):
    hrow = lax.broadcasted_iota(I32, (MLA_HEADS, 1), 0)
    vsel = (hrow >> GQA_SHIFT) * 2 + (hrow & 1)
    s = jnp.zeros(bias.shape, F32)
    for c, kv in enumerate(variants):
        s = s + jnp.where(vsel == c, _dot_nt(q8, kv), 0.0)
    p = _masked_softmax(s + bias, ok)
    pb = p.astype(BF16)
    o = jnp.zeros((MLA_HEADS, LANE), F32)
    for c, kv in enumerate(variants):
        o = o + jnp.where(vsel == c, _dot(pb, kv), 0.0)
    return o, p


def _slope_col():
    h = lax.broadcasted_iota(I32, (NSA_HEADS, 1), 0)
    return jnp.exp2(-(h + 1).astype(F32) * (8.0 / NSA_HEADS)) * LOG2E


def _attend_t(q8, kts, vts, bias, ok, new_row):
    hrow = lax.broadcasted_iota(I32, (NSA_HEADS, 1), 0)
    first = (hrow >> GQA_SHIFT) == 0
    s = jnp.where(first, _dot(q8, kts[0]), _dot(q8, kts[1])) + bias
    s = jnp.where(ok, s, NEG)
    new8 = jnp.broadcast_to(new_row, (NSA_HEADS, new_row.shape[1])).astype(BF16).astype(F32)
    kk, vv = new8[:, :LANE], new8[:, LANE:]
    k_new = jnp.where(first, kk, pltpu.roll(kk, HALF, 1))[:, :NSA_HD]
    v_new = jnp.where(first, vv, pltpu.roll(vv, HALF, 1))[:, :NSA_HD]
    s_new = jnp.sum(q8.astype(F32) * k_new, axis=1, keepdims=True)
    m = jnp.maximum(jnp.max(s, axis=1, keepdims=True), s_new)
    p = jnp.exp2(s - m)
    p_new = jnp.exp2(s_new - m)
    l = jnp.sum(p, axis=1, keepdims=True) + p_new
    pb = p.astype(BF16)
    o = jnp.where(first, _dot_nt(pb, vts[0]), _dot_nt(pb, vts[1])) + p_new * v_new
    return o / l


def _sample_cmp_win_kernel(qx_ref, q_ref, kc_ref, wmap_ref, st_ref, kvw_ref, newt_ref,
                           ocmp_ref, imp_ref, owin_ref, wout_ref, *, past_len):
    q8 = qx_ref[...]
    slope = _slope_col()
    nk = kc_ref.shape[0]
    kend = lax.broadcasted_iota(I32, (NSA_HEADS, nk), 1) * CMP_STRIDE + (CMP_BLOCK - 1)
    ok = kend <= past_len
    bias = -slope * (past_len - kend).astype(F32)
    kc = kc_ref[...]
    o, p = _attend8(q8, [kc[:, i * LANE:(i + 1) * LANE] for i in range(4)], bias, ok)
    ocmp_ref[...] = o.astype(ocmp_ref.dtype)
    hrow = lax.broadcasted_iota(I32, (NSA_HEADS, 1), 0)
    for kh in range(NSA_KV_HEADS):
        psum = jnp.sum(jnp.where((hrow >> GQA_SHIFT) == kh, p, 0.0), axis=0, keepdims=True)
        imp_ref[kh:kh + 1, :] = _split_dot(psum, wmap_ref[...])
    st = st_ref[...]
    wbuf = st.shape[1]
    stb = st.astype(BF16)
    j = lax.broadcasted_iota(I32, (NSA_HEADS, wbuf), 1)
    dist = wbuf - j
    okw = (dist <= WINDOW) & (past_len - dist >= 0)
    owin_ref[...] = _attend_t(q_ref[...], [stb[:NSA_HD], stb[NSA_HD:2 * NSA_HD]],
                              [stb[2 * NSA_HD:3 * NSA_HD], stb[3 * NSA_HD:]],
                              -slope * dist.astype(F32), okw, kvw_ref[...])
    newt = newt_ref[...]
    pick = lax.broadcasted_iota(I32, newt.shape, 1) == pl.program_id(0)
    new_col = jnp.sum(jnp.where(pick, newt, 0.0), axis=1, keepdims=True)
    nblk = wbuf // LANE
    rolled = [pltpu.roll(st[:, k * LANE:(k + 1) * LANE], LANE - 1, 1) for k in range(nblk)]
    keep = lax.broadcasted_iota(I32, (st.shape[0], LANE), 1) < LANE - 1
    for k in range(nblk):
        nxt = rolled[k + 1] if k + 1 < nblk else jnp.broadcast_to(new_col, rolled[k].shape)
        wout_ref[:, k * LANE:(k + 1) * LANE] = jnp.where(keep, rolled[k], nxt)


def _sample_cmp_win(qx, q8, kc_var, wmap, state_t, kvw, past_len):
    b = qx.shape[0]
    nk = kc_var.shape[0] // b
    wbuf = state_t.shape[2]
    wcols = wmap.shape[1]
    per_b3 = lambda bi: (bi, 0, 0)
    return pl.pallas_call(
        functools.partial(_sample_cmp_win_kernel, past_len=past_len),
        grid=(b,),
        in_specs=[pl.BlockSpec((None, NSA_HEADS, LANE), per_b3),
                  pl.BlockSpec((None, NSA_HEADS, NSA_HD), per_b3),
                  pl.BlockSpec((nk, 4 * LANE), lambda bi: (bi, 0)),
                  pl.BlockSpec(wmap.shape, lambda bi: (0, 0)),
                  pl.BlockSpec((None, NSA_KV_COLS, wbuf), per_b3),
                  pl.BlockSpec((None, 1, NSA_KV_COLS), per_b3),
                  pl.BlockSpec((NSA_KV_COLS, b), lambda bi: (0, 0))],
        out_specs=[pl.BlockSpec((None, NSA_HEADS, LANE), per_b3),
                   pl.BlockSpec((None, NSA_KV_HEADS, wcols), per_b3),
                   pl.BlockSpec((None, NSA_HEADS, NSA_HD), per_b3),
                   pl.BlockSpec((None, NSA_KV_COLS, wbuf), per_b3)],
        out_shape=[jax.ShapeDtypeStruct((b, NSA_HEADS, LANE), F32),
                   jax.ShapeDtypeStruct((b, NSA_KV_HEADS, wcols), F32),
                   jax.ShapeDtypeStruct((b, NSA_HEADS, NSA_HD), F32),
                   jax.ShapeDtypeStruct((b, NSA_KV_COLS, wbuf), F32)],
        compiler_params=_cparams(("parallel",)),
        name="sample_cmp_win",
    )(qx.reshape(b, NSA_HEADS, LANE), q8, kc_var, wmap, state_t, kvw.reshape(b, 1, NSA_KV_COLS),
      jnp.transpose(kvw))


def _sample_select_kernel(imp_ref, idx_ref, *, cur, n_slc):
    imp = imp_ref[...]
    rank, valid = _select_rank(imp, cur, n_slc)
    r, w = imp.shape
    blk = lax.broadcasted_iota(I32, (r, w), 1).astype(F32)
    lane = lax.broadcasted_iota(I32, (r, LANE), 1)
    out = jnp.zeros((r, LANE), F32)
    for t in range(SLC_TOP_N):
        pick = jnp.sum(jnp.where((rank == t) & valid, blk, 0.0), axis=1, keepdims=True)
        out = jnp.where(lane == t, pick, out)
    idx_ref[...] = out.astype(I32)


def _sample_select(imp, cur, n_slc):
    r = imp.shape[0]
    return pl.pallas_call(
        functools.partial(_sample_select_kernel, cur=cur, n_slc=n_slc),
        grid=(1,),
        in_specs=[pl.BlockSpec(imp.shape, lambda i: (0, 0))],
        out_specs=pl.BlockSpec((r, LANE), lambda i: (0, 0)),
        out_shape=jax.ShapeDtypeStruct((r, LANE), I32),
        compiler_params=_cparams(("arbitrary",)),
        name="sample_select",
    )(imp)


def _sample_slc_kernel(*refs, n_sel, past_len):
    blk_ref, page_ref = refs[0], refs[1]
    del page_ref
    q_ref, new_ref = refs[2], refs[3]
    c_refs = refs[4:4 + NSA_KV_HEADS * n_sel]
    o_ref = refs[4 + NSA_KV_HEADS * n_sel]
    bi = pl.program_id(0)
    hrow = lax.broadcasted_iota(I32, (NSA_HEADS, 1), 0)
    first = (hrow >> GQA_SHIFT) == 0
    nk = n_sel * PAGE
    col = lax.broadcasted_iota(I32, (NSA_HEADS, nk), 1)
    tile = col >> PAGE_SHIFT
    row = col & (PAGE - 1)
    kts, vts, blks = [], [], []
    for kh in range(NSA_KV_HEADS):
        mine = c_refs[kh * n_sel:(kh + 1) * n_sel]
        kts.append(jnp.concatenate([c[kh * NSA_HD:(kh + 1) * NSA_HD, :] for c in mine], axis=1).astype(BF16))
        vts.append(jnp.concatenate([c[(2 + kh) * NSA_HD:(3 + kh) * NSA_HD, :] for c in mine],
                                   axis=1).astype(BF16))
        blkv = jnp.zeros(col.shape, I32)
        for t in range(n_sel):
            blkv = jnp.where(tile == t, blk_ref[bi, kh * n_sel + t], blkv)
        blks.append(blkv)
    blkv = jnp.where(first, blks[0], blks[1])
    kpos = (blkv >> 1) * PAGE + row
    ok = ((row >> SLC_SHIFT) == (blkv & 1)) & (kpos < past_len)
    dist = past_len - kpos
    o_ref[...] = _attend_t(q_ref[...], kts, vts, -_slope_col() * dist.astype(F32), ok, new_ref[...])


def _sample_slc(q8, kvs_new, cache_slc_t, blk_idx, page_idx, past_len):
    b = q8.shape[0]
    n_sel = blk_idx.shape[1] // NSA_KV_HEADS
    per_b3 = lambda bi, blk, pg: (bi, 0, 0)
    in_specs = [pl.BlockSpec((None, NSA_HEADS, NSA_HD), per_b3),
                pl.BlockSpec((None, 1, NSA_KV_COLS), per_b3)]
    in_specs += [pl.BlockSpec((None, NSA_KV_COLS, PAGE),
                              functools.partial(lambda bi, blk, pg, j: (pg[bi, j], 0, 0), j=j))
                 for j in range(NSA_KV_HEADS * n_sel)]
    return pl.pallas_call(
        functools.partial(_sample_slc_kernel, n_sel=n_sel, past_len=past_len),
        grid_spec=pltpu.PrefetchScalarGridSpec(
            num_scalar_prefetch=2, grid=(b,), in_specs=in_specs,
            out_specs=pl.BlockSpec((None, NSA_HEADS, NSA_HD), per_b3)),
        out_shape=jax.ShapeDtypeStruct((b, NSA_HEADS, NSA_HD), F32),
        compiler_params=_cparams(("parallel",)),
        name="sample_slc",
    )(blk_idx, page_idx, q8, kvs_new.reshape(b, 1, NSA_KV_COLS), *([cache_slc_t] * (NSA_KV_HEADS * n_sel)))


def _heads_to_swapped(o8):
    b = o8.shape[0]
    return o8.reshape(b, NSA_HEADS // 2, 2, NSA_HD)[:, :, ::-1].reshape(b, NSA_HEADS * NSA_HD).astype(BF16)


def _pair_swap(o8):
    b = o8.shape[0]
    o = o8.reshape(b, NSA_HEADS // 2, 2, LANE)
    lo = jnp.arange(LANE) < HALF
    return jnp.where(lo, o[:, :, 1], o[:, :, 0]).reshape(b, 4 * LANE).astype(BF16)


def _rope_table(pos):
    half = MLA_ROPE // 2
    freqs = ROPE_THETA ** (-jnp.arange(half, dtype=F32) / half)
    ang = pos.astype(F32)[:, None] * freqs
    cos, sin = jnp.cos(ang), jnp.sin(ang)
    n = pos.shape[0]
    one, zero = jnp.ones((n, MLA_NOPE), F32), jnp.zeros((n, MLA_NOPE), F32)
    pad = jnp.zeros((n, LANE - MLA_NOPE - MLA_ROPE), F32)
    return jnp.concatenate([one, cos, cos, pad, zero, -sin, sin, pad], axis=1)


def _cmp_to_slc(n_cmp, n_slc, rows, cols):
    cs = np.arange(n_cmp)[:, None] * CMP_STRIDE
    ss = np.arange(n_slc)[None, :] * SLC_BLOCK
    inter = np.clip(np.minimum(cs + CMP_BLOCK, ss + SLC_BLOCK) - np.maximum(cs, ss), 0, None)
    w = np.zeros((rows, cols), np.float32)
    w[:n_cmp, :n_slc] = inter.astype(np.float32) / CMP_STRIDE
    return jnp.asarray(w, BF16)


def _prep_weights(w_in, mla_g_q, mla_g_kv, mla_w_uq, mla_w_uk, mla_w_uv, mla_w_o, nsa_pe_k, nsa_w1_k,
                  nsa_w2_k, nsa_pe_v, nsa_w1_v, nsa_w2_v, nsa_w_o, w_out, ln1_g, ln1_b, router_g_w,
                  router_g_b, router_e_w, router_e_b, ln2_g, ln2_b):
    d = w_in.shape[0]
    splits = (MLA_Q_RANK, MLA_KV_RANK, MLA_ROPE, NSA_HEADS * NSA_HD, NSA_KV_COLS, NSA_KV_COLS, NSA_KV_COLS,
              3 * NSA_HEADS, d, d)
    offs = np.cumsum(splits)[:-1].tolist()
    cq, ckv, kr, q, kvc, kvs, kvw, gn, ga, gb = jnp.split(w_in, offs, axis=1)
    r2 = MLA_ROPE // 2
    zc = lambda n: jnp.zeros((d, n), F32)
    tail = LANE - MLA_NOPE - MLA_ROPE
    kr_pad = jnp.concatenate([zc(MLA_NOPE), kr, zc(tail)], axis=1)
    kr_sw = jnp.concatenate([zc(MLA_NOPE), kr[:, r2:], kr[:, :r2], zc(tail)], axis=1)
    gn_pad = jnp.concatenate([gn, zc(LANE - gn.shape[1])], axis=1)
    w_all = jnp.concatenate([cq, ckv, q, kvc, kvs, kvw, kr_pad, kr_sw, gn_pad, ga, gb], axis=1).astype(BF16)

    def pad_heads(w, lo_cols):
        r, hh, c = w.shape
        out = jnp.zeros((r, hh, LANE), F32).at[:, :, lo_cols:lo_cols + c].set(w)
        return out

    nope, rope = mla_w_uq[:, :, :MLA_NOPE], mla_w_uq[:, :, MLA_NOPE:]
    wuq = jnp.concatenate([nope, rope, jnp.zeros(nope.shape[:2] + (tail,), F32)], axis=2)
    wuqs = jnp.concatenate([jnp.zeros_like(nope), rope[:, :, r2:], rope[:, :, :r2],
                            jnp.zeros(nope.shape[:2] + (tail,), F32)], axis=2)
    wuk = pad_heads(mla_w_uk, 0)
    odd = (jnp.arange(MLA_HEADS) % 2 == 1)[None, :, None]
    wuv = jnp.where(odd, pad_heads(mla_w_uv, HALF), pad_heads(mla_w_uv, 0))
    flat = lambda w: w.reshape(w.shape[0], -1).astype(BF16)
    wabs = jnp.zeros((MLA_HEADS, LANE, MLA_KV_RANK), F32).at[:, :MLA_NOPE, :].set(
        jnp.transpose(mla_w_uk, (1, 2, 0))).astype(BF16)

    def w1_cols(w1):
        return jnp.transpose(w1.reshape(2, CMP_STRIDE, NSA_HD, CMP_HIDDEN), (1, 2, 0, 3))

    wt = jnp.stack([w1_cols(nsa_w1_k), w1_cols(nsa_w1_v)])
    eye2 = jnp.eye(2, dtype=F32)
    w_big = jnp.einsum('tpdfh,tu,kv->ptkdfuvh', wt, eye2, eye2).reshape(
        CMP_STRIDE, NSA_KV_COLS, 2 * 2 * NSA_KV_HEADS * CMP_HIDDEN).astype(BF16)
    zpe = jnp.zeros((CMP_BLOCK * NSA_HD, CMP_HIDDEN), F32)
    w1_pe = jnp.concatenate([
        jnp.concatenate([nsa_w1_k, nsa_w1_k, zpe, zpe], axis=1),
        jnp.concatenate([zpe, zpe, nsa_w1_v, nsa_w1_v], axis=1)], axis=0)
    pe_flat = jnp.concatenate([nsa_pe_k.reshape(1, -1), nsa_pe_v.reshape(1, -1)], axis=1)
    w1_hi = w1_pe.astype(BF16)
    w1_lo = (w1_pe - w1_hi.astype(F32)).astype(BF16)
    z64 = jnp.zeros((CMP_HIDDEN, NSA_HD), F32)
    rows = [[nsa_w2_k, z64, z64, z64], [z64, nsa_w2_k, z64, z64], [z64, z64, nsa_w2_v, z64],
            [z64, z64, z64, nsa_w2_v]]
    w2_cmp = jnp.concatenate([jnp.concatenate(r, axis=1) for r in rows], axis=0).astype(BF16)

    perm = np.arange(NSA_HEADS * NSA_HD).reshape(NSA_HEADS // 2, 2, NSA_HD)[:, ::-1].reshape(-1)
    w_on = nsa_w_o[perm].astype(BF16)
    e3 = np.zeros((LANE, 3, NSA_HEADS * NSA_HD), np.float32)
    for h in range(NSA_HEADS):
        pos = (h // 2) * LANE + (0 if h % 2 else HALF)
        for j in range(3):
            e3[3 * h + j, j, pos:pos + NSA_HD] = 1.0
    e3 = jnp.asarray(e3.reshape(LANE, -1), BF16)
    w_r = jnp.concatenate([router_g_w, zc(HALF - N_GROUPS), router_e_w], axis=1)
    w_r_hi = w_r.astype(BF16)
    w_r_lo = (w_r - w_r_hi.astype(F32)).astype(BF16)
    b_r = jnp.concatenate([router_g_b, jnp.zeros((HALF - N_GROUPS,), F32), router_e_b])[None, :]
    return dict(
        w_all=w_all, wuq=flat(wuq), wuqs=flat(wuqs), wuk=flat(wuk), wuv=flat(wuv),
        gq=mla_g_q[None, :], gkv=mla_g_kv[None, :], wabs=wabs, w_big=w_big,
        pe_flat=pe_flat, w1_pe=jnp.stack([w1_hi, w1_lo]), w2_cmp=w2_cmp,
        w_o=mla_w_o.astype(BF16), w_on=w_on, w_out=w_out.astype(BF16), e3=e3,
        ln1_g=ln1_g[None, :], ln1_b=ln1_b[None, :], w_router=jnp.stack([w_r_hi, w_r_lo]), b_router=b_r,
        ln2_g=ln2_g[None, :], ln2_b=ln2_b[None, :])


def _prompt_layer(x, pw, pe, slopes, moe_w):
    b, s, d = x.shape
    n = b * s
    tm = min(256, s)
    z = _in_proj(x.reshape(n, d), _rope_table(jnp.arange(s)), pw, tm, s)
    o_mla = _flash(z['qm'], z['km'], z['vm'], slopes, b, s, mode='causal', kv_shared=False, alibi=False,
                   swap=False)
    nb = s // CMP_STRIDE
    fs = _cmp_fs_dense(z['kvc'], pw['w_big'])
    kc_var, kc_var_t = _cmp_finish(fs, nb, pe, pw)
    n_slc = -(-s // SLC_BLOCK)
    wmap_t = jnp.transpose(_cmp_to_slc(nb - 1, n_slc, nb, LANE))
    o_cmp, sel = _cmp_select(z['qn'], kc_var, kc_var_t, slopes, wmap_t, b, s, n_slc)
    emat_t = jnp.asarray((np.arange(s)[:, None] // SLC_BLOCK) == np.arange(LANE)[None, :], BF16)
    o_slc = _flash(z['qn'], z['kvsv'], z['kvsvt'], slopes, b, s, mode='select', kv_shared=True, alibi=True,
                   swap=True, selt=sel, emat_t=emat_t)
    o_win = _flash(z['qn'], z['kvwv'], z['kvwvt'], slopes, b, s, mode='window', kv_shared=True, alibi=True,
                   swap=True)
    h, rt = _merge(x.reshape(n, d), o_mla, o_cmp, o_slc, o_win, z, pw, min(512, s))
    y = _moe(h, rt, pw, *moe_w)
    return y.reshape(b, s, d), z


def _sample_layer(x, pw, pe, slopes, moe_w, cache_ckv, cache_kr_t, cache_cmp_t, cache_slc_t, state_t, page_table):
    b, t, d = x.shape
    n_pages = page_table.shape[1]
    past_len = n_pages * PAGE
    z = _in_proj(x.reshape(b, d), _rope_table(jnp.full((b,), past_len)), pw, b, b)
    qlat = _q_latent(z['qm'], pw['wabs'])
    qr = z['qm'].reshape(b, MLA_HEADS, LANE)[:, :, MLA_NOPE:MLA_NOPE + MLA_ROPE]
    o_lat = _mla_decode(qlat, qr, z['ckv'], z['kr'], cache_ckv, cache_kr_t, page_table)
    o_mla = _mla_out_up(o_lat.reshape(b, MLA_HEADS * MLA_KV_RANK), pw['wuv'])
    nb = past_len // CMP_STRIDE
    fs = _cmp_fs_paged(cache_cmp_t, page_table, pw['w_big'])
    kc_var, _ = _cmp_finish(fs, nb, pe, pw)
    n_slc = -(-(past_len + t) // SLC_BLOCK)
    assert n_slc >= SLC_TOP_N and past_len % SLC_BLOCK == 0
    wcols = -(-n_slc // LANE) * LANE
    wmap = _cmp_to_slc(nb - 1, n_slc, nb, wcols)
    q8 = z['qnat'].reshape(b, NSA_HEADS, NSA_HD)
    o_cmp8, imp, o_win8, win_out = _sample_cmp_win(z['qn'], q8, kc_var, wmap, state_t, z['kvw'], past_len)
    cur = past_len // SLC_BLOCK
    idx = _sample_select(imp.reshape(b * NSA_KV_HEADS, wcols), cur, n_slc)[:, :SLC_TOP_N]
    blk = idx.reshape(b, NSA_KV_HEADS * SLC_TOP_N)
    safe = jnp.minimum(blk, past_len // SLC_BLOCK - 1)
    pages = jnp.take_along_axis(page_table, safe // (PAGE // SLC_BLOCK), axis=1).astype(I32)
    o_slc8 = _sample_slc(q8, z['kvs'], cache_slc_t, blk, pages, past_len)
    h, rt = _merge(x.reshape(b, d), o_mla, _pair_swap(o_cmp8), _heads_to_swapped(o_slc8),
                   _heads_to_swapped(o_win8), z, pw, b)
    y = _moe(h, rt, pw, *moe_w)
    return y.reshape(b, t, d), z, win_out


def kernel(x_prompt, x_sample, cache_mla_ckv, cache_mla_krope, cache_nsa_cmp_kv, cache_nsa_slc_kv,
           state_nsa_win_kv, page_table, w_in, mla_g_q, mla_g_kv, mla_w_uq, mla_w_uk, mla_w_uv, mla_w_o,
           nsa_pe_k, nsa_w1_k, nsa_w2_k, nsa_pe_v, nsa_w1_v, nsa_w2_v, nsa_w_o, w_out, ln1_g, ln1_b,
           router_g_w, router_g_b, router_e_w, router_e_b, moe_w1, moe_w3, moe_w2, ln2_g, ln2_b):
    assert w_in.shape[0] == DEPTH and x_sample.shape[1] == 1
    b, s, d = x_prompt.shape
    bs = x_sample.shape[0]
    kv_shape = (2, NSA_KV_HEADS, NSA_HD)
    pw = _prep_weights(w_in[0], mla_g_q[0], mla_g_kv[0], mla_w_uq[0], mla_w_uk[0], mla_w_uv[0], mla_w_o[0],
                       nsa_pe_k[0], nsa_w1_k[0], nsa_w2_k[0], nsa_pe_v[0], nsa_w1_v[0], nsa_w2_v[0],
                       nsa_w_o[0], w_out[0], ln1_g[0], ln1_b[0], router_g_w[0], router_g_b[0],
                       router_e_w[0], router_e_b[0], ln2_g[0], ln2_b[0])
    slopes = 2.0 ** (-8.0 * jnp.arange(1, NSA_HEADS + 1, dtype=F32) / NSA_HEADS)
    moe_w = (moe_w1, moe_w3, moe_w2)
    pe = _cmp_pe(pw)
    y_p, zp = _prompt_layer(x_prompt, pw, pe, slopes, moe_w)
    def rows_last(c):
        return jnp.transpose(c, (0, 2, 3, 4, 1)).reshape(c.shape[0], NSA_KV_COLS, c.shape[1])

    y_s, zs, win_t = _sample_layer(
        x_sample, pw, pe, slopes, moe_w, cache_mla_ckv[0], jnp.transpose(cache_mla_krope[0], (0, 2, 1)),
        rows_last(cache_nsa_cmp_kv[0]), rows_last(cache_nsa_slc_kv[0]), rows_last(state_nsa_win_kv[0]),
        page_table)
    def rows_first(t):
        return jnp.transpose(t.reshape((t.shape[0],) + kv_shape + (-1,)), (0, 4, 1, 2, 3))[None]

    wp = min(WINDOW, s)
    return (y_p, y_s,
            zp['ckv'].reshape(1, b, s, MLA_KV_RANK), zs['ckv'].reshape(1, bs, 1, MLA_KV_RANK),
            zp['kr'].reshape(1, b, s, MLA_ROPE), zs['kr'].reshape(1, bs, 1, MLA_ROPE),
            rows_first(zp['kvct']), zs['kvc'].reshape((1, bs, 1) + kv_shape),
            rows_first(zp['kvst']), zs['kvs'].reshape((1, bs, 1) + kv_shape),
            rows_first(zp['kvwt'][:, :, s - wp:]), rows_first(win_t))
```
